```python
import math
import jax, jax.numpy as jnp
from jax import lax
import numpy as np

D_MODEL = 1024
BATCH = 2
SEQ = 16384
DEPTH = 1

DA_HEADS = 8
DA_HEAD_DIM = 64
DA_V_DIM = 2 * DA_HEAD_DIM
DA_QK = DA_HEADS * 2 * DA_HEAD_DIM
DA_WIDTH = DA_HEADS * DA_V_DIM
DN_HEADS = 8
DN_KEY_DIM = 128
DN_VAL_DIM = 128
DN_QK = DN_HEADS * DN_KEY_DIM
DN_WIDTH = DN_HEADS * DN_VAL_DIM
DN_CONV = 4
DN_CHUNK = 64
MEM_LEN = 256
CA_HEADS = 4
CA_HEAD_DIM = 256
CA_WIDTH = CA_HEADS * CA_HEAD_DIM
N_BRANCH = 3
BRANCH_WIDTH = 1024
D_FF = 4 * D_MODEL
REL_BUCKETS = 32
REL_MAX_EXACT = 16
REL_MAX_DIST = 128
Q_BLOCK = 128
EPS = 1e-6

IN_SPLITS = (DA_QK, DA_QK, DA_WIDTH,
             DN_QK, DN_QK, DN_WIDTH, DN_WIDTH,
             DN_HEADS, DN_HEADS,
             CA_WIDTH,
             N_BRANCH * D_MODEL)
IN_WIDTH = sum(IN_SPLITS)
SPLIT_IDX = tuple(int(i) for i in np.cumsum(IN_SPLITS)[:-1])

kernel_name = "hybrid_diffattn_gdn_memxattn_gated_merge"


def rms_norm(x, g):
    xf = x.astype(jnp.float32)
    y = xf * lax.rsqrt(jnp.mean(xf * xf, axis=-1, keepdims=True) + EPS)
    return (y * g.astype(jnp.float32)).astype(x.dtype)


def l2norm(x):
    return x * lax.rsqrt(jnp.sum(x * x, axis=-1, keepdims=True) + EPS)


def lambda_init(layer):
    return 0.8 - 0.6 * math.exp(-0.3 * layer)


def t5_causal_bucket(dist):
    nf = jnp.maximum(dist, 1).astype(jnp.float32)
    large = REL_MAX_EXACT + (jnp.log(nf / REL_MAX_EXACT) / math.log(REL_MAX_DIST / REL_MAX_EXACT)
                             * (REL_BUCKETS - REL_MAX_EXACT)).astype(jnp.int32)
    large = jnp.minimum(large, REL_BUCKETS - 1)
    return jnp.where(dist < REL_MAX_EXACT, dist, large)


def diff_attention(q, k, v, rel_bias, lam_vecs, subln_g, lam_init):
    B, S = q.shape[0], q.shape[1]
    nb = S // Q_BLOCK
    scale = DA_HEAD_DIM ** -0.5
    lv = lam_vecs.astype(jnp.float32)
    lam = jnp.exp(jnp.sum(lv[0] * lv[1])) - jnp.exp(jnp.sum(lv[2] * lv[3])) + lam_init
    kpos = jnp.arange(S, dtype=jnp.int32)
    qb = q.reshape(B, nb, Q_BLOCK, DA_HEADS, 2, DA_HEAD_DIM).transpose(1, 0, 2, 3, 4, 5)
    starts = jnp.arange(nb, dtype=jnp.int32) * Q_BLOCK

    def block(args):
        qi, start = args
        s = jnp.einsum('bqhmd,bkhmd->bhmqk', qi, k,
                       preferred_element_type=jnp.float32) * scale
        dist = (start + jnp.arange(Q_BLOCK, dtype=jnp.int32))[:, None] - kpos[None, :]
        bias = rel_bias.astype(jnp.float32)[t5_causal_bucket(jnp.maximum(dist, 0))]
        s = s + bias.transpose(2, 0, 1)[None, :, None]
        s = jnp.where(dist >= 0, s, -jnp.inf)
        p = jax.nn.softmax(s, axis=-1)
        w = p[:, :, 0] - lam * p[:, :, 1]
        return jnp.einsum('bhqk,bkhe->bqhe', w.astype(v.dtype), v)

    o = lax.map(block, (qb, starts))
    o = o.transpose(1, 0, 2, 3, 4).reshape(B, S, DA_HEADS, DA_V_DIM)
    o = rms_norm(o, subln_g) * (1.0 - lam_init)
    return o.reshape(B, S, DA_WIDTH)


def short_conv(x, w):
    C = x.shape[-1]
    y = lax.conv_general_dilated(x, w[:, None, :].astype(x.dtype), window_strides=(1,),
                                 padding=[(DN_CONV - 1, 0)],
                                 dimension_numbers=('NWC', 'WIO', 'NWC'),
                                 feature_group_count=C)
    return jax.nn.silu(y)


def gated_delta_rule(q, k, v, g, beta):
    B, S, H, dk = q.shape
    dv = v.shape[-1]
    C = DN_CHUNK
    N = S // C
    f32 = jnp.float32
    out_dtype = v.dtype

    def chunk(t):
        return jnp.moveaxis(t.reshape((B, N, C, H) + t.shape[3:]), 3, 1)

    q = chunk(l2norm(q.astype(f32)) * dk ** -0.5)
    k = chunk(l2norm(k.astype(f32)))
    v = chunk(v.astype(f32))
    g = chunk(g.astype(f32))
    beta = chunk(beta.astype(f32))
    G = jnp.cumsum(g, axis=-1)
    idx = jnp.arange(C)
    causal = idx[:, None] >= idx[None, :]
    strict = idx[:, None] > idx[None, :]
    decay = jnp.exp(jnp.where(causal, G[..., :, None] - G[..., None, :], -jnp.inf))
    k_beta = k * beta[..., None]
    v_beta = v * beta[..., None]
    lower = jnp.where(strict, jnp.einsum('bhnid,bhnjd->bhnij', k_beta, k) * decay, 0.0)
    rhs = jnp.concatenate([v_beta, k_beta * jnp.exp(G)[..., None]], axis=-1)
    sol = lax.linalg.triangular_solve(lower + jnp.eye(C, dtype=f32), rhs, left_side=True,
                                      lower=True, unit_diagonal=True)
    u, w = sol[..., :dv], sol[..., dv:]
    intra = jnp.where(causal, jnp.einsum('bhnid,bhnjd->bhnij', q, k) * decay, 0.0)
    q_dec = q * jnp.exp(G)[..., None]
    k_tail = k * jnp.exp(G[..., -1:] - G)[..., None]
    decay_last = jnp.exp(G[..., -1])

    def step(state, xs):
        qd, kt, u_c, w_c, a_c, dl = xs
        v_new = u_c - jnp.einsum('bhcd,bhde->bhce', w_c, state)
        o = jnp.einsum('bhcd,bhde->bhce', qd, state) + jnp.einsum('bhij,bhje->bhie', a_c, v_new)
        state = state * dl[..., None, None] + jnp.einsum('bhcd,bhce->bhde', kt, v_new)
        return state, o

    xs = tuple(jnp.moveaxis(a, 2, 0) for a in (q_dec, k_tail, u, w, intra, decay_last))
    _, o = lax.scan(step, jnp.zeros((B, H, dk, dv), f32), xs)
    return o.transpose(1, 0, 3, 2, 4).reshape(B, S, H, dv).astype(out_dtype)


def memory_attention(q, mk, mv):
    s = jnp.einsum('bshd,bmhd->bhsm', q, mk, preferred_element_type=jnp.float32) * CA_HEAD_DIM ** -0.5
    p = jax.nn.softmax(s, axis=-1)
    o = jnp.einsum('bhsm,bmhd->bshd', p.astype(mv.dtype), mv)
    return o.reshape(q.shape[0], q.shape[1], CA_WIDTH)


def setup_inputs(seed: int = 0) -> dict:
    key = jax.random.key(seed)
    ks = jax.random.split(key, 24)
    nrm = jax.random.normal
    f32 = jnp.float32
    return {
        "x": nrm(ks[0], (BATCH, SEQ, D_MODEL), f32),
        "mem": nrm(ks[1], (BATCH, MEM_LEN, D_MODEL), f32),
        "rel_bias": 0.5 * nrm(ks[2], (REL_BUCKETS, DA_HEADS), f32),
        "w_in": nrm(ks[3], (DEPTH, D_MODEL, IN_WIDTH), f32) * D_MODEL ** -0.5,
        "conv_w": nrm(ks[4], (DEPTH, DN_CONV, 2 * DN_QK + DN_WIDTH), f32) * DN_CONV ** -0.5,
        "dn_a_log": jnp.log(jax.random.uniform(ks[5], (DEPTH, DN_HEADS), f32, 1.0, 16.0)),
        "dn_dt_bias": 0.1 * nrm(ks[6], (DEPTH, DN_HEADS), f32),
        "dn_norm_g": 1.0 + 0.1 * nrm(ks[7], (DEPTH, DN_VAL_DIM), f32),
        "da_lambda": 0.1 * nrm(ks[8], (DEPTH, 4, DA_HEAD_DIM), f32),
        "da_subln_g": 1.0 + 0.1 * nrm(ks[9], (DEPTH, DA_V_DIM), f32),
        "mem_norm_g": 1.0 + 0.1 * nrm(ks[10], (DEPTH, D_MODEL), f32),
        "w_mem_kv": nrm(ks[11], (DEPTH, D_MODEL, 2 * CA_WIDTH), f32) * D_MODEL ** -0.5,
        "w_branch": nrm(ks[12], (DEPTH, N_BRANCH, BRANCH_WIDTH, D_MODEL), f32) * BRANCH_WIDTH ** -0.5,
        "w_out": nrm(ks[13], (DEPTH, D_MODEL, D_MODEL), f32) * D_MODEL ** -0.5,
        "ln_mix_pre": 1.0 + 0.1 * nrm(ks[14], (DEPTH, D_MODEL), f32),
        "ln_mix_post": 1.0 + 0.1 * nrm(ks[15], (DEPTH, D_MODEL), f32),
        "ln_ff_pre": 1.0 + 0.1 * nrm(ks[16], (DEPTH, D_MODEL), f32),
        "ln_ff_post": 1.0 + 0.1 * nrm(ks[17], (DEPTH, D_MODEL), f32),
        "w_ff1": nrm(ks[18], (DEPTH, D_MODEL, D_FF), f32) * D_MODEL ** -0.5,
        "w_ff2": nrm(ks[19], (DEPTH, D_FF, D_MODEL), f32) * D_FF ** -0.5,
    }


def reference(x, mem, rel_bias, w_in, conv_w, dn_a_log, dn_dt_bias, dn_norm_g, da_lambda,
              da_subln_g, mem_norm_g, w_mem_kv, w_branch, w_out, ln_mix_pre, ln_mix_post,
              ln_ff_pre, ln_ff_post, w_ff1, w_ff2):
    B, S = x.shape[0], x.shape[1]
    h = x
    for l in range(DEPTH):
        n = rms_norm(h, ln_mix_pre[l])
        (da_q, da_k, da_v, dn_q, dn_k, dn_v, dn_z, dn_b, dn_a, ca_q, gate_logits) = [
            n @ wp for wp in jnp.split(w_in[l], SPLIT_IDX, axis=-1)]

        y_a = diff_attention(da_q.reshape(B, S, DA_HEADS, 2, DA_HEAD_DIM),
                             da_k.reshape(B, S, DA_HEADS, 2, DA_HEAD_DIM),
                             da_v.reshape(B, S, DA_HEADS, DA_V_DIM),
                             rel_bias, da_lambda[l], da_subln_g[l], lambda_init(l))

        qkv = short_conv(jnp.concatenate([dn_q, dn_k, dn_v], axis=-1), conv_w[l])
        cq, ck, cv = jnp.split(qkv, (DN_QK, 2 * DN_QK), axis=-1)
        beta = jax.nn.sigmoid(dn_b.astype(jnp.float32))
        g = -jnp.exp(dn_a_log[l].astype(jnp.float32)) * jax.nn.softplus(
            dn_a.astype(jnp.float32) + dn_dt_bias[l].astype(jnp.float32))
        o_dn = gated_delta_rule(cq.reshape(B, S, DN_HEADS, DN_KEY_DIM),
                                ck.reshape(B, S, DN_HEADS, DN_KEY_DIM),
                                cv.reshape(B, S, DN_HEADS, DN_VAL_DIM), g, beta)
        y_b = (rms_norm(o_dn, dn_norm_g[l])
               * jax.nn.silu(dn_z.reshape(B, S, DN_HEADS, DN_VAL_DIM))).reshape(B, S, DN_WIDTH)

        mkv = rms_norm(mem, mem_norm_g[l]) @ w_mem_kv[l]
        mk, mv = jnp.split(mkv, 2, axis=-1)
        Mn = mem.shape[1]
        y_c = memory_attention(ca_q.reshape(B, S, CA_HEADS, CA_HEAD_DIM),
                               mk.reshape(B, Mn, CA_HEADS, CA_HEAD_DIM),
                               mv.reshape(B, Mn, CA_HEADS, CA_HEAD_DIM))

        ys = jnp.stack([y_a, y_b, y_c], axis=2)
        proj = jnp.einsum('bsiw,iwd->bsid', ys, w_branch[l])
        gates = jax.nn.sigmoid(gate_logits.reshape(B, S, N_BRANCH, D_MODEL))
        mixed = jnp.sum(gates * proj, axis=2)
        h = h + rms_norm(mixed @ w_out[l], ln_mix_post[l])

        n = rms_norm(h, ln_ff_pre[l])
        f = jnp.square(jax.nn.relu(n @ w_ff1[l])) @ w_ff2[l]
        h = h + rms_norm(f, ln_ff_post[l])
    return h
```

```python
import functools
import math

import numpy as np
import jax
import jax.numpy as jnp
from jax import lax
from jax.experimental import pallas as pl
from jax.experimental.pallas import tpu as pltpu

F32 = jnp.float32
BF16 = jnp.bfloat16

EPS = 1e-6
MASK_VALUE = -1e30
LANES = 128
SUBLANES = 8
VMEM_LIMIT_BYTES = 56 * 1024 * 1024

DA_HEADS = 8
DA_HEAD_DIM = 64
DN_HEADS = 8
DN_DIM = 128
DN_CONV = 4
DN_CHUNK = 64
DN_PAIR = 2 * DN_CHUNK
CA_HEADS = 4
CA_HEAD_DIM = 256
REL_BUCKETS = 32
REL_MAX_EXACT = 16
REL_MAX_DIST = 128

ATTN_TILE = 256


def _params(*sem):
    return pltpu.CompilerParams(dimension_semantics=sem, vmem_limit_bytes=VMEM_LIMIT_BYTES)


def _mm(a, b):
    return jnp.dot(a.astype(BF16), b.astype(BF16), preferred_element_type=F32)


def _mm_nt(a, b):
    return lax.dot_general(a.astype(BF16), b.astype(BF16), (((1,), (1,)), ((), ())),
                           preferred_element_type=F32)


def _mm_exact_rhs(m01, a):
    hi = a.astype(BF16)
    r1 = a - hi.astype(F32)
    mid = r1.astype(BF16)
    lo = (r1 - mid.astype(F32)).astype(BF16)
    dot = functools.partial(jnp.dot, preferred_element_type=F32)
    return dot(m01, hi) + dot(m01, mid) + dot(m01, lo)


def _rms(x, axis):
    return x * lax.rsqrt(jnp.mean(x * x, axis=axis, keepdims=True) + EPS)


def _sigmoid(x):
    return 1.0 / (1.0 + jnp.exp(-x))


def _silu(x):
    return x * _sigmoid(x)


def _norm_matmul_body(x_ref, g_ref, w_ref, o_ref, n_ref):
    @pl.when(pl.program_id(1) == 0)
    def _():
        n_ref[...] = (_rms(x_ref[...], -1) * g_ref[...]).astype(BF16)

    o_ref[...] = jnp.dot(n_ref[...], w_ref[...], preferred_element_type=F32).astype(o_ref.dtype)


def _norm_matmul(x2d, g_row, w, col_block_offset, n_cols, out_dtype, tm, tn):
    t, d = x2d.shape
    return pl.pallas_call(
        _norm_matmul_body,
        grid=(t // tm, n_cols // tn),
        in_specs=[pl.BlockSpec((tm, d), lambda i, j: (i, 0)),
                  pl.BlockSpec((1, d), lambda i, j: (0, 0)),
                  pl.BlockSpec((d, tn), lambda i, j: (0, j + col_block_offset))],
        out_specs=pl.BlockSpec((tm, tn), lambda i, j: (i, j)),
        out_shape=jax.ShapeDtypeStruct((t, n_cols), out_dtype),
        scratch_shapes=[pltpu.VMEM((tm, d), BF16)],
        compiler_params=_params("parallel", "arbitrary"),
        name="norm_matmul",
    )(x2d, g_row, w)


def _bucket_tiles(tile):
    k = np.arange(tile)[:, None]
    q = np.arange(tile)[None, :]

    def bucket(dist):
        nf = np.maximum(dist, 1).astype(np.float64)
        large = REL_MAX_EXACT + np.trunc(
            np.log(nf / REL_MAX_EXACT) / math.log(REL_MAX_DIST / REL_MAX_EXACT)
            * (REL_BUCKETS - REL_MAX_EXACT)).astype(np.int64)
        large = np.minimum(large, REL_BUCKETS - 1)
        return np.where(dist < REL_MAX_EXACT, dist, large)

    diag = np.where(q >= k, bucket(np.maximum(q - k, 0)), -1)
    prev = bucket(q - k + tile)
    return np.stack([diag, prev]).astype(np.int32)


def _bias_tile_body(rb_ref, bucket_ref, o_ref):
    h = pl.program_id(0)
    far = rb_ref[REL_BUCKETS - 1, h]
    bk = bucket_ref[...]
    out = jnp.full(bk.shape, MASK_VALUE, F32)
    for b in range(REL_BUCKETS):
        out = jnp.where(bk == b, rb_ref[b, h] - far, out)
    o_ref[...] = out


def _bias_tiles(rel_bias, tile):
    heads = rel_bias.shape[1]
    buckets = jnp.asarray(_bucket_tiles(tile))
    return pl.pallas_call(
        _bias_tile_body,
        grid=(heads,),
        in_specs=[pl.BlockSpec(memory_space=pltpu.SMEM),
                  pl.BlockSpec((2, tile, tile), lambda h: (0, 0, 0))],
        out_specs=pl.BlockSpec((None, 2, tile, tile), lambda h: (h, 0, 0, 0)),
        out_shape=jax.ShapeDtypeStruct((heads, 2, tile, tile), F32),
        compiler_params=_params("arbitrary"),
        name="bias_tiles",
    )(rel_bias, buckets)


def _attn_body(lam_ref, qt_ref, k_ref, vt_ref, bias_ref, g_ref, o_ref,
               qz_ref, m_ref, l_ref, acc_ref, *, tile, lam_init):
    qi = pl.program_id(2)

    qt = qt_ref[...]
    row = lax.broadcasted_iota(jnp.int32, qt.shape, 0)
    zero = jnp.zeros_like(qt)
    qz_ref[:, :tile] = jnp.where(row < DA_HEAD_DIM, qt, zero)
    qz_ref[:, tile:] = jnp.where(row >= DA_HEAD_DIM, qt, zero)
    m_ref[...] = jnp.full(m_ref.shape, MASK_VALUE, F32)
    l_ref[...] = jnp.zeros(l_ref.shape, F32)
    acc_ref[...] = jnp.zeros(acc_ref.shape, F32)

    def key_tile(koff, bias):
        s = jnp.dot(k_ref[pl.ds(koff, tile), :], qz_ref[...], preferred_element_type=F32)
        if bias is not None:
            s = s + jnp.concatenate([bias, bias], axis=1)
        m_old = m_ref[...]
        m_new = jnp.maximum(m_old, jnp.max(s, axis=0, keepdims=True))
        alpha = jnp.exp(m_old - m_new)
        p = jnp.exp(s - m_new)
        l_ref[...] = alpha * l_ref[...] + jnp.sum(p, axis=0, keepdims=True)
        pv = jnp.dot(vt_ref[:, pl.ds(koff, tile)], p.astype(BF16), preferred_element_type=F32)
        acc_ref[...] = alpha * acc_ref[...] + pv
        m_ref[...] = m_new

    def far_tile(ki, carry):
        key_tile(pl.multiple_of(ki * tile, tile), None)
        return carry

    lax.fori_loop(0, jnp.maximum(qi - 1, 0), far_tile, 0)

    @pl.when(qi >= 1)
    def _():
        key_tile(pl.multiple_of((qi - 1) * tile, tile), bias_ref[1])

    key_tile(pl.multiple_of(qi * tile, tile), bias_ref[0])

    lv = lam_ref[...]
    lam = (jnp.exp(jnp.sum(lv[0:1] * lv[1:2], axis=-1, keepdims=True))
           - jnp.exp(jnp.sum(lv[2:3] * lv[3:4], axis=-1, keepdims=True)) + lam_init)
    acc = acc_ref[...] * (1.0 / l_ref[...])
    ot = acc[:, :tile] - lam * acc[:, tile:]
    o = _rms(ot, 0).T * g_ref[...] * (1.0 - lam_init)
    o_ref[...] = o.astype(o_ref.dtype)


def _diff_attention(lam_vecs, qt, da, vt, bias, g_row, batch, seq, lam_init):
    tile = ATTN_TILE
    nq = seq // tile
    hd = 2 * DA_HEAD_DIM
    k_col0 = DA_HEADS
    return pl.pallas_call(
        functools.partial(_attn_body, tile=tile, lam_init=lam_init),
        grid=(batch, DA_HEADS, nq),
        in_specs=[pl.BlockSpec((4, DA_HEAD_DIM), lambda b, h, i: (0, 0)),
                  pl.BlockSpec((None, None, hd, tile), lambda b, h, i: (b, h, 0, i)),
                  pl.BlockSpec((seq, hd), lambda b, h, i: (b, k_col0 + h)),
                  pl.BlockSpec((None, None, hd, seq), lambda b, h, i: (b, h, 0, 0)),
                  pl.BlockSpec((None, 2, tile, tile), lambda b, h, i: (h, 0, 0, 0)),
                  pl.BlockSpec((1, hd), lambda b, h, i: (0, 0))],
        out_specs=pl.BlockSpec((tile, hd), lambda b, h, i: (b * nq + i, h)),
        out_shape=jax.ShapeDtypeStruct((batch * seq, DA_HEADS * hd), BF16),
        scratch_shapes=[pltpu.VMEM((hd, 2 * tile), BF16),
                        pltpu.VMEM((1, 2 * tile), F32),
                        pltpu.VMEM((1, 2 * tile), F32),
                        pltpu.VMEM((hd, 2 * tile), F32)],
        compiler_params=_params("parallel", "parallel", "arbitrary"),
        name="diff_attn",
    )(lam_vecs, qt, da, vt, bias, g_row)


def _shift_rows(x, prev, s):
    if s == 0:
        return x
    rolled = pltpu.roll(x, s, 0)
    prolled = pltpu.roll(prev, s, 0)
    r8 = lax.broadcasted_iota(jnp.int32, prev.shape, 0)
    top = jnp.where(r8 < s, prolled, rolled[:SUBLANES])
    return jnp.concatenate([top, rolled[SUBLANES:]], axis=0)


def _gdn_body(q_ref, k_ref, v_ref, z_ref, ba_ref, cwq_ref, cwk_ref, cwv_ref, alog_ref, dtb_ref,
              gn_ref, o_ref, state_ref, tail_ref):
    p = DN_PAIR
    c = DN_CHUNK

    @pl.when(pl.program_id(1) == 0)
    def _():
        state_ref[...] = jnp.zeros(state_ref.shape, F32)
        tail_ref[...] = jnp.zeros(tail_ref.shape, F32)

    ri = lax.broadcasted_iota(jnp.int32, (p, p), 0)
    ci = lax.broadcasted_iota(jnp.int32, (p, p), 1)
    same = (ri // c) == (ci // c)
    causal = same & (ri >= ci)
    strict = same & (ri > ci)
    cum01 = jnp.where(causal, 1.0, 0.0).astype(BF16)

    ba = ba_ref[...]
    beta_all = _sigmoid(ba)
    xs = ba + dtb_ref[...]
    softplus = jnp.maximum(xs, 0.0) + jnp.log(1.0 + jnp.exp(-jnp.abs(xs)))
    g_all = -jnp.exp(alog_ref[...]) * softplus

    def conv(x_ref, w_ref, which, h):
        cols = slice(h * DN_DIM, (h + 1) * DN_DIM)
        x = x_ref[:, cols]
        prev = tail_ref[which, :, cols]
        w = w_ref[:, cols]
        y = w[DN_CONV - 1:DN_CONV] * x
        for s in range(1, DN_CONV):
            y = y + w[DN_CONV - 1 - s:DN_CONV - s] * _shift_rows(x, prev, s)
        tail_ref[which, :, cols] = x[p - SUBLANES:]
        return _silu(y)

    for h in range(DN_HEADS):
        cols = slice(h * DN_DIM, (h + 1) * DN_DIM)
        q = conv(q_ref, cwq_ref, 0, h)
        k = conv(k_ref, cwk_ref, 1, h)
        v = conv(v_ref, cwv_ref, 2, h)
        q = q * lax.rsqrt(jnp.sum(q * q, axis=-1, keepdims=True) + EPS) * (DN_DIM ** -0.5)
        k = k * lax.rsqrt(jnp.sum(k * k, axis=-1, keepdims=True) + EPS)

        beta = jnp.broadcast_to(beta_all[:, h:h + 1], (p, DN_DIM))
        g = jnp.broadcast_to(g_all[:, DN_HEADS + h:DN_HEADS + h + 1], (p, p))
        gc = _mm_exact_rhs(cum01, g)
        gr = gc.T
        glast = jnp.concatenate(
            [jnp.broadcast_to(gc[c - 1:c], (c, p)), jnp.broadcast_to(gc[p - 1:p], (c, p))], axis=0)
        decay = jnp.exp(jnp.where(causal, gc - gr, MASK_VALUE))
        exp_g = jnp.exp(gc)

        kb = k * beta
        vb = v * beta
        low = jnp.where(strict, _mm_nt(kb, k) * decay, 0.0)
        y = -low
        pw = low
        for _ in range(5):
            pw = _mm(pw, pw)
            y = y + pw + _mm(y, pw)
        rhs = jnp.concatenate([vb, kb * exp_g], axis=1)
        sol = rhs + _mm(y, rhs)
        u = sol[:, :DN_DIM]
        w = sol[:, DN_DIM:]
        intra = jnp.where(causal, _mm_nt(q, k) * decay, 0.0)
        q_dec = q * exp_g
        kt_t = (k * jnp.exp(glast - gc)).T
        dl = jnp.exp(glast)

        outs = []
        zeros = jnp.zeros((c, DN_DIM), F32)
        for ch in range(2):
            rows = slice(ch * c, (ch + 1) * c)
            st = state_ref[h]
            v_new = u[rows] - _mm(w[rows], st)
            v_pad = jnp.concatenate([v_new, zeros] if ch == 0 else [zeros, v_new], axis=0)
            outs.append(_mm(q_dec[rows], st) + _mm(intra[rows], v_pad))
            state_ref[h] = st * dl[ch * c:ch * c + 1] + _mm(kt_t, v_pad)
        o = jnp.concatenate(outs, axis=0)
        yb = _rms(o, -1) * gn_ref[...] * _silu(z_ref[:, cols])
        o_ref[:, cols] = yb.astype(o_ref.dtype)


def _gdn(dn, ba, conv_w, alog_row, dtb_row, gn_row, batch, seq):
    t = batch * seq
    p = DN_PAIR
    nt = seq // p
    width = DN_HEADS * DN_DIM
    tok = lambda col: pl.BlockSpec((p, width), lambda b, i: (b * nt + i, col))
    cw = lambda col: pl.BlockSpec((DN_CONV, width), lambda b, i: (0, col))
    row = pl.BlockSpec((1, LANES), lambda b, i: (0, 0))
    return pl.pallas_call(
        _gdn_body,
        grid=(batch, nt),
        in_specs=[tok(0), tok(1), tok(2), tok(3),
                  pl.BlockSpec((p, LANES), lambda b, i: (b * nt + i, 0)),
                  cw(0), cw(1), cw(2), row, row, row],
        out_specs=pl.BlockSpec((p, width), lambda b, i: (b * nt + i, 0)),
        out_shape=jax.ShapeDtypeStruct((t, width), BF16),
        scratch_shapes=[pltpu.VMEM((DN_HEADS, DN_DIM, DN_DIM), F32),
                        pltpu.VMEM((3, SUBLANES, width), F32)],
        compiler_params=_params("parallel", "arbitrary"),
        name="gdn",
    )(dn, dn, dn, dn, ba, conv_w, conv_w, conv_w, alog_row, dtb_row, gn_row)


def _merge_body(x_ref, cq_ref, g0_ref, g1_ref, g2_ref, ya_ref, yb_ref, mkt_ref, mv_ref,
                wb_ref, wo_ref, ln_ref, o_ref):
    cq = cq_ref[...]
    heads = []
    for h in range(CA_HEADS):
        qh = cq[:, h * CA_HEAD_DIM:(h + 1) * CA_HEAD_DIM]
        s = jnp.dot(qh, mkt_ref[h], preferred_element_type=F32) * (CA_HEAD_DIM ** -0.5)
        e = jnp.exp(s - jnp.max(s, axis=-1, keepdims=True))
        pr = e / jnp.sum(e, axis=-1, keepdims=True)
        heads.append(jnp.dot(pr.astype(BF16), mv_ref[h], preferred_element_type=F32))
    yc = jnp.concatenate(heads, axis=1)

    mixed = _sigmoid(g0_ref[...].astype(F32)) * jnp.dot(ya_ref[...], wb_ref[0], preferred_element_type=F32)
    mixed = mixed + _sigmoid(g1_ref[...].astype(F32)) * jnp.dot(yb_ref[...], wb_ref[1], preferred_element_type=F32)
    mixed = mixed + _sigmoid(g2_ref[...].astype(F32)) * _mm(yc, wb_ref[2])
    out = _mm(mixed, wo_ref[...])
    o_ref[...] = x_ref[...] + _rms(out, -1) * ln_ref[...]


def _merge(x2d, cg, ya, yb, mkt, mv, wb, wo, ln_row, batch, seq, tm):
    t, d = x2d.shape
    nt = seq // tm
    tok = lambda col: pl.BlockSpec((tm, d), lambda i: (i, col))
    mem = pl.BlockSpec((None, CA_HEADS, CA_HEAD_DIM, CA_HEAD_DIM), lambda i: (i // nt, 0, 0, 0))
    return pl.pallas_call(
        _merge_body,
        grid=(t // tm,),
        in_specs=[tok(0), tok(0), tok(1), tok(2), tok(3), tok(0), tok(0), mem, mem,
                  pl.BlockSpec((3, d, d), lambda i: (0, 0, 0)),
                  pl.BlockSpec((d, d), lambda i: (0, 0)),
                  pl.BlockSpec((1, d), lambda i: (0, 0))],
        out_specs=tok(0),
        out_shape=jax.ShapeDtypeStruct((t, d), F32),
        compiler_params=_params("parallel"),
        name="merge",
    )(x2d, cg, cg, cg, cg, ya, yb, mkt, mv, wb, wo, ln_row)


def _ffn_body(h_ref, pre_ref, post_ref, w1_ref, w2_ref, o_ref, *, chunk):
    hh = h_ref[...]
    n = (_rms(hh, -1) * pre_ref[...]).astype(BF16)
    d_ff = w1_ref.shape[1]
    f = jnp.zeros(hh.shape, F32)
    for c0 in range(0, d_ff, chunk):
        a = jnp.dot(n, w1_ref[:, c0:c0 + chunk], preferred_element_type=F32)
        a = jnp.square(jnp.maximum(a, 0.0))
        f = f + jnp.dot(a.astype(BF16), w2_ref[c0:c0 + chunk, :], preferred_element_type=F32)
    o_ref[...] = hh + _rms(f, -1) * post_ref[...]


def _ffn(h2d, pre_row, post_row, w1, w2, tm):
    t, d = h2d.shape
    d_ff = w1.shape[1]
    return pl.pallas_call(
        functools.partial(_ffn_body, chunk=1024),
        grid=(t // tm,),
        in_specs=[pl.BlockSpec((tm, d), lambda i: (i, 0)),
                  pl.BlockSpec((1, d), lambda i: (0, 0)),
                  pl.BlockSpec((1, d), lambda i: (0, 0)),
                  pl.BlockSpec((d, d_ff), lambda i: (0, 0)),
                  pl.BlockSpec((d_ff, d), lambda i: (0, 0))],
        out_specs=pl.BlockSpec((tm, d), lambda i: (i, 0)),
        out_shape=jax.ShapeDtypeStruct((t, d), F32),
        compiler_params=_params("parallel"),
        name="ffn",
    )(h2d, pre_row, post_row, w1, w2)


def _pad_lanes(vec, offset):
    return jnp.zeros((1, LANES), F32).at[0, offset:offset + vec.shape[0]].set(vec.astype(F32))


def _layer(h2d, mem2d, rel_bias, w_in, conv_w, dn_a_log, dn_dt_bias, dn_norm_g, da_lambda,
           da_subln_g, mem_norm_g, w_mem_kv, w_branch, w_out, ln_mix_pre, ln_mix_post,
           ln_ff_pre, ln_ff_post, w_ff1, w_ff2, batch, seq, mem_len, layer):
    d = h2d.shape[1]
    t = batch * seq
    lam_init = 0.8 - 0.6 * math.exp(-0.3 * layer)
    tm = min(1024, seq)
    row = lambda v: v.reshape(1, -1).astype(F32)

    da_w = 3 * d
    dn_w = 4 * d
    small0 = da_w + dn_w
    rest0 = small0 + 2 * DN_HEADS
    scale_q = jnp.concatenate([jnp.full((d,), DA_HEAD_DIM ** -0.5, F32), jnp.ones((2 * d,), F32)])
    w_da = (w_in[:, :da_w] * scale_q).astype(BF16)
    w_dn = w_in[:, da_w:small0].astype(BF16)
    w_small = jnp.zeros((d, LANES), F32).at[:, :2 * DN_HEADS].set(w_in[:, small0:rest0]).astype(BF16)
    w_rest = w_in[:, rest0:].astype(BF16)
    g_pre = row(ln_mix_pre)

    da = _norm_matmul(h2d, g_pre, w_da, 0, da_w, BF16, tm, 1024)
    dn = _norm_matmul(h2d, g_pre, w_dn, 0, dn_w, F32, tm, 1024)
    ba = _norm_matmul(h2d, g_pre, w_small, 0, LANES, F32, tm, LANES)
    cg = _norm_matmul(h2d, g_pre, w_rest, 0, 4 * d, BF16, tm, 1024)
    mkv = _norm_matmul(mem2d, row(mem_norm_g), w_mem_kv.astype(BF16), 0, 2 * d, BF16,
                       min(512, batch * mem_len), 1024)

    hd = 2 * DA_HEAD_DIM
    qt = da[:, :d].reshape(batch, seq, DA_HEADS, hd).transpose(0, 2, 3, 1)
    vt = da[:, 2 * d:].reshape(batch, seq, DA_HEADS, hd).transpose(0, 2, 3, 1)
    bias = _bias_tiles(rel_bias.astype(F32), ATTN_TILE)
    y_a = _diff_attention(da_lambda.astype(F32), qt, da, vt, bias, row(da_subln_g), batch, seq, lam_init)

    y_b = _gdn(dn, ba, conv_w.astype(F32), _pad_lanes(dn_a_log, DN_HEADS), _pad_lanes(dn_dt_bias, DN_HEADS),
               row(dn_norm_g), batch, seq)

    mkt = mkv[:, :d].reshape(batch, mem_len, CA_HEADS, CA_HEAD_DIM).transpose(0, 2, 3, 1)
    mv = mkv[:, d:].reshape(batch, mem_len, CA_HEADS, CA_HEAD_DIM).transpose(0, 2, 1, 3)
    h1 = _merge(h2d, cg, y_a, y_b, mkt, mv, w_branch.astype(BF16), w_out.astype(BF16),
                row(ln_mix_post), batch, seq, min(512, seq))

    return _ffn(h1, row(ln_ff_pre), row(ln_ff_post), w_ff1.astype(BF16), w_ff2.astype(BF16), min(512, seq))


def kernel(x, mem, rel_bias, w_in, conv_w, dn_a_log, dn_dt_bias, dn_norm_g, da_lambda, da_subln_g,
           mem_norm_g, w_mem_kv, w_branch, w_out, ln_mix_pre, ln_mix_post, ln_ff_pre, ln_ff_post,
           w_ff1, w_ff2):
    batch, seq, d = x.shape
    mem_len = mem.shape[1]
    assert seq % ATTN_TILE == 0 and seq % DN_PAIR == 0
    h2d = x.reshape(batch * seq, d)
    mem2d = mem.reshape(batch * mem_len, d)
    for layer in range(w_in.shape[0]):
        h2d = _layer(h2d, mem2d, rel_bias, w_in[layer], conv_w[layer], dn_a_log[layer], dn_dt_bias[layer],
                     dn_norm_g[layer], da_lambda[layer], da_subln_g[layer], mem_norm_g[layer],
                     w_mem_kv[layer], w_branch[layer], w_out[layer], ln_mix_pre[layer], ln_mix_post[layer],
                     ln_ff_pre[layer], ln_ff_post[layer], w_ff1[layer], w_ff2[layer],
                     batch, seq, mem_len, layer)
    return h2d.reshape(batch, seq, d)
```

```python
import functools
import math

import numpy as np
import jax
import jax.numpy as jnp
from jax import lax
from jax.experimental import pallas as pl
from jax.experimental.pallas import tpu as pltpu

F32 = jnp.float32
BF16 = jnp.bfloat16

EPS = 1e-6
MASK_VALUE = -1e30
LOG2E = math.log2(math.e)
LANES = 128
SUBLANES = 8
VMEM_LIMIT_BYTES = 56 * 1024 * 1024

DA_HEADS = 8
DA_HEAD_DIM = 64
DN_HEADS = 8
DN_DIM = 128
DN_CONV = 4
DN_CHUNK = 64
DN_PAIR = 2 * DN_CHUNK
CA_HEADS = 4
CA_HEAD_DIM = 256
REL_BUCKETS = 32
REL_MAX_EXACT = 16
REL_MAX_DIST = 128

ATTN_TILE = 256


def _params(*sem):
    return pltpu.CompilerParams(dimension_semantics=sem, vmem_limit_bytes=VMEM_LIMIT_BYTES)


def _mm(a, b):
    return jnp.dot(a.astype(BF16), b.astype(BF16), preferred_element_type=F32)


def _mm_nt(a, b):
    return lax.dot_general(a.astype(BF16), b.astype(BF16), (((1,), (1,)), ((), ())),
                           preferred_element_type=F32)


def _mm_exact_rhs(m01, a):
    hi = a.astype(BF16)
    r1 = a - hi.astype(F32)
    mid = r1.astype(BF16)
    lo = (r1 - mid.astype(F32)).astype(BF16)
    dot = functools.partial(jnp.dot, preferred_element_type=F32)
    return dot(m01, hi) + dot(m01, mid) + dot(m01, lo)


def _rms(x, axis):
    return x * lax.rsqrt(jnp.mean(x * x, axis=axis, keepdims=True) + EPS)


def _sigmoid(x):
    return 1.0 / (1.0 + jnp.exp(-x))


def _silu(x):
    return x * _sigmoid(x)


def _norm_matmul_body(x_ref, g_ref, w_ref, o_ref, n_ref):
    @pl.when(pl.program_id(1) == 0)
    def _():
        n_ref[...] = (_rms(x_ref[...], -1) * g_ref[...]).astype(BF16)

    o_ref[...] = jnp.dot(n_ref[...], w_ref[...], preferred_element_type=F32).astype(o_ref.dtype)


def _norm_matmul(x2d, g_row, w, col_block_offset, n_cols, out_dtype, tm, tn):
    t, d = x2d.shape
    return pl.pallas_call(
        _norm_matmul_body,
        grid=(t // tm, n_cols // tn),
        in_specs=[pl.BlockSpec((tm, d), lambda i, j: (i, 0)),
                  pl.BlockSpec((1, d), lambda i, j: (0, 0)),
                  pl.BlockSpec((d, tn), lambda i, j: (0, j + col_block_offset))],
        out_specs=pl.BlockSpec((tm, tn), lambda i, j: (i, j)),
        out_shape=jax.ShapeDtypeStruct((t, n_cols), out_dtype),
        scratch_shapes=[pltpu.VMEM((tm, d), BF16)],
        compiler_params=_params("parallel", "arbitrary"),
        name="norm_matmul",
    )(x2d, g_row, w)


def _bucket_tiles(tile):
    k = np.arange(tile)[:, None]
    q = np.arange(tile)[None, :]

    def bucket(dist):
        nf = np.maximum(dist, 1).astype(np.float64)
        large = REL_MAX_EXACT + np.trunc(
            np.log(nf / REL_MAX_EXACT) / math.log(REL_MAX_DIST / REL_MAX_EXACT)
            * (REL_BUCKETS - REL_MAX_EXACT)).astype(np.int64)
        large = np.minimum(large, REL_BUCKETS - 1)
        return np.where(dist < REL_MAX_EXACT, dist, large)

    diag = np.where(q >= k, bucket(np.maximum(q - k, 0)), -1)
    prev = bucket(q - k + tile)
    return np.stack([diag, prev]).astype(np.int32)


def _bias_tile_body(rb_ref, bucket_ref, o_ref):
    h = pl.program_id(0)
    far = rb_ref[REL_BUCKETS - 1, h]
    bk = bucket_ref[...]
    out = jnp.full(bk.shape, MASK_VALUE, F32)
    for b in range(REL_BUCKETS):
        out = jnp.where(bk == b, (rb_ref[b, h] - far) * LOG2E, out)
    o_ref[...] = out


def _bias_tiles(rel_bias, tile):
    heads = rel_bias.shape[1]
    buckets = jnp.asarray(_bucket_tiles(tile))
    return pl.pallas_call(
        _bias_tile_body,
        grid=(heads,),
        in_specs=[pl.BlockSpec(memory_space=pltpu.SMEM),
                  pl.BlockSpec((2, tile, tile), lambda h: (0, 0, 0))],
        out_specs=pl.BlockSpec((None, 2, tile, tile), lambda h: (h, 0, 0, 0)),
        out_shape=jax.ShapeDtypeStruct((heads, 2, tile, tile), F32),
        compiler_params=_params("arbitrary"),
        name="bias_tiles",
    )(rel_bias, buckets)


def _attn_body(lam_ref, qt_ref, k_ref, vt_ref, bias_ref, g_ref, o_ref,
               qz_ref, m_ref, acc_ref, s0_ref, s1_ref, p0_ref, p1_ref, a0_ref, a1_ref, *, tile, lam_init):
    qi = pl.program_id(2)
    hd = 2 * DA_HEAD_DIM

    qt = qt_ref[...]
    row = lax.broadcasted_iota(jnp.int32, qt.shape, 0)
    zero = jnp.zeros_like(qt)
    qz_ref[:, :tile] = jnp.where(row < DA_HEAD_DIM, qt, zero)
    qz_ref[:, tile:] = jnp.where(row >= DA_HEAD_DIM, qt, zero)
    m_ref[...] = jnp.full(m_ref.shape, MASK_VALUE, F32)
    acc_ref[...] = jnp.zeros(acc_ref.shape, F32)

    def koff(t):
        return pl.multiple_of(t * tile, tile)

    def scores(t, s_ref):
        s_ref[...] = jnp.dot(k_ref[pl.ds(koff(t), tile), :], qz_ref[...], preferred_element_type=F32)

    def softmax(s_ref, p_ref, a_ref, bias=None):
        s = s_ref[...]
        if bias is not None:
            s = s + jnp.concatenate([bias, bias], axis=1)
        m_old = m_ref[...]
        m_new = jnp.maximum(m_old, jnp.max(s, axis=0, keepdims=True))
        a_ref[...] = jnp.exp2(m_old - m_new)
        p_ref[...] = jnp.exp2(s - m_new).astype(BF16)
        m_ref[...] = m_new

    def accumulate(t, p_ref, a_ref):
        pv = jnp.dot(vt_ref[:, pl.ds(koff(t), tile)], p_ref[...], preferred_element_type=F32)
        acc_ref[...] = a_ref[...] * acc_ref[...] + pv

    def whole_tile(t, bias=None):
        scores(t, s0_ref)
        softmax(s0_ref, p0_ref, a0_ref, bias)
        accumulate(t, p0_ref, a0_ref)

    n_far = jnp.maximum(qi - 1, 0)
    first = n_far % 2
    n_pairs = n_far // 2

    @pl.when(first == 1)
    def _():
        whole_tile(0)

    @pl.when(n_pairs > 0)
    def _():
        p1_ref[...] = jnp.zeros(p1_ref.shape, BF16)
        a1_ref[...] = jnp.ones(a1_ref.shape, F32)
        scores(first, s0_ref)

        def pair(j, carry):
            ta = first + 2 * j
            scores(ta + 1, s1_ref)
            softmax(s0_ref, p0_ref, a0_ref)
            accumulate(jnp.maximum(ta - 1, 0), p1_ref, a1_ref)
            scores(ta + 2, s0_ref)
            softmax(s1_ref, p1_ref, a1_ref)
            accumulate(ta, p0_ref, a0_ref)
            return carry

        lax.fori_loop(0, n_pairs, pair, 0)
        accumulate(first + 2 * n_pairs - 1, p1_ref, a1_ref)

    @pl.when(qi >= 1)
    def _():
        whole_tile(qi - 1, bias_ref[1])

    whole_tile(qi, bias_ref[0])

    lv = lam_ref[...]
    lam = (jnp.exp(jnp.sum(lv[0:1] * lv[1:2], axis=-1, keepdims=True))
           - jnp.exp(jnp.sum(lv[2:3] * lv[3:4], axis=-1, keepdims=True)) + lam_init)
    acc = acc_ref[...]
    acc = acc[:hd] * (1.0 / acc[hd:hd + 1])
    ot = acc[:, :tile] - lam * acc[:, tile:]
    o = _rms(ot, 0).T * g_ref[...] * (1.0 - lam_init)
    o_ref[...] = o.astype(o_ref.dtype)


def _diff_attention(lam_vecs, qt, da, vt_ones, bias, g_row, batch, seq, lam_init):
    tile = ATTN_TILE
    nq = seq // tile
    hd = 2 * DA_HEAD_DIM
    hv = vt_ones.shape[2]
    k_col0 = DA_HEADS
    return pl.pallas_call(
        functools.partial(_attn_body, tile=tile, lam_init=lam_init),
        grid=(batch, DA_HEADS, nq),
        in_specs=[pl.BlockSpec((4, DA_HEAD_DIM), lambda b, h, i: (0, 0)),
                  pl.BlockSpec((None, None, hd, tile), lambda b, h, i: (b, h, 0, i)),
                  pl.BlockSpec((seq, hd), lambda b, h, i: (b, k_col0 + h)),
                  pl.BlockSpec((None, None, hv, seq), lambda b, h, i: (b, h, 0, 0)),
                  pl.BlockSpec((None, 2, tile, tile), lambda b, h, i: (h, 0, 0, 0)),
                  pl.BlockSpec((1, hd), lambda b, h, i: (0, 0))],
        out_specs=pl.BlockSpec((tile, hd), lambda b, h, i: (b * nq + i, h)),
        out_shape=jax.ShapeDtypeStruct((batch * seq, DA_HEADS * hd), BF16),
        scratch_shapes=[pltpu.VMEM((hd, 2 * tile), BF16),
                        pltpu.VMEM((1, 2 * tile), F32),
                        pltpu.VMEM((hv, 2 * tile), F32),
                        pltpu.VMEM((tile, 2 * tile), F32),
                        pltpu.VMEM((tile, 2 * tile), F32),
                        pltpu.VMEM((tile, 2 * tile), BF16),
                        pltpu.VMEM((tile, 2 * tile), BF16),
                        pltpu.VMEM((1, 2 * tile), F32),
                        pltpu.VMEM((1, 2 * tile), F32)],
        compiler_params=_params("parallel", "parallel", "arbitrary"),
        name="diff_attn",
    )(lam_vecs, qt, da, vt_ones, bias, g_row)


def _shift_rows(x, prev, s):
    if s == 0:
        return x
    rolled = pltpu.roll(x, s, 0)
    prolled = pltpu.roll(prev, s, 0)
    r8 = lax.broadcasted_iota(jnp.int32, prev.shape, 0)
    top = jnp.where(r8 < s, prolled, rolled[:SUBLANES])
    return jnp.concatenate([top, rolled[SUBLANES:]], axis=0)


def _gdn_body(q_ref, k_ref, v_ref, z_ref, ba_ref, cwq_ref, cwk_ref, cwv_ref, alog_ref, dtb_ref,
              gn_ref, o_ref, state_ref, tail_ref):
    p = DN_PAIR
    c = DN_CHUNK

    @pl.when(pl.program_id(1) == 0)
    def _():
        state_ref[...] = jnp.zeros(state_ref.shape, F32)
        tail_ref[...] = jnp.zeros(tail_ref.shape, F32)

    ri = lax.broadcasted_iota(jnp.int32, (p, p), 0)
    ci = lax.broadcasted_iota(jnp.int32, (p, p), 1)
    same = (ri // c) == (ci // c)
    causal = same & (ri >= ci)
    strict = same & (ri > ci)
    cum01 = jnp.where(causal, 1.0, 0.0).astype(BF16)

    ba = ba_ref[...]
    beta_all = _sigmoid(ba)
    xs = ba + dtb_ref[...]
    softplus = jnp.maximum(xs, 0.0) + jnp.log(1.0 + jnp.exp(-jnp.abs(xs)))
    g_all = -jnp.exp(alog_ref[...]) * softplus

    def conv(x_ref, w_ref, which, h):
        cols = slice(h * DN_DIM, (h + 1) * DN_DIM)
        x = x_ref[:, cols]
        prev = tail_ref[which, :, cols]
        w = w_ref[:, cols]
        y = w[DN_CONV - 1:DN_CONV] * x
        for s in range(1, DN_CONV):
            y = y + w[DN_CONV - 1 - s:DN_CONV - s] * _shift_rows(x, prev, s)
        tail_ref[which, :, cols] = x[p - SUBLANES:]
        return _silu(y)

    gc_all = _mm_exact_rhs(cum01, g_all)
    glast_all = jnp.concatenate(
        [jnp.broadcast_to(gc_all[c - 1:c], (c, LANES)), jnp.broadcast_to(gc_all[p - 1:p], (c, LANES))], axis=0)
    expg_all = jnp.exp(gc_all)
    tail_all = jnp.exp(glast_all - gc_all)
    dl_all = jnp.exp(glast_all)

    def lane(x, j):
        return jnp.broadcast_to(x[:, j:j + 1], (x.shape[0], DN_DIM))

    heads = range(DN_HEADS)
    q = [conv(q_ref, cwq_ref, 0, h) for h in heads]
    k = [conv(k_ref, cwk_ref, 1, h) for h in heads]
    v = [conv(v_ref, cwv_ref, 2, h) for h in heads]
    q = [x * lax.rsqrt(jnp.sum(x * x, axis=-1, keepdims=True) + EPS) * (DN_DIM ** -0.5) for x in q]
    k = [x * lax.rsqrt(jnp.sum(x * x, axis=-1, keepdims=True) + EPS) for x in k]

    gc = [lane(gc_all, DN_HEADS + h) for h in heads]
    decay = [jnp.exp(jnp.where(causal, x - x.T, MASK_VALUE)) for x in gc]
    exp_g = [lane(expg_all, DN_HEADS + h) for h in heads]
    kb = [k[h] * lane(beta_all, h) for h in heads]
    vb = [v[h] * lane(beta_all, h) for h in heads]
    low = [jnp.where(strict, _mm_nt(kb[h], k[h]) * decay[h], 0.0) for h in heads]
    intra = [jnp.where(causal, _mm_nt(q[h], k[h]) * decay[h], 0.0) for h in heads]
    y = [-x for x in low]
    pw = low
    for _ in range(5):
        pw = [_mm(x, x) for x in pw]
        y = [y[h] + pw[h] + _mm(y[h], pw[h]) for h in heads]
    rhs = [jnp.concatenate([vb[h], kb[h] * exp_g[h]], axis=1) for h in heads]
    sol = [rhs[h] + _mm(y[h], rhs[h]) for h in heads]
    q_dec = [q[h] * exp_g[h] for h in heads]
    kt_t = [(k[h] * lane(tail_all, DN_HEADS + h)).T for h in heads]

    outs = [[], []]
    zeros = jnp.zeros((c, DN_DIM), F32)
    for ch in range(2):
        rows = slice(ch * c, (ch + 1) * c)
        st = [state_ref[h] for h in heads]
        v_new = [sol[h][rows, :DN_DIM] - _mm(sol[h][rows, DN_DIM:], st[h]) for h in heads]
        v_pad = [jnp.concatenate([x, zeros] if ch == 0 else [zeros, x], axis=0) for x in v_new]
        outs[ch] = [_mm(q_dec[h][rows], st[h]) + _mm(intra[h][rows], v_pad[h]) for h in heads]
        for h in heads:
            dl = jnp.broadcast_to(dl_all[ch * c:ch * c + 1, DN_HEADS + h:DN_HEADS + h + 1], (1, DN_DIM))
            state_ref[h] = st[h] * dl + _mm(kt_t[h], v_pad[h])
    for h in heads:
        cols = slice(h * DN_DIM, (h + 1) * DN_DIM)
        o = jnp.concatenate([outs[0][h], outs[1][h]], axis=0)
        yb = _rms(o, -1) * gn_ref[...] * _silu(z_ref[:, cols])
        o_ref[:, cols] = yb.astype(o_ref.dtype)


def _gdn(dn, ba, conv_w, alog_row, dtb_row, gn_row, batch, seq):
    t = batch * seq
    p = DN_PAIR
    nt = seq // p
    width = DN_HEADS * DN_DIM
    tok = lambda col: pl.BlockSpec((p, width), lambda b, i: (b * nt + i, col))
    cw = lambda col: pl.BlockSpec((DN_CONV, width), lambda b, i: (0, col))
    row = pl.BlockSpec((1, LANES), lambda b, i: (0, 0))
    return pl.pallas_call(
        _gdn_body,
        grid=(batch, nt),
        in_specs=[tok(0), tok(1), tok(2), tok(3),
                  pl.BlockSpec((p, LANES), lambda b, i: (b * nt + i, 0)),
                  cw(0), cw(1), cw(2), row, row, row],
        out_specs=pl.BlockSpec((p, width), lambda b, i: (b * nt + i, 0)),
        out_shape=jax.ShapeDtypeStruct((t, width), BF16),
        scratch_shapes=[pltpu.VMEM((DN_HEADS, DN_DIM, DN_DIM), F32),
                        pltpu.VMEM((3, SUBLANES, width), F32)],
        compiler_params=_params("parallel", "arbitrary"),
        name="gdn",
    )(dn, dn, dn, dn, ba, conv_w, conv_w, conv_w, alog_row, dtb_row, gn_row)


def _merge_body(x_ref, cq_ref, g0_ref, g1_ref, g2_ref, ya_ref, yb_ref, mkt_ref, mv_ref,
                wb_ref, wo_ref, ln_ref, o_ref):
    cq = cq_ref[...]
    heads = []
    for h in range(CA_HEADS):
        qh = cq[:, h * CA_HEAD_DIM:(h + 1) * CA_HEAD_DIM]
        s = jnp.dot(qh, mkt_ref[h], preferred_element_type=F32) * (CA_HEAD_DIM ** -0.5)
        e = jnp.exp(s - jnp.max(s, axis=-1, keepdims=True))
        pr = e / jnp.sum(e, axis=-1, keepdims=True)
        heads.append(jnp.dot(pr.astype(BF16), mv_ref[h], preferred_element_type=F32))
    yc = jnp.concatenate(heads, axis=1)

    mixed = _sigmoid(g0_ref[...].astype(F32)) * jnp.dot(ya_ref[...], wb_ref[0], preferred_element_type=F32)
    mixed = mixed + _sigmoid(g1_ref[...].astype(F32)) * jnp.dot(yb_ref[...], wb_ref[1], preferred_element_type=F32)
    mixed = mixed + _sigmoid(g2_ref[...].astype(F32)) * _mm(yc, wb_ref[2])
    out = _mm(mixed, wo_ref[...])
    o_ref[...] = x_ref[...] + _rms(out, -1) * ln_ref[...]


def _merge(x2d, cg, ya, yb, mkt, mv, wb, wo, ln_row, batch, seq, tm):
    t, d = x2d.shape
    nt = seq // tm
    tok = lambda col: pl.BlockSpec((tm, d), lambda i: (i, col))
    mem = pl.BlockSpec((None, CA_HEADS, CA_HEAD_DIM, CA_HEAD_DIM), lambda i: (i // nt, 0, 0, 0))
    return pl.pallas_call(
        _merge_body,
        grid=(t // tm,),
        in_specs=[tok(0), tok(0), tok(1), tok(2), tok(3), tok(0), tok(0), mem, mem,
                  pl.BlockSpec((3, d, d), lambda i: (0, 0, 0)),
                  pl.BlockSpec((d, d), lambda i: (0, 0)),
                  pl.BlockSpec((1, d), lambda i: (0, 0))],
        out_specs=tok(0),
        out_shape=jax.ShapeDtypeStruct((t, d), F32),
        compiler_params=_params("parallel"),
        name="merge",
    )(x2d, cg, cg, cg, cg, ya, yb, mkt, mv, wb, wo, ln_row)


def _ffn_body(h_ref, pre_ref, post_ref, w1_ref, w2_ref, o_ref, *, chunk):
    hh = h_ref[...]
    n = (_rms(hh, -1) * pre_ref[...]).astype(BF16)
    d_ff = w1_ref.shape[1]
    f = jnp.zeros(hh.shape, F32)
    for c0 in range(0, d_ff, chunk):
        a = jnp.dot(n, w1_ref[:, c0:c0 + chunk], preferred_element_type=F32)
        a = jnp.square(jnp.maximum(a, 0.0))
        f = f + jnp.dot(a.astype(BF16), w2_ref[c0:c0 + chunk, :], preferred_element_type=F32)
    o_ref[...] = hh + _rms(f, -1) * post_ref[...]


def _ffn(h2d, pre_row, post_row, w1, w2, tm):
    t, d = h2d.shape
    d_ff = w1.shape[1]
    return pl.pallas_call(
        functools.partial(_ffn_body, chunk=1024),
        grid=(t // tm,),
        in_specs=[pl.BlockSpec((tm, d), lambda i: (i, 0)),
                  pl.BlockSpec((1, d), lambda i: (0, 0)),
                  pl.BlockSpec((1, d), lambda i: (0, 0)),
                  pl.BlockSpec((d, d_ff), lambda i: (0, 0)),
                  pl.BlockSpec((d_ff, d), lambda i: (0, 0))],
        out_specs=pl.BlockSpec((tm, d), lambda i: (i, 0)),
        out_shape=jax.ShapeDtypeStruct((t, d), F32),
        compiler_params=_params("parallel"),
        name="ffn",
    )(h2d, pre_row, post_row, w1, w2)


def _pad_lanes(vec, offset):
    return jnp.zeros((1, LANES), F32).at[0, offset:offset + vec.shape[0]].set(vec.astype(F32))


def _layer(h2d, mem2d, rel_bias, w_in, conv_w, dn_a_log, dn_dt_bias, dn_norm_g, da_lambda,
           da_subln_g, mem_norm_g, w_mem_kv, w_branch, w_out, ln_mix_pre, ln_mix_post,
           ln_ff_pre, ln_ff_post, w_ff1, w_ff2, batch, seq, mem_len, layer):
    d = h2d.shape[1]
    t = batch * seq
    lam_init = 0.8 - 0.6 * math.exp(-0.3 * layer)
    tm = min(1024, seq)
    row = lambda v: v.reshape(1, -1).astype(F32)

    da_w = 3 * d
    dn_w = 4 * d
    small0 = da_w + dn_w
    rest0 = small0 + 2 * DN_HEADS
    scale_q = jnp.concatenate([jnp.full((d,), DA_HEAD_DIM ** -0.5 * LOG2E, F32), jnp.ones((2 * d,), F32)])
    w_da = (w_in[:, :da_w] * scale_q).astype(BF16)
    w_dn = w_in[:, da_w:small0].astype(BF16)
    w_small = jnp.zeros((d, LANES), F32).at[:, :2 * DN_HEADS].set(w_in[:, small0:rest0]).astype(BF16)
    w_rest = w_in[:, rest0:].astype(BF16)
    g_pre = row(ln_mix_pre)

    da = _norm_matmul(h2d, g_pre, w_da, 0, da_w, BF16, tm, 1024)
    dn = _norm_matmul(h2d, g_pre, w_dn, 0, dn_w, F32, tm, 1024)
    ba = _norm_matmul(h2d, g_pre, w_small, 0, LANES, F32, tm, LANES)
    cg = _norm_matmul(h2d, g_pre, w_rest, 0, 4 * d, BF16, tm, 1024)
    mkv = _norm_matmul(mem2d, row(mem_norm_g), w_mem_kv.astype(BF16), 0, 2 * d, BF16,
                       min(512, batch * mem_len), 1024)

    hd = 2 * DA_HEAD_DIM
    qt = da[:, :d].reshape(batch, seq, DA_HEADS, hd).transpose(0, 2, 3, 1)
    vt = da[:, 2 * d:].reshape(batch, seq, DA_HEADS, hd).transpose(0, 2, 3, 1)
    vt_ones = jnp.concatenate([vt, jnp.ones((batch, DA_HEADS, 2 * SUBLANES, seq), BF16)], axis=2)
    bias = _bias_tiles(rel_bias.astype(F32), ATTN_TILE)
    y_a = _diff_attention(da_lambda.astype(F32), qt, da, vt_ones, bias, row(da_subln_g), batch, seq, lam_init)

    y_b = _gdn(dn, ba, conv_w.astype(F32), _pad_lanes(dn_a_log, DN_HEADS), _pad_lanes(dn_dt_bias, DN_HEADS),
               row(dn_norm_g), batch, seq)

    mkt = mkv[:, :d].reshape(batch, mem_len, CA_HEADS, CA_HEAD_DIM).transpose(0, 2, 3, 1)
    mv = mkv[:, d:].reshape(batch, mem_len, CA_HEADS, CA_HEAD_DIM).transpose(0, 2, 1, 3)
    h1 = _merge(h2d, cg, y_a, y_b, mkt, mv, w_branch.astype(BF16), w_out.astype(BF16),
                row(ln_mix_post), batch, seq, min(512, seq))

    return _ffn(h1, row(ln_ff_pre), row(ln_ff_post), w_ff1.astype(BF16), w_ff2.astype(BF16), min(512, seq))


def kernel(x, mem, rel_bias, w_in, conv_w, dn_a_log, dn_dt_bias, dn_norm_g, da_lambda, da_subln_g,
           mem_norm_g, w_mem_kv, w_branch, w_out, ln_mix_pre, ln_mix_post, ln_ff_pre, ln_ff_post,
           w_ff1, w_ff2):
    batch, seq, d = x.shape
    mem_len = mem.shape[1]
    assert seq % ATTN_TILE == 0 and seq % DN_PAIR == 0
    h2d = x.reshape(batch * seq, d)
    mem2d = mem.reshape(batch * mem_len, d)
    for layer in range(w_in.shape[0]):
        h2d = _layer(h2d, mem2d, rel_bias, w_in[layer], conv_w[layer], dn_a_log[layer], dn_dt_bias[layer],
                     dn_norm_g[layer], da_lambda[layer], da_subln_g[layer], mem_norm_g[layer],
                     w_mem_kv[layer], w_branch[layer], w_out[layer], ln_mix_pre[layer], ln_mix_post[layer],
                     ln_ff_pre[layer], ln_ff_post[layer], w_ff1[layer], w_ff2[layer],
                     batch, seq, mem_len, layer)
    return h2d.reshape(batch, seq, d)
```

```python
import functools
import math

import numpy as np
import jax
import jax.numpy as jnp
from jax import lax
from jax.experimental import pallas as pl
from jax.experimental.pallas import tpu as pltpu

F32 = jnp.float32
BF16 = jnp.bfloat16

EPS = 1e-6
MASK_VALUE = -1e30
LOG2E = math.log2(math.e)
LANES = 128
SUBLANES = 8
VMEM_LIMIT_BYTES = 56 * 1024 * 1024

DA_HEADS = 8
DA_HEAD_DIM = 64
DN_HEADS = 8
DN_DIM = 128
DN_CONV = 4
DN_CHUNK = 64
DN_PAIR = 2 * DN_CHUNK
CA_HEADS = 4
CA_HEAD_DIM = 256
REL_BUCKETS = 32
REL_MAX_EXACT = 16
REL_MAX_DIST = 128

ATTN_TILE = 256
ATTN_TRIP_TILES = 8
SCORE_LIMIT = 60.0


def _params(*sem):
    return pltpu.CompilerParams(dimension_semantics=sem, vmem_limit_bytes=VMEM_LIMIT_BYTES)


def _mm(a, b):
    return jnp.dot(a.astype(BF16), b.astype(BF16), preferred_element_type=F32)


def _mm_nt(a, b):
    return lax.dot_general(a.astype(BF16), b.astype(BF16), (((1,), (1,)), ((), ())),
                           preferred_element_type=F32)


def _mm_exact_rhs(m01, a):
    hi = a.astype(BF16)
    r1 = a - hi.astype(F32)
    mid = r1.astype(BF16)
    lo = (r1 - mid.astype(F32)).astype(BF16)
    dot = functools.partial(jnp.dot, preferred_element_type=F32)
    return dot(m01, hi) + dot(m01, mid) + dot(m01, lo)


def _rms(x, axis):
    return x * lax.rsqrt(jnp.mean(x * x, axis=axis, keepdims=True) + EPS)


def _sigmoid(x):
    return 1.0 / (1.0 + jnp.exp(-x))


def _silu(x):
    return x * _sigmoid(x)


def _norm_matmul_body(x_ref, g_ref, w_ref, o_ref, n_ref):
    @pl.when(pl.program_id(1) == 0)
    def _():
        n_ref[...] = (_rms(x_ref[...], -1) * g_ref[...]).astype(BF16)

    o_ref[...] = jnp.dot(n_ref[...], w_ref[...], preferred_element_type=F32).astype(o_ref.dtype)


def _norm_matmul(x2d, g_row, w, col_block_offset, n_cols, out_dtype, tm, tn):
    t, d = x2d.shape
    return pl.pallas_call(
        _norm_matmul_body,
        grid=(t // tm, n_cols // tn),
        in_specs=[pl.BlockSpec((tm, d), lambda i, j: (i, 0)),
                  pl.BlockSpec((1, d), lambda i, j: (0, 0)),
                  pl.BlockSpec((d, tn), lambda i, j: (0, j + col_block_offset))],
        out_specs=pl.BlockSpec((tm, tn), lambda i, j: (i, j)),
        out_shape=jax.ShapeDtypeStruct((t, n_cols), out_dtype),
        scratch_shapes=[pltpu.VMEM((tm, d), BF16)],
        compiler_params=_params("parallel", "arbitrary"),
        name="norm_matmul",
    )(x2d, g_row, w)


BIAS_PREV, BIAS_DIAG, BIAS_MASKED, BIAS_NONE = 0, 1, 2, 3


def _bucket_table(tile):
    k = np.arange(tile)[:, None]
    q = np.arange(tile)[None, :]

    def bucket(dist):
        nf = np.maximum(dist, 1).astype(np.float64)
        large = REL_MAX_EXACT + np.trunc(
            np.log(nf / REL_MAX_EXACT) / math.log(REL_MAX_DIST / REL_MAX_EXACT)
            * (REL_BUCKETS - REL_MAX_EXACT)).astype(np.int64)
        large = np.minimum(large, REL_BUCKETS - 1)
        return np.where(dist < REL_MAX_EXACT, dist, large)

    prev = bucket(q - k + tile)
    diag = np.where(q >= k, bucket(np.maximum(q - k, 0)), -1)
    masked = np.full((tile, tile), -1)
    far = np.full((tile, tile), REL_BUCKETS - 1)
    return np.concatenate([prev, diag, masked, far], axis=0).astype(np.int32)


def _bias_table_body(rb_ref, bucket_ref, o_ref):
    h = pl.program_id(0)
    far = rb_ref[REL_BUCKETS - 1, h]
    bk = bucket_ref[...]
    out = jnp.full(bk.shape, MASK_VALUE, F32)
    for b in range(REL_BUCKETS):
        out = jnp.where(bk == b, (rb_ref[b, h] - far) * LOG2E, out)
    o_ref[...] = out


def _bias_table(rel_bias, tile):
    heads = rel_bias.shape[1]
    buckets = jnp.asarray(_bucket_table(tile))
    rows = buckets.shape[0]
    return pl.pallas_call(
        _bias_table_body,
        grid=(heads,),
        in_specs=[pl.BlockSpec(memory_space=pltpu.SMEM),
                  pl.BlockSpec((rows, tile), lambda h: (0, 0))],
        out_specs=pl.BlockSpec((None, rows, tile), lambda h: (h, 0, 0)),
        out_shape=jax.ShapeDtypeStruct((heads, rows, tile), F32),
        compiler_params=_params("arbitrary"),
        name="bias_table",
    )(rel_bias, buckets)


def _attn_body(lam_ref, qt_ref, k_ref, vt_ref, bias_ref, g_ref, o_ref,
               qz_ref, m_ref, x_ref, acc_ref, p_ref, *, tile, lam_init):
    qi = pl.program_id(2)
    hd = 2 * DA_HEAD_DIM

    qt = qt_ref[...]
    row = lax.broadcasted_iota(jnp.int32, qt.shape, 0)
    zero = jnp.zeros_like(qt)
    qz_ref[:, :tile] = jnp.where(row < DA_HEAD_DIM, qt, zero)
    qz_ref[:, tile:] = jnp.where(row >= DA_HEAD_DIM, qt, zero)

    def tile_off(t):
        return pl.multiple_of(t * tile, tile)

    def bias_rows(block):
        b = bias_ref[pl.ds(pl.multiple_of(block * tile, tile), tile), :]
        return jnp.concatenate([b, b], axis=1)

    def scores(off, bias):
        s = jnp.dot(k_ref[pl.ds(off, tile), :], qz_ref[...], preferred_element_type=F32)
        return s if bias is None else s + bias

    def values(off, n_keys, p):
        return jnp.dot(vt_ref[:, pl.ds(off, n_keys)], p, preferred_element_type=F32)

    def exact_tile(off, bias, first):
        s = scores(off, bias)
        c = jnp.max(s, axis=0, keepdims=True)
        if first:
            m_new = c
        else:
            m_old = m_ref[...]
            m_new = jnp.maximum(m_old, c)
        pv = values(off, tile, jnp.exp2(s - m_new).astype(BF16))
        acc_ref[...] = pv if first else jnp.exp2(m_old - m_new) * acc_ref[...] + pv
        m_ref[...] = m_new

    def plain_group(tiles, contiguous, first):
        cpart = None
        pv = None
        for i, (off, bias) in enumerate(tiles):
            s = scores(off, bias)
            part = jnp.max(s.reshape(tile // SUBLANES, SUBLANES, 2 * tile), axis=0)
            cpart = part if cpart is None else jnp.maximum(cpart, part)
            p = jnp.exp2(s).astype(BF16)
            if contiguous:
                p_ref[i * tile:(i + 1) * tile, :] = p
            else:
                d = values(off, tile, p)
                pv = d if pv is None else pv + d
        if contiguous:
            n_keys = len(tiles) * tile
            pv = values(tiles[0][0], n_keys, p_ref[:n_keys, :])
        acc_ref[...] = pv if first else acc_ref[...] + pv
        x_ref[...] = cpart if first else jnp.maximum(x_ref[...], cpart)

    def finish():
        lv = lam_ref[...]
        lam = (jnp.exp(jnp.sum(lv[0:1] * lv[1:2], axis=-1, keepdims=True))
               - jnp.exp(jnp.sum(lv[2:3] * lv[3:4], axis=-1, keepdims=True)) + lam_init)
        acc = acc_ref[...]
        acc = acc[:hd] * (1.0 / acc[hd:hd + 1])
        ot = acc[:, :tile] - lam * acc[:, tile:]
        o = _rms(ot, 0).T * g_ref[...] * (1.0 - lam_init)
        o_ref[...] = o.astype(o_ref.dtype)

    n_far = jnp.maximum(qi - 1, 0)
    leftover = n_far % 2
    n_even = n_far - leftover
    plain_group(
        [(tile_off(qi), bias_rows(BIAS_DIAG)),
         (tile_off(jnp.maximum(qi - 1, 0)), bias_rows(jnp.where(qi >= 1, BIAS_PREV, BIAS_MASKED))),
         (tile_off(jnp.maximum(qi - 2, 0)), bias_rows(jnp.where(leftover == 1, BIAS_NONE, BIAS_MASKED)))],
        contiguous=False, first=True)

    def far_tiles(start, count):
        plain_group([(tile_off(start + i), None) for i in range(count)], contiguous=True, first=False)

    def trip(j, carry):
        far_tiles(j * ATTN_TRIP_TILES, ATTN_TRIP_TILES)
        return carry

    n_trips = n_even // ATTN_TRIP_TILES
    lax.fori_loop(0, n_trips, trip, 0)
    done = n_trips * ATTN_TRIP_TILES
    count = ATTN_TRIP_TILES // 2
    while count >= 2:
        take = (n_even - done) >= count

        @pl.when(take)
        def _(done=done, count=count):
            far_tiles(done, count)

        done = done + jnp.where(take, count, 0)
        count //= 2
    finish()

    col_max = jnp.max(x_ref[...], axis=0, keepdims=True)

    @pl.when(jnp.max(jnp.abs(col_max)) > SCORE_LIMIT)
    def _():
        exact_tile(tile_off(qi), bias_rows(BIAS_DIAG), first=True)

        def body(t, carry):
            exact_tile(tile_off(t), bias_rows(jnp.where(t == qi - 1, BIAS_PREV, BIAS_NONE)), first=False)
            return carry

        lax.fori_loop(0, qi, body, 0)
        finish()


def _diff_attention(lam_vecs, qt, da, vt_ones, bias, g_row, batch, seq, lam_init):
    tile = ATTN_TILE
    nq = seq // tile
    hd = 2 * DA_HEAD_DIM
    hv = vt_ones.shape[2]
    k_col0 = DA_HEADS
    row = pltpu.VMEM((1, 2 * tile), F32)
    return pl.pallas_call(
        functools.partial(_attn_body, tile=tile, lam_init=lam_init),
        grid=(batch, DA_HEADS, nq),
        in_specs=[pl.BlockSpec((4, DA_HEAD_DIM), lambda b, h, i: (0, 0)),
                  pl.BlockSpec((None, None, hd, tile), lambda b, h, i: (b, h, 0, i)),
                  pl.BlockSpec((seq, hd), lambda b, h, i: (b, k_col0 + h)),
                  pl.BlockSpec((None, None, hv, seq), lambda b, h, i: (b, h, 0, 0)),
                  pl.BlockSpec((None,) + bias.shape[1:], lambda b, h, i: (h, 0, 0)),
                  pl.BlockSpec((1, hd), lambda b, h, i: (0, 0))],
        out_specs=pl.BlockSpec((tile, hd), lambda b, h, i: (b * nq + i, h)),
        out_shape=jax.ShapeDtypeStruct((batch * seq, DA_HEADS * hd), BF16),
        scratch_shapes=[pltpu.VMEM((hd, 2 * tile), BF16),
                        row,
                        pltpu.VMEM((SUBLANES, 2 * tile), F32),
                        pltpu.VMEM((hv, 2 * tile), F32),
                        pltpu.VMEM((ATTN_TRIP_TILES * tile, 2 * tile), BF16)],
        compiler_params=_params("parallel", "parallel", "arbitrary"),
        name="diff_attn",
    )(lam_vecs, qt, da, vt_ones, bias, g_row)


def _shift_rows(x, prev, s):
    if s == 0:
        return x
    rolled = pltpu.roll(x, s, 0)
    prolled = pltpu.roll(prev, s, 0)
    r8 = lax.broadcasted_iota(jnp.int32, prev.shape, 0)
    top = jnp.where(r8 < s, prolled, rolled[:SUBLANES])
    return jnp.concatenate([top, rolled[SUBLANES:]], axis=0)


def _gdn_body(q_ref, k_ref, v_ref, z_ref, ba_ref, cwq_ref, cwk_ref, cwv_ref, alog_ref, dtb_ref,
              gn_ref, o_ref, state_ref, tail_ref):
    p = DN_PAIR
    c = DN_CHUNK

    @pl.when(pl.program_id(1) == 0)
    def _():
        state_ref[...] = jnp.zeros(state_ref.shape, F32)
        tail_ref[...] = jnp.zeros(tail_ref.shape, F32)

    ri = lax.broadcasted_iota(jnp.int32, (p, p), 0)
    ci = lax.broadcasted_iota(jnp.int32, (p, p), 1)
    same = (ri // c) == (ci // c)
    causal = same & (ri >= ci)
    strict = same & (ri > ci)
    cum01 = jnp.where(causal, 1.0, 0.0).astype(BF16)

    ba = ba_ref[...]
    beta_all = _sigmoid(ba)
    xs = ba + dtb_ref[...]
    softplus = jnp.maximum(xs, 0.0) + jnp.log(1.0 + jnp.exp(-jnp.abs(xs)))
    g_all = -jnp.exp(alog_ref[...]) * softplus

    def conv(x_ref, w_ref, which, h):
        cols = slice(h * DN_DIM, (h + 1) * DN_DIM)
        x = x_ref[:, cols]
        prev = tail_ref[which, :, cols]
        w = w_ref[:, cols]
        y = w[DN_CONV - 1:DN_CONV] * x
        for s in range(1, DN_CONV):
            y = y + w[DN_CONV - 1 - s:DN_CONV - s] * _shift_rows(x, prev, s)
        tail_ref[which, :, cols] = x[p - SUBLANES:]
        return _silu(y)

    gc_all = _mm_exact_rhs(cum01, g_all)
    glast_all = jnp.concatenate(
        [jnp.broadcast_to(gc_all[c - 1:c], (c, LANES)), jnp.broadcast_to(gc_all[p - 1:p], (c, LANES))], axis=0)
    expg_all = jnp.exp(gc_all)
    tail_all = jnp.exp(glast_all - gc_all)
    dl_all = jnp.exp(glast_all)

    def lane(x, j):
        return jnp.broadcast_to(x[:, j:j + 1], (x.shape[0], DN_DIM))

    heads = range(DN_HEADS)
    q = [conv(q_ref, cwq_ref, 0, h) for h in heads]
    k = [conv(k_ref, cwk_ref, 1, h) for h in heads]
    v = [conv(v_ref, cwv_ref, 2, h) for h in heads]
    q = [x * lax.rsqrt(jnp.sum(x * x, axis=-1, keepdims=True) + EPS) * (DN_DIM ** -0.5) for x in q]
    k = [x * lax.rsqrt(jnp.sum(x * x, axis=-1, keepdims=True) + EPS) for x in k]

    gc = [lane(gc_all, DN_HEADS + h) for h in heads]
    decay = [jnp.exp(jnp.where(causal, x - x.T, MASK_VALUE)) for x in gc]
    exp_g = [lane(expg_all, DN_HEADS + h) for h in heads]
    kb = [k[h] * lane(beta_all, h) for h in heads]
    vb = [v[h] * lane(beta_all, h) for h in heads]
    low = [jnp.where(strict, _mm_nt(kb[h], k[h]) * decay[h], 0.0) for h in heads]
    intra = [jnp.where(causal, _mm_nt(q[h], k[h]) * decay[h], 0.0) for h in heads]
    y = [-x for x in low]
    pw = low
    for _ in range(5):
        pw = [_mm(x, x) for x in pw]
        y = [y[h] + pw[h] + _mm(y[h], pw[h]) for h in heads]
    rhs = [jnp.concatenate([vb[h], kb[h] * exp_g[h]], axis=1) for h in heads]
    sol = [rhs[h] + _mm(y[h], rhs[h]) for h in heads]
    q_dec = [q[h] * exp_g[h] for h in heads]
    kt_t = [(k[h] * lane(tail_all, DN_HEADS + h)).T for h in heads]

    outs = [[], []]
    zeros = jnp.zeros((c, DN_DIM), F32)
    for ch in range(2):
        rows = slice(ch * c, (ch + 1) * c)
        st = [state_ref[h] for h in heads]
        v_new = [sol[h][rows, :DN_DIM] - _mm(sol[h][rows, DN_DIM:], st[h]) for h in heads]
        v_pad = [jnp.concatenate([x, zeros] if ch == 0 else [zeros, x], axis=0) for x in v_new]
        outs[ch] = [_mm(q_dec[h][rows], st[h]) + _mm(intra[h][rows], v_pad[h]) for h in heads]
        for h in heads:
            dl = jnp.broadcast_to(dl_all[ch * c:ch * c + 1, DN_HEADS + h:DN_HEADS + h + 1], (1, DN_DIM))
            state_ref[h] = st[h] * dl + _mm(kt_t[h], v_pad[h])
    for h in heads:
        cols = slice(h * DN_DIM, (h + 1) * DN_DIM)
        o = jnp.concatenate([outs[0][h], outs[1][h]], axis=0)
        yb = _rms(o, -1) * gn_ref[...] * _silu(z_ref[:, cols])
        o_ref[:, cols] = yb.astype(o_ref.dtype)


def _gdn(dn, ba, conv_w, alog_row, dtb_row, gn_row, batch, seq):
    t = batch * seq
    p = DN_PAIR
    nt = seq // p
    width = DN_HEADS * DN_DIM
    tok = lambda col: pl.BlockSpec((p, width), lambda b, i: (b * nt + i, col))
    cw = lambda col: pl.BlockSpec((DN_CONV, width), lambda b, i: (0, col))
    row = pl.BlockSpec((1, LANES), lambda b, i: (0, 0))
    return pl.pallas_call(
        _gdn_body,
        grid=(batch, nt),
        in_specs=[tok(0), tok(1), tok(2), tok(3),
                  pl.BlockSpec((p, LANES), lambda b, i: (b * nt + i, 0)),
                  cw(0), cw(1), cw(2), row, row, row],
        out_specs=pl.BlockSpec((p, width), lambda b, i: (b * nt + i, 0)),
        out_shape=jax.ShapeDtypeStruct((t, width), BF16),
        scratch_shapes=[pltpu.VMEM((DN_HEADS, DN_DIM, DN_DIM), F32),
                        pltpu.VMEM((3, SUBLANES, width), F32)],
        compiler_params=_params("parallel", "arbitrary"),
        name="gdn",
    )(dn, dn, dn, dn, ba, conv_w, conv_w, conv_w, alog_row, dtb_row, gn_row)


def _merge_body(x_ref, cq_ref, g0_ref, g1_ref, g2_ref, ya_ref, yb_ref, mkt_ref, mv_ref,
                wb_ref, wo_ref, ln_ref, o_ref):
    cq = cq_ref[...]
    heads = []
    for h in range(CA_HEADS):
        qh = cq[:, h * CA_HEAD_DIM:(h + 1) * CA_HEAD_DIM]
        s = jnp.dot(qh, mkt_ref[h], preferred_element_type=F32) * (CA_HEAD_DIM ** -0.5)
        e = jnp.exp(s - jnp.max(s, axis=-1, keepdims=True))
        pr = e / jnp.sum(e, axis=-1, keepdims=True)
        heads.append(jnp.dot(pr.astype(BF16), mv_ref[h], preferred_element_type=F32))
    yc = jnp.concatenate(heads, axis=1)

    mixed = _sigmoid(g0_ref[...].astype(F32)) * jnp.dot(ya_ref[...], wb_ref[0], preferred_element_type=F32)
    mixed = mixed + _sigmoid(g1_ref[...].astype(F32)) * jnp.dot(yb_ref[...], wb_ref[1], preferred_element_type=F32)
    mixed = mixed + _sigmoid(g2_ref[...].astype(F32)) * _mm(yc, wb_ref[2])
    out = _mm(mixed, wo_ref[...])
    o_ref[...] = x_ref[...] + _rms(out, -1) * ln_ref[...]


def _merge(x2d, cg, ya, yb, mkt, mv, wb, wo, ln_row, batch, seq, tm):
    t, d = x2d.shape
    nt = seq // tm
    tok = lambda col: pl.BlockSpec((tm, d), lambda i: (i, col))
    mem = pl.BlockSpec((None, CA_HEADS, CA_HEAD_DIM, CA_HEAD_DIM), lambda i: (i // nt, 0, 0, 0))
    return pl.pallas_call(
        _merge_body,
        grid=(t // tm,),
        in_specs=[tok(0), tok(0), tok(1), tok(2), tok(3), tok(0), tok(0), mem, mem,
                  pl.BlockSpec((3, d, d), lambda i: (0, 0, 0)),
                  pl.BlockSpec((d, d), lambda i: (0, 0)),
                  pl.BlockSpec((1, d), lambda i: (0, 0))],
        out_specs=tok(0),
        out_shape=jax.ShapeDtypeStruct((t, d), F32),
        compiler_params=_params("parallel"),
        name="merge",
    )(x2d, cg, cg, cg, cg, ya, yb, mkt, mv, wb, wo, ln_row)


def _ffn_body(h_ref, pre_ref, post_ref, w1_ref, w2_ref, o_ref, *, chunk):
    hh = h_ref[...]
    n = (_rms(hh, -1) * pre_ref[...]).astype(BF16)
    d_ff = w1_ref.shape[1]
    f = jnp.zeros(hh.shape, F32)
    for c0 in range(0, d_ff, chunk):
        a = jnp.dot(n, w1_ref[:, c0:c0 + chunk], preferred_element_type=F32)
        a = jnp.square(jnp.maximum(a, 0.0))
        f = f + jnp.dot(a.astype(BF16), w2_ref[c0:c0 + chunk, :], preferred_element_type=F32)
    o_ref[...] = hh + _rms(f, -1) * post_ref[...]


def _ffn(h2d, pre_row, post_row, w1, w2, tm):
    t, d = h2d.shape
    d_ff = w1.shape[1]
    return pl.pallas_call(
        functools.partial(_ffn_body, chunk=1024),
        grid=(t // tm,),
        in_specs=[pl.BlockSpec((tm, d), lambda i: (i, 0)),
                  pl.BlockSpec((1, d), lambda i: (0, 0)),
                  pl.BlockSpec((1, d), lambda i: (0, 0)),
                  pl.BlockSpec((d, d_ff), lambda i: (0, 0)),
                  pl.BlockSpec((d_ff, d), lambda i: (0, 0))],
        out_specs=pl.BlockSpec((tm, d), lambda i: (i, 0)),
        out_shape=jax.ShapeDtypeStruct((t, d), F32),
        compiler_params=_params("parallel"),
        name="ffn",
    )(h2d, pre_row, post_row, w1, w2)


def _pad_lanes(vec, offset):
    return jnp.zeros((1, LANES), F32).at[0, offset:offset + vec.shape[0]].set(vec.astype(F32))


def _layer(h2d, mem2d, rel_bias, w_in, conv_w, dn_a_log, dn_dt_bias, dn_norm_g, da_lambda,
           da_subln_g, mem_norm_g, w_mem_kv, w_branch, w_out, ln_mix_pre, ln_mix_post,
           ln_ff_pre, ln_ff_post, w_ff1, w_ff2, batch, seq, mem_len, layer):
    d = h2d.shape[1]
    t = batch * seq
    lam_init = 0.8 - 0.6 * math.exp(-0.3 * layer)
    tm = min(1024, seq)
    row = lambda v: v.reshape(1, -1).astype(F32)

    da_w = 3 * d
    dn_w = 4 * d
    small0 = da_w + dn_w
    rest0 = small0 + 2 * DN_HEADS
    scale_q = jnp.concatenate([jnp.full((d,), DA_HEAD_DIM ** -0.5 * LOG2E, F32), jnp.ones((2 * d,), F32)])
    w_da = (w_in[:, :da_w] * scale_q).astype(BF16)
    w_dn = w_in[:, da_w:small0].astype(BF16)
    w_small = jnp.zeros((d, LANES), F32).at[:, :2 * DN_HEADS].set(w_in[:, small0:rest0]).astype(BF16)
    w_rest = w_in[:, rest0:].astype(BF16)
    g_pre = row(ln_mix_pre)

    da = _norm_matmul(h2d, g_pre, w_da, 0, da_w, BF16, tm, 1024)
    dn = _norm_matmul(h2d, g_pre, w_dn, 0, dn_w, F32, tm, 1024)
    ba = _norm_matmul(h2d, g_pre, w_small, 0, LANES, F32, tm, LANES)
    cg = _norm_matmul(h2d, g_pre, w_rest, 0, 4 * d, BF16, tm, 1024)
    mkv = _norm_matmul(mem2d, row(mem_norm_g), w_mem_kv.astype(BF16), 0, 2 * d, BF16,
                       min(512, batch * mem_len), 1024)

    hd = 2 * DA_HEAD_DIM
    qt = da[:, :d].reshape(batch, seq, DA_HEADS, hd).transpose(0, 2, 3, 1)
    vt = da[:, 2 * d:].reshape(batch, seq, DA_HEADS, hd).transpose(0, 2, 3, 1)
    vt_ones = jnp.concatenate([vt, jnp.ones((batch, DA_HEADS, 2 * SUBLANES, seq), BF16)], axis=2)
    bias = _bias_table(rel_bias.astype(F32), ATTN_TILE)
    y_a = _diff_attention(da_lambda.astype(F32), qt, da, vt_ones, bias, row(da_subln_g), batch, seq, lam_init)

    y_b = _gdn(dn, ba, conv_w.astype(F32), _pad_lanes(dn_a_log, DN_HEADS), _pad_lanes(dn_dt_bias, DN_HEADS),
               row(dn_norm_g), batch, seq)

    mkt = mkv[:, :d].reshape(batch, mem_len, CA_HEADS, CA_HEAD_DIM).transpose(0, 2, 3, 1)
    mv = mkv[:, d:].reshape(batch, mem_len, CA_HEADS, CA_HEAD_DIM).transpose(0, 2, 1, 3)
    h1 = _merge(h2d, cg, y_a, y_b, mkt, mv, w_branch.astype(BF16), w_out.astype(BF16),
                row(ln_mix_post), batch, seq, min(512, seq))

    return _ffn(h1, row(ln_ff_pre), row(ln_ff_post), w_ff1.astype(BF16), w_ff2.astype(BF16), min(512, seq))


def kernel(x, mem, rel_bias, w_in, conv_w, dn_a_log, dn_dt_bias, dn_norm_g, da_lambda, da_subln_g,
           mem_norm_g, w_mem_kv, w_branch, w_out, ln_mix_pre, ln_mix_post, ln_ff_pre, ln_ff_post,
           w_ff1, w_ff2):
    batch, seq, d = x.shape
    mem_len = mem.shape[1]
    assert seq % ATTN_TILE == 0 and seq % DN_PAIR == 0
    h2d = x.reshape(batch * seq, d)
    mem2d = mem.reshape(batch * mem_len, d)
    for layer in range(w_in.shape[0]):
        h2d = _layer(h2d, mem2d, rel_bias, w_in[layer], conv_w[layer], dn_a_log[layer], dn_dt_bias[layer],
                     dn_norm_g[layer], da_lambda[layer], da_subln_g[layer], mem_norm_g[layer],
                     w_mem_kv[layer], w_branch[layer], w_out[layer], ln_mix_pre[layer], ln_mix_post[layer],
                     ln_ff_pre[layer], ln_ff_post[layer], w_ff1[layer], w_ff2[layer],
                     batch, seq, mem_len, layer)
    return h2d.reshape(batch, seq, d)
```

```python
import functools
import math

import numpy as np
import jax
import jax.numpy as jnp
from jax import lax
from jax.experimental import pallas as pl
from jax.experimental.pallas import tpu as pltpu

F32 = jnp.float32
BF16 = jnp.bfloat16

EPS = 1e-6
MASK_VALUE = -1e30
LOG2E = math.log2(math.e)
LANES = 128
SUBLANES = 8
VMEM_LIMIT_BYTES = 56 * 1024 * 1024

DA_HEADS = 8
DA_HEAD_DIM = 64
DN_HEADS = 8
DN_DIM = 128
DN_CONV = 4
DN_CHUNK = 64
DN_PAIR = 2 * DN_CHUNK
CA_HEADS = 4
CA_HEAD_DIM = 256
REL_BUCKETS = 32
REL_MAX_EXACT = 16
REL_MAX_DIST = 128

ATTN_TILE = 256
ATTN_QTILES = 2
ATTN_TRIP_TILES = 8
SCORE_LIMIT = 60.0


def _params(*sem):
    return pltpu.CompilerParams(dimension_semantics=sem, vmem_limit_bytes=VMEM_LIMIT_BYTES)


def _mm(a, b):
    return jnp.dot(a.astype(BF16), b.astype(BF16), preferred_element_type=F32)


def _mm_nt(a, b):
    return lax.dot_general(a.astype(BF16), b.astype(BF16), (((1,), (1,)), ((), ())),
                           preferred_element_type=F32)


def _mm_exact_rhs(m01, a):
    hi = a.astype(BF16)
    r1 = a - hi.astype(F32)
    mid = r1.astype(BF16)
    lo = (r1 - mid.astype(F32)).astype(BF16)
    dot = functools.partial(jnp.dot, preferred_element_type=F32)
    return dot(m01, hi) + dot(m01, mid) + dot(m01, lo)


def _rms(x, axis):
    return x * lax.rsqrt(jnp.mean(x * x, axis=axis, keepdims=True) + EPS)


def _sigmoid(x):
    return 1.0 / (1.0 + jnp.exp(-x))


def _silu(x):
    return x * _sigmoid(x)


def _normed(x_ref, g_ref):
    return (_rms(x_ref[...], -1) * g_ref[...]).astype(BF16)


def _norm_matmul_body(x_ref, g_ref, *refs, chunk):
    n_out = len(refs) // 2
    n = _normed(x_ref, g_ref)
    for w_ref, o_ref in zip(refs[:n_out], refs[n_out:]):
        cols = w_ref.shape[1]
        step = min(chunk, cols)
        for c0 in range(0, cols, step):
            o_ref[:, c0:c0 + step] = jnp.dot(n, w_ref[:, c0:c0 + step],
                                             preferred_element_type=F32).astype(o_ref.dtype)


def _norm_matmul(x2d, g_row, weights, out_dtypes, tm):
    t, d = x2d.shape
    return pl.pallas_call(
        functools.partial(_norm_matmul_body, chunk=1024),
        grid=(t // tm,),
        in_specs=[pl.BlockSpec((tm, d), lambda i: (i, 0)),
                  pl.BlockSpec((1, d), lambda i: (0, 0))]
                 + [pl.BlockSpec(w.shape, lambda i: (0, 0)) for w in weights],
        out_specs=[pl.BlockSpec((tm, w.shape[1]), lambda i: (i, 0)) for w in weights],
        out_shape=[jax.ShapeDtypeStruct((t, w.shape[1]), dt) for w, dt in zip(weights, out_dtypes)],
        compiler_params=_params("parallel"),
        name="norm_matmul",
    )(x2d, g_row, *weights)


def _norm_matmul_t_body(x_ref, g_ref, wt_ref, o_ref, *, chunk):
    n = _normed(x_ref, g_ref)
    for r0 in range(0, wt_ref.shape[0], chunk):
        o_ref[r0:r0 + chunk, :] = lax.dot_general(
            wt_ref[r0:r0 + chunk, :], n, (((1,), (1,)), ((), ())),
            preferred_element_type=F32).astype(o_ref.dtype)


def _norm_matmul_t(x2d, g_row, wt, out_dtype, tm):
    t, d = x2d.shape
    rows = wt.shape[0]
    return pl.pallas_call(
        functools.partial(_norm_matmul_t_body, chunk=512),
        grid=(t // tm,),
        in_specs=[pl.BlockSpec((tm, d), lambda i: (i, 0)),
                  pl.BlockSpec((1, d), lambda i: (0, 0)),
                  pl.BlockSpec((rows, d), lambda i: (0, 0))],
        out_specs=pl.BlockSpec((rows, tm), lambda i: (0, i)),
        out_shape=jax.ShapeDtypeStruct((rows, t), out_dtype),
        compiler_params=_params("parallel"),
        name="norm_matmul_t",
    )(x2d, g_row, wt)


BIAS_PREV, BIAS_DIAG, BIAS_MASKED, BIAS_NONE = 0, 1, 2, 3


def _bucket_table(tile):
    k = np.arange(tile)[:, None]
    q = np.arange(tile)[None, :]

    def bucket(dist):
        nf = np.maximum(dist, 1).astype(np.float64)
        large = REL_MAX_EXACT + np.trunc(
            np.log(nf / REL_MAX_EXACT) / math.log(REL_MAX_DIST / REL_MAX_EXACT)
            * (REL_BUCKETS - REL_MAX_EXACT)).astype(np.int64)
        large = np.minimum(large, REL_BUCKETS - 1)
        return np.where(dist < REL_MAX_EXACT, dist, large)

    prev = bucket(q - k + tile)
    diag = np.where(q >= k, bucket(np.maximum(q - k, 0)), -1)
    masked = np.full((tile, tile), -1)
    far = np.full((tile, tile), REL_BUCKETS - 1)
    return np.concatenate([prev, diag, masked, far], axis=0).astype(np.int32)


def _bias_table_body(rb_ref, bucket_ref, o_ref):
    h = pl.program_id(0)
    far = rb_ref[REL_BUCKETS - 1, h]
    bk = bucket_ref[...]
    out = jnp.full(bk.shape, MASK_VALUE, F32)
    for b in range(REL_BUCKETS):
        out = jnp.where(bk == b, (rb_ref[b, h] - far) * LOG2E, out)
    o_ref[...] = out


def _bias_table(rel_bias, tile):
    heads = rel_bias.shape[1]
    buckets = jnp.asarray(_bucket_table(tile))
    rows = buckets.shape[0]
    return pl.pallas_call(
        _bias_table_body,
        grid=(heads,),
        in_specs=[pl.BlockSpec(memory_space=pltpu.SMEM),
                  pl.BlockSpec((rows, tile), lambda h: (0, 0))],
        out_specs=pl.BlockSpec((None, rows, tile), lambda h: (h, 0, 0)),
        out_shape=jax.ShapeDtypeStruct((heads, rows, tile), F32),
        compiler_params=_params("arbitrary"),
        name="bias_table",
    )(rel_bias, buckets)


def _attn_body(lam_ref, qt_ref, k_ref, vt_ref, bias_ref, g_ref, o_ref,
               qz_ref, m_ref, x_ref, acc_ref, p_ref, *, tile, nsub, lam_init):
    j = pl.program_id(2)
    hd = 2 * DA_HEAD_DIM
    qw = nsub * tile
    t0 = nsub * j

    qt = qt_ref[...]
    row = lax.broadcasted_iota(jnp.int32, qt.shape, 0)
    zero = jnp.zeros_like(qt)
    qz_ref[:, :qw] = jnp.where(row < DA_HEAD_DIM, qt, zero)
    qz_ref[:, qw:] = jnp.where(row >= DA_HEAD_DIM, qt, zero)

    def tile_off(t):
        return pl.multiple_of(jnp.maximum(t, 0) * tile, tile)

    def bias_cols(t, valid=None):
        parts = []
        for a in range(nsub):
            rel = t - (t0 + a)
            blk = jnp.where(rel == 0, BIAS_DIAG,
                            jnp.where(rel == -1, BIAS_PREV, jnp.where(rel < -1, BIAS_NONE, BIAS_MASKED)))
            if valid is not None:
                blk = jnp.where(valid, blk, BIAS_MASKED)
            parts.append(bias_ref[pl.ds(pl.multiple_of(blk * tile, tile), tile), :])
        return jnp.concatenate(parts + parts, axis=1)

    def scores(off, bias):
        s = jnp.dot(k_ref[pl.ds(off, tile), :], qz_ref[...], preferred_element_type=F32)
        return s if bias is None else s + bias

    def values(off, n_keys, p):
        vt = jnp.concatenate([vt_ref[:, pl.ds(off, n_keys)], jnp.ones((2 * SUBLANES, n_keys), BF16)], axis=0)
        return jnp.dot(vt, p, preferred_element_type=F32)

    def exact_tile(off, bias, first):
        s = scores(off, bias)
        c = jnp.max(s, axis=0, keepdims=True)
        if first:
            m_new = c
        else:
            m_old = m_ref[...]
            m_new = jnp.maximum(m_old, c)
        pv = values(off, tile, jnp.exp2(s - m_new).astype(BF16))
        acc_ref[...] = pv if first else jnp.exp2(m_old - m_new) * acc_ref[...] + pv
        m_ref[...] = m_new

    def plain_group(tiles, contiguous, first):
        cpart = None
        pv = None
        for i, (off, bias) in enumerate(tiles):
            s = scores(off, bias)
            part = jnp.max(s.reshape(tile // SUBLANES, SUBLANES, 2 * qw), axis=0)
            cpart = part if cpart is None else jnp.maximum(cpart, part)
            p = jnp.exp2(s).astype(BF16)
            if contiguous:
                p_ref[i * tile:(i + 1) * tile, :] = p
            else:
                d = values(off, tile, p)
                pv = d if pv is None else pv + d
        if contiguous:
            n_keys = len(tiles) * tile
            pv = values(tiles[0][0], n_keys, p_ref[:n_keys, :])
        acc_ref[...] = pv if first else acc_ref[...] + pv
        x_ref[...] = cpart if first else jnp.maximum(x_ref[...], cpart)

    def finish():
        lv = lam_ref[...]
        lam = (jnp.exp(jnp.sum(lv[0:1] * lv[1:2], axis=-1, keepdims=True))
               - jnp.exp(jnp.sum(lv[2:3] * lv[3:4], axis=-1, keepdims=True)) + lam_init)
        acc = acc_ref[...]
        acc = acc[:hd] * (1.0 / acc[hd:hd + 1])
        ot = acc[:, :qw] - lam * acc[:, qw:]
        o = _rms(ot, 0).T * g_ref[...] * (1.0 - lam_init)
        o_ref[...] = o.astype(o_ref.dtype)

    n_far = jnp.maximum(t0 - 1, 0)
    leftover = n_far % 2
    n_even = n_far - leftover
    near = [(t0 + c, None) for c in range(nsub)] + [(t0 - 1, j >= 1), (t0 - 2, leftover == 1)]
    plain_group([(tile_off(t), bias_cols(t, valid)) for t, valid in near], contiguous=False, first=True)

    def far_tiles(start, count):
        plain_group([(tile_off(start + i), None) for i in range(count)], contiguous=True, first=False)

    def trip(u, carry):
        far_tiles(u * ATTN_TRIP_TILES, ATTN_TRIP_TILES)
        return carry

    n_trips = n_even // ATTN_TRIP_TILES
    lax.fori_loop(0, n_trips, trip, 0)
    done = n_trips * ATTN_TRIP_TILES
    count = ATTN_TRIP_TILES // 2
    while count >= 2:
        take = (n_even - done) >= count

        @pl.when(take)
        def _(done=done, count=count):
            far_tiles(done, count)

        done = done + jnp.where(take, count, 0)
        count //= 2
    finish()

    col_max = jnp.max(x_ref[...], axis=0, keepdims=True)

    @pl.when(jnp.max(jnp.abs(col_max)) > SCORE_LIMIT)
    def _():
        exact_tile(tile_off(t0), bias_cols(t0), first=True)

        def body(u, carry):
            t = u + jnp.where(u >= t0, 1, 0)
            exact_tile(tile_off(t), bias_cols(t), first=False)
            return carry

        lax.fori_loop(0, t0 + nsub - 1, body, 0)
        finish()


def _diff_attention(lam_vecs, qvt, kdn, bias, g_row, batch, seq, lam_init):
    tile = ATTN_TILE
    qw = ATTN_QTILES * tile
    assert seq % qw == 0
    nqt = seq // qw
    hd = 2 * DA_HEAD_DIM
    acc_rows = hd + 2 * SUBLANES
    return pl.pallas_call(
        functools.partial(_attn_body, tile=tile, nsub=ATTN_QTILES, lam_init=lam_init),
        grid=(batch, DA_HEADS, nqt),
        in_specs=[pl.BlockSpec((4, DA_HEAD_DIM), lambda b, h, i: (0, 0)),
                  pl.BlockSpec((hd, qw), lambda b, h, i: (h, b * nqt + i)),
                  pl.BlockSpec((seq, hd), lambda b, h, i: (b, h)),
                  pl.BlockSpec((hd, seq), lambda b, h, i: (DA_HEADS + h, b)),
                  pl.BlockSpec((None,) + bias.shape[1:], lambda b, h, i: (h, 0, 0)),
                  pl.BlockSpec((1, hd), lambda b, h, i: (0, 0))],
        out_specs=pl.BlockSpec((qw, hd), lambda b, h, i: (b * nqt + i, h)),
        out_shape=jax.ShapeDtypeStruct((batch * seq, DA_HEADS * hd), BF16),
        scratch_shapes=[pltpu.VMEM((hd, 2 * qw), BF16),
                        pltpu.VMEM((1, 2 * qw), F32),
                        pltpu.VMEM((SUBLANES, 2 * qw), F32),
                        pltpu.VMEM((acc_rows, 2 * qw), F32),
                        pltpu.VMEM((ATTN_TRIP_TILES * tile, 2 * qw), BF16)],
        compiler_params=_params("parallel", "parallel", "arbitrary"),
        name="diff_attn",
    )(lam_vecs, qvt, kdn, qvt, bias, g_row)


def _shift_rows(x, prev, s):
    if s == 0:
        return x
    rolled = pltpu.roll(x, s, 0)
    prolled = pltpu.roll(prev, s, 0)
    r8 = lax.broadcasted_iota(jnp.int32, prev.shape, 0)
    top = jnp.where(r8 < s, prolled, rolled[:SUBLANES])
    return jnp.concatenate([top, rolled[SUBLANES:]], axis=0)


def _gdn_body(q_ref, k_ref, v_ref, z_ref, ba_ref, cwq_ref, cwk_ref, cwv_ref, alog_ref, dtb_ref,
              gn_ref, o_ref, state_ref, tail_ref):
    p = DN_PAIR
    c = DN_CHUNK

    @pl.when(pl.program_id(1) == 0)
    def _():
        state_ref[...] = jnp.zeros(state_ref.shape, F32)
        tail_ref[...] = jnp.zeros(tail_ref.shape, F32)

    ri = lax.broadcasted_iota(jnp.int32, (p, p), 0)
    ci = lax.broadcasted_iota(jnp.int32, (p, p), 1)
    same = (ri // c) == (ci // c)
    causal = same & (ri >= ci)
    strict = same & (ri > ci)
    cum01 = jnp.where(causal, 1.0, 0.0).astype(BF16)

    ba = ba_ref[...]
    beta_all = _sigmoid(ba)
    xs = ba + dtb_ref[...]
    softplus = jnp.maximum(xs, 0.0) + jnp.log(1.0 + jnp.exp(-jnp.abs(xs)))
    g_all = -jnp.exp(alog_ref[...]) * softplus

    def conv(x_ref, w_ref, which, h):
        cols = slice(h * DN_DIM, (h + 1) * DN_DIM)
        x = x_ref[:, cols].astype(F32)
        prev = tail_ref[which, :, cols]
        w = w_ref[:, cols]
        y = w[DN_CONV - 1:DN_CONV] * x
        for s in range(1, DN_CONV):
            y = y + w[DN_CONV - 1 - s:DN_CONV - s] * _shift_rows(x, prev, s)
        tail_ref[which, :, cols] = x[p - SUBLANES:]
        return _silu(y)

    gc_all = _mm_exact_rhs(cum01, g_all)
    glast_all = jnp.concatenate(
        [jnp.broadcast_to(gc_all[c - 1:c], (c, LANES)), jnp.broadcast_to(gc_all[p - 1:p], (c, LANES))], axis=0)
    expg_all = jnp.exp(gc_all)
    tail_all = jnp.exp(glast_all - gc_all)
    dl_all = jnp.exp(glast_all)

    def lane(x, j):
        return jnp.broadcast_to(x[:, j:j + 1], (x.shape[0], DN_DIM))

    heads = range(DN_HEADS)
    q = [conv(q_ref, cwq_ref, 0, h) for h in heads]
    k = [conv(k_ref, cwk_ref, 1, h) for h in heads]
    v = [conv(v_ref, cwv_ref, 2, h) for h in heads]
    q = [x * lax.rsqrt(jnp.sum(x * x, axis=-1, keepdims=True) + EPS) * (DN_DIM ** -0.5) for x in q]
    k = [x * lax.rsqrt(jnp.sum(x * x, axis=-1, keepdims=True) + EPS) for x in k]

    gc = [lane(gc_all, DN_HEADS + h) for h in heads]
    decay = [jnp.exp(jnp.where(causal, x - x.T, MASK_VALUE)) for x in gc]
    exp_g = [lane(expg_all, DN_HEADS + h) for h in heads]
    kb = [k[h] * lane(beta_all, h) for h in heads]
    vb = [v[h] * lane(beta_all, h) for h in heads]
    low = [jnp.where(strict, _mm_nt(kb[h], k[h]) * decay[h], 0.0) for h in heads]
    intra = [jnp.where(causal, _mm_nt(q[h], k[h]) * decay[h], 0.0) for h in heads]
    y = [-x for x in low]
    pw = low
    for _ in range(5):
        pw = [_mm(x, x) for x in pw]
        y = [y[h] + pw[h] + _mm(y[h], pw[h]) for h in heads]
    rhs = [jnp.concatenate([vb[h], kb[h] * exp_g[h]], axis=1) for h in heads]
    sol = [rhs[h] + _mm(y[h], rhs[h]) for h in heads]
    q_dec = [q[h] * exp_g[h] for h in heads]
    kt_t = [(k[h] * lane(tail_all, DN_HEADS + h)).T for h in heads]

    outs = [[], []]
    zeros = jnp.zeros((c, DN_DIM), F32)
    for ch in range(2):
        rows = slice(ch * c, (ch + 1) * c)
        st = [state_ref[h] for h in heads]
        v_new = [sol[h][rows, :DN_DIM] - _mm(sol[h][rows, DN_DIM:], st[h]) for h in heads]
        v_pad = [jnp.concatenate([x, zeros] if ch == 0 else [zeros, x], axis=0) for x in v_new]
        outs[ch] = [_mm(q_dec[h][rows], st[h]) + _mm(intra[h][rows], v_pad[h]) for h in heads]
        for h in heads:
            dl = jnp.broadcast_to(dl_all[ch * c:ch * c + 1, DN_HEADS + h:DN_HEADS + h + 1], (1, DN_DIM))
            state_ref[h] = st[h] * dl + _mm(kt_t[h], v_pad[h])
    for h in heads:
        cols = slice(h * DN_DIM, (h + 1) * DN_DIM)
        o = jnp.concatenate([outs[0][h], outs[1][h]], axis=0)
        yb = _rms(o, -1) * gn_ref[...] * _silu(z_ref[:, cols].astype(F32))
        o_ref[:, cols] = yb.astype(o_ref.dtype)


def _gdn(kdn, ba, conv_w, alog_row, dtb_row, gn_row, batch, seq):
    t = batch * seq
    p = DN_PAIR
    nt = seq // p
    width = DN_HEADS * DN_DIM
    tok = lambda col: pl.BlockSpec((p, width), lambda b, i: (b * nt + i, col))
    cw = lambda col: pl.BlockSpec((DN_CONV, width), lambda b, i: (0, col))
    row = pl.BlockSpec((1, LANES), lambda b, i: (0, 0))
    return pl.pallas_call(
        _gdn_body,
        grid=(batch, nt),
        in_specs=[tok(1), tok(2), tok(3), tok(4),
                  pl.BlockSpec((p, LANES), lambda b, i: (b * nt + i, 0)),
                  cw(0), cw(1), cw(2), row, row, row],
        out_specs=pl.BlockSpec((p, width), lambda b, i: (b * nt + i, 0)),
        out_shape=jax.ShapeDtypeStruct((t, width), BF16),
        scratch_shapes=[pltpu.VMEM((DN_HEADS, DN_DIM, DN_DIM), F32),
                        pltpu.VMEM((3, SUBLANES, width), F32)],
        compiler_params=_params("parallel", "arbitrary"),
        name="gdn",
    )(kdn, kdn, kdn, kdn, ba, conv_w, conv_w, conv_w, alog_row, dtb_row, gn_row)


def _merge_body(x_ref, cq_ref, g0_ref, g1_ref, g2_ref, ya_ref, yb_ref, mkt_ref, mv_ref,
                wb_ref, wo_ref, ln_ref, o_ref):
    cq = cq_ref[...]
    heads = []
    for h in range(CA_HEADS):
        qh = cq[:, h * CA_HEAD_DIM:(h + 1) * CA_HEAD_DIM]
        s = jnp.dot(qh, mkt_ref[h], preferred_element_type=F32) * (CA_HEAD_DIM ** -0.5)
        e = jnp.exp(s - jnp.max(s, axis=-1, keepdims=True))
        pr = e / jnp.sum(e, axis=-1, keepdims=True)
        heads.append(jnp.dot(pr.astype(BF16), mv_ref[h], preferred_element_type=F32))
    yc = jnp.concatenate(heads, axis=1)

    mixed = _sigmoid(g0_ref[...].astype(F32)) * jnp.dot(ya_ref[...], wb_ref[0], preferred_element_type=F32)
    mixed = mixed + _sigmoid(g1_ref[...].astype(F32)) * jnp.dot(yb_ref[...], wb_ref[1], preferred_element_type=F32)
    mixed = mixed + _sigmoid(g2_ref[...].astype(F32)) * _mm(yc, wb_ref[2])
    out = _mm(mixed, wo_ref[...])
    o_ref[...] = x_ref[...] + _rms(out, -1) * ln_ref[...]


def _merge(x2d, cg, ya, yb, mkt, mv, wb, wo, ln_row, batch, seq, tm):
    t, d = x2d.shape
    nt = seq // tm
    tok = lambda col: pl.BlockSpec((tm, d), lambda i: (i, col))
    mem = pl.BlockSpec((None, CA_HEADS, CA_HEAD_DIM, CA_HEAD_DIM), lambda i: (i // nt, 0, 0, 0))
    return pl.pallas_call(
        _merge_body,
        grid=(t // tm,),
        in_specs=[tok(0), tok(0), tok(1), tok(2), tok(3), tok(0), tok(0), mem, mem,
                  pl.BlockSpec((3, d, d), lambda i: (0, 0, 0)),
                  pl.BlockSpec((d, d), lambda i: (0, 0)),
                  pl.BlockSpec((1, d), lambda i: (0, 0))],
        out_specs=tok(0),
        out_shape=jax.ShapeDtypeStruct((t, d), F32),
        compiler_params=_params("parallel"),
        name="merge",
    )(x2d, cg, cg, cg, cg, ya, yb, mkt, mv, wb, wo, ln_row)


def _ffn_body(h_ref, pre_ref, post_ref, w1_ref, w2_ref, o_ref, *, chunk):
    hh = h_ref[...]
    n = (_rms(hh, -1) * pre_ref[...]).astype(BF16)
    d_ff = w1_ref.shape[1]
    f = jnp.zeros(hh.shape, F32)
    for c0 in range(0, d_ff, chunk):
        a = jnp.dot(n, w1_ref[:, c0:c0 + chunk], preferred_element_type=F32)
        a = jnp.square(jnp.maximum(a, 0.0))
        f = f + jnp.dot(a.astype(BF16), w2_ref[c0:c0 + chunk, :], preferred_element_type=F32)
    o_ref[...] = hh + _rms(f, -1) * post_ref[...]


def _ffn(h2d, pre_row, post_row, w1, w2, tm):
    t, d = h2d.shape
    d_ff = w1.shape[1]
    return pl.pallas_call(
        functools.partial(_ffn_body, chunk=1024),
        grid=(t // tm,),
        in_specs=[pl.BlockSpec((tm, d), lambda i: (i, 0)),
                  pl.BlockSpec((1, d), lambda i: (0, 0)),
                  pl.BlockSpec((1, d), lambda i: (0, 0)),
                  pl.BlockSpec((d, d_ff), lambda i: (0, 0)),
                  pl.BlockSpec((d_ff, d), lambda i: (0, 0))],
        out_specs=pl.BlockSpec((tm, d), lambda i: (i, 0)),
        out_shape=jax.ShapeDtypeStruct((t, d), F32),
        compiler_params=_params("parallel"),
        name="ffn",
    )(h2d, pre_row, post_row, w1, w2)


def _pad_lanes(vec, offset):
    return jnp.zeros((1, LANES), F32).at[0, offset:offset + vec.shape[0]].set(vec.astype(F32))


def _layer(h2d, mem2d, rel_bias, w_in, conv_w, dn_a_log, dn_dt_bias, dn_norm_g, da_lambda,
           da_subln_g, mem_norm_g, w_mem_kv, w_branch, w_out, ln_mix_pre, ln_mix_post,
           ln_ff_pre, ln_ff_post, w_ff1, w_ff2, batch, seq, mem_len, layer):
    d = h2d.shape[1]
    t = batch * seq
    lam_init = 0.8 - 0.6 * math.exp(-0.3 * layer)
    tm = min(512, seq)
    row = lambda v: v.reshape(1, -1).astype(F32)

    small0 = 7 * d
    rest0 = small0 + 2 * DN_HEADS
    w_q = w_in[:, :d] * (DA_HEAD_DIM ** -0.5 * LOG2E)
    w_qv_t = jnp.concatenate([w_q, w_in[:, 2 * d:3 * d]], axis=1).T.astype(BF16)
    w_kdn = jnp.concatenate([w_in[:, d:2 * d], w_in[:, 3 * d:small0]], axis=1).astype(BF16)
    w_small = jnp.zeros((d, LANES), F32).at[:, :2 * DN_HEADS].set(w_in[:, small0:rest0]).astype(BF16)
    w_rest = w_in[:, rest0:].astype(BF16)
    g_pre = row(ln_mix_pre)

    qvt = _norm_matmul_t(h2d, g_pre, w_qv_t, BF16, tm)
    (kdn,) = _norm_matmul(h2d, g_pre, [w_kdn], [BF16], tm)
    cg, ba = _norm_matmul(h2d, g_pre, [w_rest, w_small], [BF16, F32], tm)
    (mkv,) = _norm_matmul(mem2d, row(mem_norm_g), [w_mem_kv.astype(BF16)], [BF16], min(512, batch * mem_len))

    bias = _bias_table(rel_bias.astype(F32), ATTN_TILE)
    y_a = _diff_attention(da_lambda.astype(F32), qvt, kdn, bias, row(da_subln_g), batch, seq, lam_init)

    y_b = _gdn(kdn, ba, conv_w.astype(F32), _pad_lanes(dn_a_log, DN_HEADS), _pad_lanes(dn_dt_bias, DN_HEADS),
               row(dn_norm_g), batch, seq)

    mkt = mkv[:, :d].reshape(batch, mem_len, CA_HEADS, CA_HEAD_DIM).transpose(0, 2, 3, 1)
    mv = mkv[:, d:].reshape(batch, mem_len, CA_HEADS, CA_HEAD_DIM).transpose(0, 2, 1, 3)
    h1 = _merge(h2d, cg, y_a, y_b, mkt, mv, w_branch.astype(BF16), w_out.astype(BF16),
                row(ln_mix_post), batch, seq, min(512, seq))

    return _ffn(h1, row(ln_ff_pre), row(ln_ff_post), w_ff1.astype(BF16), w_ff2.astype(BF16), min(512, seq))


def kernel(x, mem, rel_bias, w_in, conv_w, dn_a_log, dn_dt_bias, dn_norm_g, da_lambda, da_subln_g,
           mem_norm_g, w_mem_kv, w_branch, w_out, ln_mix_pre, ln_mix_post, ln_ff_pre, ln_ff_post,
           w_ff1, w_ff2):
    batch, seq, d = x.shape
    mem_len = mem.shape[1]
    assert seq % (ATTN_QTILES * ATTN_TILE) == 0 and seq % DN_PAIR == 0
    h2d = x.reshape(batch * seq, d)
    mem2d = mem.reshape(batch * mem_len, d)
    for layer in range(w_in.shape[0]):
        h2d = _layer(h2d, mem2d, rel_bias, w_in[layer], conv_w[layer], dn_a_log[layer], dn_dt_bias[layer],
                     dn_norm_g[layer], da_lambda[layer], da_subln_g[layer], mem_norm_g[layer],
                     w_mem_kv[layer], w_branch[layer], w_out[layer], ln_mix_pre[layer], ln_mix_post[layer],
                     ln_ff_pre[layer], ln_ff_post[layer], w_ff1[layer], w_ff2[layer],
                     batch, seq, mem_len, layer)
    return h2d.reshape(batch, seq, d)
```

```python
import functools
import math

import numpy as np
import jax
import jax.numpy as jnp
from jax import lax
from jax.experimental import pallas as pl
from jax.experimental.pallas import tpu as pltpu

F32 = jnp.float32
BF16 = jnp.bfloat16

EPS = 1e-6
MASK_VALUE = -1e30
LOG2E = math.log2(math.e)
LANES = 128
SUBLANES = 8
VMEM_LIMIT_BYTES = 56 * 1024 * 1024

DA_HEADS = 8
DA_HEAD_DIM = 64
DN_HEADS = 8
DN_DIM = 128
DN_CONV = 4
DN_CHUNK = 64
DN_PAIR = 2 * DN_CHUNK
CA_HEADS = 4
CA_HEAD_DIM = 256
REL_BUCKETS = 32
REL_MAX_EXACT = 16
REL_MAX_DIST = 128

ATTN_TILE = 256
ATTN_QTILES = 2
ATTN_TRIP_TILES = 16
SCORE_LIMIT = 60.0


def _params(*sem):
    return pltpu.CompilerParams(dimension_semantics=sem, vmem_limit_bytes=VMEM_LIMIT_BYTES)


def _mm(a, b):
    return jnp.dot(a.astype(BF16), b.astype(BF16), preferred_element_type=F32)


def _mm_nt(a, b):
    return lax.dot_general(a.astype(BF16), b.astype(BF16), (((1,), (1,)), ((), ())),
                           preferred_element_type=F32)


def _mm_exact_rhs(m01, a):
    hi = a.astype(BF16)
    r1 = a - hi.astype(F32)
    mid = r1.astype(BF16)
    lo = (r1 - mid.astype(F32)).astype(BF16)
    dot = functools.partial(jnp.dot, preferred_element_type=F32)
    return dot(m01, hi) + dot(m01, mid) + dot(m01, lo)


def _rms(x, axis):
    return x * lax.rsqrt(jnp.mean(x * x, axis=axis, keepdims=True) + EPS)


def _sigmoid(x):
    return 1.0 / (1.0 + jnp.exp(-x))


def _silu(x):
    h = 0.5 * x
    return h + h * jnp.tanh(h)


def _normed(x_ref, g_ref):
    return (_rms(x_ref[...], -1) * g_ref[...]).astype(BF16)


def _norm_matmul_body(x_ref, g_ref, *refs, chunk):
    n_out = len(refs) // 2
    n = _normed(x_ref, g_ref)
    for w_ref, o_ref in zip(refs[:n_out], refs[n_out:]):
        cols = w_ref.shape[1]
        step = min(chunk, cols)
        for c0 in range(0, cols, step):
            o_ref[:, c0:c0 + step] = jnp.dot(n, w_ref[:, c0:c0 + step],
                                             preferred_element_type=F32).astype(o_ref.dtype)


def _norm_matmul(x2d, g_row, weights, out_dtypes, tm):
    t, d = x2d.shape
    return pl.pallas_call(
        functools.partial(_norm_matmul_body, chunk=1024),
        grid=(t // tm,),
        in_specs=[pl.BlockSpec((tm, d), lambda i: (i, 0)),
                  pl.BlockSpec((1, d), lambda i: (0, 0))]
                 + [pl.BlockSpec(w.shape, lambda i: (0, 0)) for w in weights],
        out_specs=[pl.BlockSpec((tm, w.shape[1]), lambda i: (i, 0)) for w in weights],
        out_shape=[jax.ShapeDtypeStruct((t, w.shape[1]), dt) for w, dt in zip(weights, out_dtypes)],
        compiler_params=_params("parallel"),
        name="norm_matmul",
    )(x2d, g_row, *weights)


def _norm_matmul_t_body(x_ref, g_ref, wt_ref, o_ref, *, chunk):
    n = _normed(x_ref, g_ref)
    for r0 in range(0, wt_ref.shape[0], chunk):
        o_ref[r0:r0 + chunk, :] = lax.dot_general(
            wt_ref[r0:r0 + chunk, :], n, (((1,), (1,)), ((), ())),
            preferred_element_type=F32).astype(o_ref.dtype)


def _norm_matmul_t(x2d, g_row, wt, out_dtype, tm):
    t, d = x2d.shape
    rows = wt.shape[0]
    return pl.pallas_call(
        functools.partial(_norm_matmul_t_body, chunk=512),
        grid=(t // tm,),
        in_specs=[pl.BlockSpec((tm, d), lambda i: (i, 0)),
                  pl.BlockSpec((1, d), lambda i: (0, 0)),
                  pl.BlockSpec((rows, d), lambda i: (0, 0))],
        out_specs=pl.BlockSpec((rows, tm), lambda i: (0, i)),
        out_shape=jax.ShapeDtypeStruct((rows, t), out_dtype),
        compiler_params=_params("parallel"),
        name="norm_matmul_t",
    )(x2d, g_row, wt)


BIAS_PREV, BIAS_DIAG, BIAS_MASKED, BIAS_NONE = 0, 1, 2, 3


def _bucket_table(tile):
    k = np.arange(tile)[:, None]
    q = np.arange(tile)[None, :]

    def bucket(dist):
        nf = np.maximum(dist, 1).astype(np.float64)
        large = REL_MAX_EXACT + np.trunc(
            np.log(nf / REL_MAX_EXACT) / math.log(REL_MAX_DIST / REL_MAX_EXACT)
            * (REL_BUCKETS - REL_MAX_EXACT)).astype(np.int64)
        large = np.minimum(large, REL_BUCKETS - 1)
        return np.where(dist < REL_MAX_EXACT, dist, large)

    prev = bucket(q - k + tile)
    diag = np.where(q >= k, bucket(np.maximum(q - k, 0)), -1)
    masked = np.full((tile, tile), -1)
    far = np.full((tile, tile), REL_BUCKETS - 1)
    return np.concatenate([prev, diag, masked, far], axis=0).astype(np.int32)


def _bias_table_body(rb_ref, bucket_ref, o_ref):
    h = pl.program_id(0)
    far = rb_ref[REL_BUCKETS - 1, h]
    bk = bucket_ref[...]
    out = jnp.full(bk.shape, MASK_VALUE, F32)
    for b in range(REL_BUCKETS):
        out = jnp.where(bk == b, (rb_ref[b, h] - far) * LOG2E, out)
    o_ref[...] = out


def _bias_table(rel_bias, tile):
    heads = rel_bias.shape[1]
    buckets = jnp.asarray(_bucket_table(tile))
    rows = buckets.shape[0]
    return pl.pallas_call(
        _bias_table_body,
        grid=(heads,),
        in_specs=[pl.BlockSpec(memory_space=pltpu.SMEM),
                  pl.BlockSpec((rows, tile), lambda h: (0, 0))],
        out_specs=pl.BlockSpec((None, rows, tile), lambda h: (h, 0, 0)),
        out_shape=jax.ShapeDtypeStruct((heads, rows, tile), F32),
        compiler_params=_params("arbitrary"),
        name="bias_table",
    )(rel_bias, buckets)


def _attn_body(lam_ref, qt_ref, k_ref, vt_ref, bias_ref, g_ref, o_ref,
               qz_ref, m_ref, x_ref, l_ref, acc_ref, p_ref, *, tile, nsub, lam_init):
    j = pl.program_id(2)
    hd = 2 * DA_HEAD_DIM
    qw = nsub * tile
    t0 = nsub * j

    qt = qt_ref[...]
    row = lax.broadcasted_iota(jnp.int32, qt.shape, 0)
    zero = jnp.zeros_like(qt)
    qz_ref[:, :qw] = jnp.where(row < DA_HEAD_DIM, qt, zero)
    qz_ref[:, qw:] = jnp.where(row >= DA_HEAD_DIM, qt, zero)

    def tile_off(t):
        return pl.multiple_of(jnp.maximum(t, 0) * tile, tile)

    def bias_cols(t, valid=None):
        parts = []
        for a in range(nsub):
            rel = t - (t0 + a)
            blk = jnp.where(rel == 0, BIAS_DIAG,
                            jnp.where(rel == -1, BIAS_PREV, jnp.where(rel < -1, BIAS_NONE, BIAS_MASKED)))
            if valid is not None:
                blk = jnp.where(valid, blk, BIAS_MASKED)
            parts.append(bias_ref[pl.ds(pl.multiple_of(blk * tile, tile), tile), :])
        return jnp.concatenate(parts + parts, axis=1)

    def scores(off, bias):
        s = jnp.dot(k_ref[pl.ds(off, tile), :], qz_ref[...], preferred_element_type=F32)
        return s if bias is None else s + bias

    def values(off, n_keys, p):
        return jnp.dot(vt_ref[:, pl.ds(off, n_keys)], p, preferred_element_type=F32)

    def by_sublane(s):
        return s.reshape(tile // SUBLANES, SUBLANES, 2 * qw)

    def exact_tile(off, bias, first):
        s = scores(off, bias)
        c = jnp.max(s, axis=0, keepdims=True)
        if first:
            m_new = c
        else:
            m_old = m_ref[...]
            m_new = jnp.maximum(m_old, c)
            alpha = jnp.exp2(m_old - m_new)
        p = jnp.exp2(s - m_new)
        rows = jnp.sum(by_sublane(p), axis=0)
        pv = values(off, tile, p.astype(BF16))
        acc_ref[...] = pv if first else alpha * acc_ref[...] + pv
        l_ref[...] = rows if first else alpha * l_ref[...] + rows
        m_ref[...] = m_new

    def plain_group(tiles, contiguous, first):
        cpart = None
        lpart = None
        pv = None
        for i, (off, bias) in enumerate(tiles):
            s = scores(off, bias)
            part = jnp.max(by_sublane(s), axis=0)
            cpart = part if cpart is None else jnp.maximum(cpart, part)
            p = jnp.exp2(s)
            rows = jnp.sum(by_sublane(p), axis=0)
            lpart = rows if lpart is None else lpart + rows
            if contiguous:
                p_ref[i * tile:(i + 1) * tile, :] = p.astype(BF16)
            else:
                d = values(off, tile, p.astype(BF16))
                pv = d if pv is None else pv + d
        if contiguous:
            n_keys = len(tiles) * tile
            pv = values(tiles[0][0], n_keys, p_ref[:n_keys, :])
        acc_ref[...] = pv if first else acc_ref[...] + pv
        l_ref[...] = lpart if first else l_ref[...] + lpart
        x_ref[...] = cpart if first else jnp.maximum(x_ref[...], cpart)

    def finish():
        lv = lam_ref[...]
        lam = (jnp.exp(jnp.sum(lv[0:1] * lv[1:2], axis=-1, keepdims=True))
               - jnp.exp(jnp.sum(lv[2:3] * lv[3:4], axis=-1, keepdims=True)) + lam_init)
        acc = acc_ref[...] * (1.0 / jnp.sum(l_ref[...], axis=0, keepdims=True))
        ot = acc[:, :qw] - lam * acc[:, qw:]
        o = _rms(ot, 0).T * g_ref[...] * (1.0 - lam_init)
        o_ref[...] = o.astype(o_ref.dtype)

    n_far = jnp.maximum(t0 - 1, 0)
    leftover = n_far % 2
    n_even = n_far - leftover
    near = [(t0 + c, None) for c in range(nsub)] + [(t0 - 1, j >= 1), (t0 - 2, leftover == 1)]
    plain_group([(tile_off(t), bias_cols(t, valid)) for t, valid in near], contiguous=False, first=True)

    def far_tiles(start, count):
        plain_group([(tile_off(start + i), None) for i in range(count)], contiguous=True, first=False)

    def trip(u, carry):
        far_tiles(u * ATTN_TRIP_TILES, ATTN_TRIP_TILES)
        return carry

    n_trips = n_even // ATTN_TRIP_TILES
    lax.fori_loop(0, n_trips, trip, 0)
    done = n_trips * ATTN_TRIP_TILES
    count = ATTN_TRIP_TILES // 2
    while count >= 2:
        take = (n_even - done) >= count

        @pl.when(take)
        def _(done=done, count=count):
            far_tiles(done, count)

        done = done + jnp.where(take, count, 0)
        count //= 2
    finish()

    col_max = jnp.max(x_ref[...], axis=0, keepdims=True)

    @pl.when(jnp.max(jnp.abs(col_max)) > SCORE_LIMIT)
    def _():
        exact_tile(tile_off(t0), bias_cols(t0), first=True)

        def body(u, carry):
            t = u + jnp.where(u >= t0, 1, 0)
            exact_tile(tile_off(t), bias_cols(t), first=False)
            return carry

        lax.fori_loop(0, t0 + nsub - 1, body, 0)
        finish()


def _diff_attention(lam_vecs, qvt, kdn, bias, g_row, batch, seq, lam_init):
    tile = ATTN_TILE
    qw = ATTN_QTILES * tile
    assert seq % qw == 0
    nqt = seq // qw
    hd = 2 * DA_HEAD_DIM
    return pl.pallas_call(
        functools.partial(_attn_body, tile=tile, nsub=ATTN_QTILES, lam_init=lam_init),
        grid=(batch, DA_HEADS, nqt),
        in_specs=[pl.BlockSpec((4, DA_HEAD_DIM), lambda b, h, i: (0, 0)),
                  pl.BlockSpec((hd, qw), lambda b, h, i: (h, b * nqt + i)),
                  pl.BlockSpec((seq, hd), lambda b, h, i: (b, h)),
                  pl.BlockSpec((hd, seq), lambda b, h, i: (DA_HEADS + h, b)),
                  pl.BlockSpec((None,) + bias.shape[1:], lambda b, h, i: (h, 0, 0)),
                  pl.BlockSpec((1, hd), lambda b, h, i: (0, 0))],
        out_specs=pl.BlockSpec((qw, hd), lambda b, h, i: (b * nqt + i, h)),
        out_shape=jax.ShapeDtypeStruct((batch * seq, DA_HEADS * hd), BF16),
        scratch_shapes=[pltpu.VMEM((hd, 2 * qw), BF16),
                        pltpu.VMEM((1, 2 * qw), F32),
                        pltpu.VMEM((SUBLANES, 2 * qw), F32),
                        pltpu.VMEM((SUBLANES, 2 * qw), F32),
                        pltpu.VMEM((hd, 2 * qw), F32),
                        pltpu.VMEM((ATTN_TRIP_TILES * tile, 2 * qw), BF16)],
        compiler_params=_params("parallel", "parallel", "arbitrary"),
        name="diff_attn",
    )(lam_vecs, qvt, kdn, qvt, bias, g_row)


def _shift_rows(x, prev, s):
    if s == 0:
        return x
    rolled = pltpu.roll(x, s, 0)
    prolled = pltpu.roll(prev, s, 0)
    r8 = lax.broadcasted_iota(jnp.int32, prev.shape, 0)
    top = jnp.where(r8 < s, prolled, rolled[:SUBLANES])
    return jnp.concatenate([top, rolled[SUBLANES:]], axis=0)


def _dn_proj_body(x_ref, g_ref, w_ref, cw_ref, o_ref, xq_ref, xk_ref, xv_ref, tail_ref, *, tiles_per_seq):
    width = DN_HEADS * DN_DIM
    rows = x_ref.shape[0]

    @pl.when(pl.program_id(0) % tiles_per_seq == 0)
    def _():
        for x_scr in (xq_ref, xk_ref, xv_ref):
            x_scr[:SUBLANES, :] = jnp.zeros((SUBLANES, width), F32)

    @pl.when(pl.program_id(0) % tiles_per_seq != 0)
    def _():
        for which, x_scr in enumerate((xq_ref, xk_ref, xv_ref)):
            x_scr[:SUBLANES, :] = tail_ref[which]

    n = _normed(x_ref, g_ref)

    def project(group):
        return jnp.dot(n, w_ref[:, group * width:(group + 1) * width], preferred_element_type=F32)

    for which, x_scr in enumerate((xq_ref, xk_ref, xv_ref)):
        xp = project(1 + which)
        x_scr[SUBLANES:, :] = xp
        tail_ref[which] = xp[rows - SUBLANES:]
    for group in (0, 4):
        o_ref[:, group * width:(group + 1) * width] = project(group).astype(o_ref.dtype)
    for which, x_scr in enumerate((xq_ref, xk_ref, xv_ref)):
        for h in range(DN_HEADS):
            cols = slice(h * DN_DIM, (h + 1) * DN_DIM)
            w = 0.5 * cw_ref[:, which * width + h * DN_DIM:which * width + (h + 1) * DN_DIM]
            y = w[DN_CONV - 1:DN_CONV] * x_scr[SUBLANES:, cols]
            for s in range(1, DN_CONV):
                y = y + w[DN_CONV - 1 - s:DN_CONV - s] * x_scr[SUBLANES - s:SUBLANES - s + rows, cols]
            y = y + y * jnp.tanh(y)
            if which == 0:
                y = y * lax.rsqrt(jnp.sum(y * y, axis=-1, keepdims=True) + EPS) * (DN_DIM ** -0.5)
            elif which == 1:
                y = y * lax.rsqrt(jnp.sum(y * y, axis=-1, keepdims=True) + EPS)
            c0 = (1 + which) * width + h * DN_DIM
            o_ref[:, c0:c0 + DN_DIM] = y.astype(o_ref.dtype)


def _dn_proj(x2d, g_row, w_kdn, conv_w, seq, tm):
    t, d = x2d.shape
    width = DN_HEADS * DN_DIM
    x_scr = pltpu.VMEM((SUBLANES + tm, width), F32)
    return pl.pallas_call(
        functools.partial(_dn_proj_body, tiles_per_seq=seq // tm),
        grid=(t // tm,),
        in_specs=[pl.BlockSpec((tm, d), lambda i: (i, 0)),
                  pl.BlockSpec((1, d), lambda i: (0, 0)),
                  pl.BlockSpec(w_kdn.shape, lambda i: (0, 0)),
                  pl.BlockSpec(conv_w.shape, lambda i: (0, 0))],
        out_specs=pl.BlockSpec((tm, w_kdn.shape[1]), lambda i: (i, 0)),
        out_shape=jax.ShapeDtypeStruct((t, w_kdn.shape[1]), BF16),
        scratch_shapes=[x_scr, x_scr, x_scr, pltpu.VMEM((3, SUBLANES, width), F32)],
        compiler_params=_params("arbitrary"),
        name="dn_proj",
    )(x2d, g_row, w_kdn, conv_w)


def _gdn_body(q_ref, k_ref, v_ref, z_ref, ba_ref, alog_ref, dtb_ref, gn_ref, o_ref, state_ref):
    p = DN_PAIR
    c = DN_CHUNK

    @pl.when(pl.program_id(1) == 0)
    def _():
        state_ref[...] = jnp.zeros(state_ref.shape, F32)

    ri = lax.broadcasted_iota(jnp.int32, (p, p), 0)
    ci = lax.broadcasted_iota(jnp.int32, (p, p), 1)
    same = (ri // c) == (ci // c)
    causal = same & (ri >= ci)
    strict = same & (ri > ci)
    cum01 = jnp.where(causal, 1.0, 0.0).astype(BF16)

    ba = ba_ref[...]
    beta_all = _sigmoid(ba)
    xs = ba + dtb_ref[...]
    softplus = jnp.maximum(xs, 0.0) + jnp.log(1.0 + jnp.exp(-jnp.abs(xs)))
    g_all = -jnp.exp(alog_ref[...]) * softplus

    gc_all = _mm_exact_rhs(cum01, g_all)
    glast_all = jnp.concatenate(
        [jnp.broadcast_to(gc_all[c - 1:c], (c, LANES)), jnp.broadcast_to(gc_all[p - 1:p], (c, LANES))], axis=0)
    expg_all = jnp.exp(gc_all)
    tail_all = jnp.exp(glast_all - gc_all)
    dl_all = jnp.exp(glast_all)

    def lane(x, j):
        return jnp.broadcast_to(x[:, j:j + 1], (x.shape[0], DN_DIM))

    heads = range(DN_HEADS)
    head_cols = [slice(h * DN_DIM, (h + 1) * DN_DIM) for h in heads]
    q = [q_ref[:, cols].astype(F32) for cols in head_cols]
    k = [k_ref[:, cols].astype(F32) for cols in head_cols]
    v = [v_ref[:, cols].astype(F32) for cols in head_cols]

    gc = [lane(gc_all, DN_HEADS + h) for h in heads]
    decay = [jnp.exp(jnp.where(causal, x - x.T, MASK_VALUE)) for x in gc]
    exp_g = [lane(expg_all, DN_HEADS + h) for h in heads]
    kb = [k[h] * lane(beta_all, h) for h in heads]
    vb = [v[h] * lane(beta_all, h) for h in heads]
    low = [jnp.where(strict, _mm_nt(kb[h], k[h]) * decay[h], 0.0) for h in heads]
    intra = [jnp.where(causal, _mm_nt(q[h], k[h]) * decay[h], 0.0) for h in heads]
    y = [-x for x in low]
    pw = low
    for _ in range(5):
        pw = [_mm(x, x) for x in pw]
        y = [y[h] + pw[h] + _mm(y[h], pw[h]) for h in heads]
    rhs = [jnp.concatenate([vb[h], kb[h] * exp_g[h]], axis=1) for h in heads]
    sol = [rhs[h] + _mm(y[h], rhs[h]) for h in heads]
    q_dec = [q[h] * exp_g[h] for h in heads]
    kt_t = [(k[h] * lane(tail_all, DN_HEADS + h)).T for h in heads]

    outs = [[], []]
    zeros = jnp.zeros((c, DN_DIM), F32)
    for ch in range(2):
        rows = slice(ch * c, (ch + 1) * c)
        st = [state_ref[h] for h in heads]
        v_new = [sol[h][rows, :DN_DIM] - _mm(sol[h][rows, DN_DIM:], st[h]) for h in heads]
        v_pad = [jnp.concatenate([x, zeros] if ch == 0 else [zeros, x], axis=0) for x in v_new]
        outs[ch] = [_mm(q_dec[h][rows], st[h]) + _mm(intra[h][rows], v_pad[h]) for h in heads]
        for h in heads:
            dl = jnp.broadcast_to(dl_all[ch * c:ch * c + 1, DN_HEADS + h:DN_HEADS + h + 1], (1, DN_DIM))
            state_ref[h] = st[h] * dl + _mm(kt_t[h], v_pad[h])
    for h in heads:
        cols = slice(h * DN_DIM, (h + 1) * DN_DIM)
        o = jnp.concatenate([outs[0][h], outs[1][h]], axis=0)
        yb = _rms(o, -1) * gn_ref[...] * _silu(z_ref[:, cols].astype(F32))
        o_ref[:, cols] = yb.astype(o_ref.dtype)


def _gdn(kdn, ba, alog_row, dtb_row, gn_row, batch, seq):
    t = batch * seq
    p = DN_PAIR
    nt = seq // p
    width = DN_HEADS * DN_DIM
    tok = lambda col: pl.BlockSpec((p, width), lambda b, i: (b * nt + i, col))
    row = pl.BlockSpec((1, LANES), lambda b, i: (0, 0))
    return pl.pallas_call(
        _gdn_body,
        grid=(batch, nt),
        in_specs=[tok(1), tok(2), tok(3), tok(4),
                  pl.BlockSpec((p, LANES), lambda b, i: (b * nt + i, 0)),
                  row, row, row],
        out_specs=pl.BlockSpec((p, width), lambda b, i: (b * nt + i, 0)),
        out_shape=jax.ShapeDtypeStruct((t, width), BF16),
        scratch_shapes=[pltpu.VMEM((DN_HEADS, DN_DIM, DN_DIM), F32)],
        compiler_params=_params("parallel", "arbitrary"),
        name="gdn",
    )(kdn, kdn, kdn, kdn, ba, alog_row, dtb_row, gn_row)


def _merge_body(x_ref, cq_ref, g0_ref, g1_ref, g2_ref, ya_ref, yb_ref, mkt_ref, mv_ref,
                wb_ref, wo_ref, ln_ref, o_ref):
    cq = cq_ref[...]
    heads = []
    for h in range(CA_HEADS):
        qh = cq[:, h * CA_HEAD_DIM:(h + 1) * CA_HEAD_DIM]
        s = jnp.dot(qh, mkt_ref[h], preferred_element_type=F32) * (CA_HEAD_DIM ** -0.5)
        e = jnp.exp(s - jnp.max(s, axis=-1, keepdims=True))
        pr = e / jnp.sum(e, axis=-1, keepdims=True)
        heads.append(jnp.dot(pr.astype(BF16), mv_ref[h], preferred_element_type=F32))
    yc = jnp.concatenate(heads, axis=1)

    mixed = _sigmoid(g0_ref[...].astype(F32)) * jnp.dot(ya_ref[...], wb_ref[0], preferred_element_type=F32)
    mixed = mixed + _sigmoid(g1_ref[...].astype(F32)) * jnp.dot(yb_ref[...], wb_ref[1], preferred_element_type=F32)
    mixed = mixed + _sigmoid(g2_ref[...].astype(F32)) * _mm(yc, wb_ref[2])
    out = _mm(mixed, wo_ref[...])
    o_ref[...] = x_ref[...] + _rms(out, -1) * ln_ref[...]


def _merge(x2d, cg, ya, yb, mkt, mv, wb, wo, ln_row, batch, seq, tm):
    t, d = x2d.shape
    nt = seq // tm
    tok = lambda col: pl.BlockSpec((tm, d), lambda i: (i, col))
    mem = pl.BlockSpec((None, CA_HEADS, CA_HEAD_DIM, CA_HEAD_DIM), lambda i: (i // nt, 0, 0, 0))
    return pl.pallas_call(
        _merge_body,
        grid=(t // tm,),
        in_specs=[tok(0), tok(0), tok(1), tok(2), tok(3), tok(0), tok(0), mem, mem,
                  pl.BlockSpec((3, d, d), lambda i: (0, 0, 0)),
                  pl.BlockSpec((d, d), lambda i: (0, 0)),
                  pl.BlockSpec((1, d), lambda i: (0, 0))],
        out_specs=tok(0),
        out_shape=jax.ShapeDtypeStruct((t, d), F32),
        compiler_params=_params("parallel"),
        name="merge",
    )(x2d, cg, cg, cg, cg, ya, yb, mkt, mv, wb, wo, ln_row)


def _ffn_body(h_ref, pre_ref, post_ref, w1_ref, w2_ref, o_ref, *, chunk):
    hh = h_ref[...]
    n = (_rms(hh, -1) * pre_ref[...]).astype(BF16)
    d_ff = w1_ref.shape[1]
    f = jnp.zeros(hh.shape, F32)
    for c0 in range(0, d_ff, chunk):
        a = jnp.dot(n, w1_ref[:, c0:c0 + chunk], preferred_element_type=F32)
        a = jnp.square(jnp.maximum(a, 0.0))
        f = f + jnp.dot(a.astype(BF16), w2_ref[c0:c0 + chunk, :], preferred_element_type=F32)
    o_ref[...] = hh + _rms(f, -1) * post_ref[...]


def _ffn(h2d, pre_row, post_row, w1, w2, tm):
    t, d = h2d.shape
    d_ff = w1.shape[1]
    return pl.pallas_call(
        functools.partial(_ffn_body, chunk=1024),
        grid=(t // tm,),
        in_specs=[pl.BlockSpec((tm, d), lambda i: (i, 0)),
                  pl.BlockSpec((1, d), lambda i: (0, 0)),
                  pl.BlockSpec((1, d), lambda i: (0, 0)),
                  pl.BlockSpec((d, d_ff), lambda i: (0, 0)),
                  pl.BlockSpec((d_ff, d), lambda i: (0, 0))],
        out_specs=pl.BlockSpec((tm, d), lambda i: (i, 0)),
        out_shape=jax.ShapeDtypeStruct((t, d), F32),
        compiler_params=_params("parallel"),
        name="ffn",
    )(h2d, pre_row, post_row, w1, w2)


def _pad_lanes(vec, offset):
    return jnp.zeros((1, LANES), F32).at[0, offset:offset + vec.shape[0]].set(vec.astype(F32))


def _layer(h2d, mem2d, rel_bias, w_in, conv_w, dn_a_log, dn_dt_bias, dn_norm_g, da_lambda,
           da_subln_g, mem_norm_g, w_mem_kv, w_branch, w_out, ln_mix_pre, ln_mix_post,
           ln_ff_pre, ln_ff_post, w_ff1, w_ff2, batch, seq, mem_len, layer):
    d = h2d.shape[1]
    t = batch * seq
    lam_init = 0.8 - 0.6 * math.exp(-0.3 * layer)
    tm = min(512, seq)
    row = lambda v: v.reshape(1, -1).astype(F32)

    small0 = 7 * d
    rest0 = small0 + 2 * DN_HEADS
    w_q = w_in[:, :d] * (DA_HEAD_DIM ** -0.5 * LOG2E)
    w_qv_t = jnp.concatenate([w_q, w_in[:, 2 * d:3 * d]], axis=1).T.astype(BF16)
    w_kdn = jnp.concatenate([w_in[:, d:2 * d], w_in[:, 3 * d:small0]], axis=1).astype(BF16)
    w_small = jnp.zeros((d, LANES), F32).at[:, :2 * DN_HEADS].set(w_in[:, small0:rest0]).astype(BF16)
    w_rest = w_in[:, rest0:].astype(BF16)
    g_pre = row(ln_mix_pre)

    qvt = _norm_matmul_t(h2d, g_pre, w_qv_t, BF16, tm)
    kdn = _dn_proj(h2d, g_pre, w_kdn, conv_w.astype(F32), seq, min(256, seq))
    cg, ba = _norm_matmul(h2d, g_pre, [w_rest, w_small], [BF16, F32], tm)
    (mkv,) = _norm_matmul(mem2d, row(mem_norm_g), [w_mem_kv.astype(BF16)], [BF16], min(512, batch * mem_len))

    bias = _bias_table(rel_bias.astype(F32), ATTN_TILE)
    y_a = _diff_attention(da_lambda.astype(F32), qvt, kdn, bias, row(da_subln_g), batch, seq, lam_init)

    y_b = _gdn(kdn, ba, _pad_lanes(dn_a_log, DN_HEADS), _pad_lanes(dn_dt_bias, DN_HEADS),
               row(dn_norm_g), batch, seq)

    mkt = mkv[:, :d].reshape(batch, mem_len, CA_HEADS, CA_HEAD_DIM).transpose(0, 2, 3, 1)
    mv = mkv[:, d:].reshape(batch, mem_len, CA_HEADS, CA_HEAD_DIM).transpose(0, 2, 1, 3)
    h1 = _merge(h2d, cg, y_a, y_b, mkt, mv, w_branch.astype(BF16), w_out.astype(BF16),
                row(ln_mix_post), batch, seq, min(512, seq))

    return _ffn(h1, row(ln_ff_pre), row(ln_ff_post), w_ff1.astype(BF16), w_ff2.astype(BF16), min(512, seq))


def kernel(x, mem, rel_bias, w_in, conv_w, dn_a_log, dn_dt_bias, dn_norm_g, da_lambda, da_subln_g,
           mem_norm_g, w_mem_kv, w_branch, w_out, ln_mix_pre, ln_mix_post, ln_ff_pre, ln_ff_post,
           w_ff1, w_ff2):
    batch, seq, d = x.shape
    mem_len = mem.shape[1]
    assert seq % (ATTN_QTILES * ATTN_TILE) == 0 and seq % DN_PAIR == 0
    h2d = x.reshape(batch * seq, d)
    mem2d = mem.reshape(batch * mem_len, d)
    for layer in range(w_in.shape[0]):
        h2d = _layer(h2d, mem2d, rel_bias, w_in[layer], conv_w[layer], dn_a_log[layer], dn_dt_bias[layer],
                     dn_norm_g[layer], da_lambda[layer], da_subln_g[layer], mem_norm_g[layer],
                     w_mem_kv[layer], w_branch[layer], w_out[layer], ln_mix_pre[layer], ln_mix_post[layer],
                     ln_ff_pre[layer], ln_ff_post[layer], w_ff1[layer], w_ff2[layer],
                     batch, seq, mem_len, layer)
    return h2d.reshape(batch, seq, d)
```

```python
import functools
import math

import numpy as np
import jax
import jax.numpy as jnp
from jax import lax
from jax.experimental import pallas as pl
from jax.experimental.pallas import tpu as pltpu

F32 = jnp.float32
BF16 = jnp.bfloat16

EPS = 1e-6
MASK_VALUE = -1e30
LOG2E = math.log2(math.e)
LANES = 128
SUBLANES = 8
VMEM_LIMIT_BYTES = 56 * 1024 * 1024

DA_HEADS = 8
DA_HEAD_DIM = 64
DN_HEADS = 8
DN_DIM = 128
DN_CONV = 4
DN_CHUNK = 64
DN_PAIR = 2 * DN_CHUNK
CA_HEADS = 4
CA_HEAD_DIM = 256
REL_BUCKETS = 32
REL_MAX_EXACT = 16
REL_MAX_DIST = 128

GDN_STEP_PAIRS = 2
ATTN_TILE = 256
ATTN_QTILES = 4
ATTN_TRIP_TILES = 8
SCORE_LIMIT = 60.0


def _params(*sem):
    return pltpu.CompilerParams(dimension_semantics=sem, vmem_limit_bytes=VMEM_LIMIT_BYTES)


def _mm(a, b):
    return jnp.dot(a.astype(BF16), b.astype(BF16), preferred_element_type=F32)


def _mm_nt(a, b):
    return lax.dot_general(a.astype(BF16), b.astype(BF16), (((1,), (1,)), ((), ())),
                           preferred_element_type=F32)


def _mm_exact_rhs(m01, a):
    hi = a.astype(BF16)
    r1 = a - hi.astype(F32)
    mid = r1.astype(BF16)
    lo = (r1 - mid.astype(F32)).astype(BF16)
    dot = functools.partial(jnp.dot, preferred_element_type=F32)
    return dot(m01, hi) + dot(m01, mid) + dot(m01, lo)


def _rms(x, axis):
    return x * lax.rsqrt(jnp.mean(x * x, axis=axis, keepdims=True) + EPS)


def _sigmoid(x):
    return 1.0 / (1.0 + jnp.exp(-x))


def _silu(x):
    h = 0.5 * x
    return h + h * jnp.tanh(h)


def _normed(x_ref, g_ref):
    return (_rms(x_ref[...], -1) * g_ref[...]).astype(BF16)


def _norm_matmul_body(x_ref, g_ref, *refs, chunk):
    n_out = len(refs) // 2
    n = _normed(x_ref, g_ref)
    for w_ref, o_ref in zip(refs[:n_out], refs[n_out:]):
        cols = w_ref.shape[1]
        step = min(chunk, cols)
        for c0 in range(0, cols, step):
            o_ref[:, c0:c0 + step] = jnp.dot(n, w_ref[:, c0:c0 + step],
                                             preferred_element_type=F32).astype(o_ref.dtype)


def _norm_matmul(x2d, g_row, weights, out_dtypes, tm):
    t, d = x2d.shape
    return pl.pallas_call(
        functools.partial(_norm_matmul_body, chunk=1024),
        grid=(t // tm,),
        in_specs=[pl.BlockSpec((tm, d), lambda i: (i, 0)),
                  pl.BlockSpec((1, d), lambda i: (0, 0))]
                 + [pl.BlockSpec(w.shape, lambda i: (0, 0)) for w in weights],
        out_specs=[pl.BlockSpec((tm, w.shape[1]), lambda i: (i, 0)) for w in weights],
        out_shape=[jax.ShapeDtypeStruct((t, w.shape[1]), dt) for w, dt in zip(weights, out_dtypes)],
        compiler_params=_params("parallel"),
        name="norm_matmul",
    )(x2d, g_row, *weights)


def _norm_matmul_t_body(x_ref, g_ref, wt_ref, o_ref, *, chunk):
    n = _normed(x_ref, g_ref)
    for r0 in range(0, wt_ref.shape[0], chunk):
        o_ref[r0:r0 + chunk, :] = lax.dot_general(
            wt_ref[r0:r0 + chunk, :], n, (((1,), (1,)), ((), ())),
            preferred_element_type=F32).astype(o_ref.dtype)


def _norm_matmul_t(x2d, g_row, wt, out_dtype, tm):
    t, d = x2d.shape
    rows = wt.shape[0]
    return pl.pallas_call(
        functools.partial(_norm_matmul_t_body, chunk=512),
        grid=(t // tm,),
        in_specs=[pl.BlockSpec((tm, d), lambda i: (i, 0)),
                  pl.BlockSpec((1, d), lambda i: (0, 0)),
                  pl.BlockSpec((rows, d), lambda i: (0, 0))],
        out_specs=pl.BlockSpec((rows, tm), lambda i: (0, i)),
        out_shape=jax.ShapeDtypeStruct((rows, t), out_dtype),
        compiler_params=_params("parallel"),
        name="norm_matmul_t",
    )(x2d, g_row, wt)


BIAS_PREV, BIAS_DIAG, BIAS_MASKED, BIAS_NONE = 0, 1, 2, 3


def _bucket_table(tile):
    k = np.arange(tile)[:, None]
    q = np.arange(tile)[None, :]

    def bucket(dist):
        nf = np.maximum(dist, 1).astype(np.float64)
        large = REL_MAX_EXACT + np.trunc(
            np.log(nf / REL_MAX_EXACT) / math.log(REL_MAX_DIST / REL_MAX_EXACT)
            * (REL_BUCKETS - REL_MAX_EXACT)).astype(np.int64)
        large = np.minimum(large, REL_BUCKETS - 1)
        return np.where(dist < REL_MAX_EXACT, dist, large)

    prev = bucket(q - k + tile)
    diag = np.where(q >= k, bucket(np.maximum(q - k, 0)), -1)
    masked = np.full((tile, tile), -1)
    far = np.full((tile, tile), REL_BUCKETS - 1)
    return np.concatenate([prev, diag, masked, far], axis=0).astype(np.int32)


def _bias_table_body(rb_ref, bucket_ref, o_ref):
    h = pl.program_id(0)
    far = rb_ref[REL_BUCKETS - 1, h]
    bk = bucket_ref[...]
    out = jnp.full(bk.shape, MASK_VALUE, F32)
    for b in range(REL_BUCKETS):
        out = jnp.where(bk == b, (rb_ref[b, h] - far) * LOG2E, out)
    o_ref[...] = out


def _bias_table(rel_bias, tile):
    heads = rel_bias.shape[1]
    buckets = jnp.asarray(_bucket_table(tile))
    rows = buckets.shape[0]
    return pl.pallas_call(
        _bias_table_body,
        grid=(heads,),
        in_specs=[pl.BlockSpec(memory_space=pltpu.SMEM),
                  pl.BlockSpec((rows, tile), lambda h: (0, 0))],
        out_specs=pl.BlockSpec((None, rows, tile), lambda h: (h, 0, 0)),
        out_shape=jax.ShapeDtypeStruct((heads, rows, tile), F32),
        compiler_params=_params("arbitrary"),
        name="bias_table",
    )(rel_bias, buckets)


def _attn_body(lam_ref, qt_ref, k_ref, vt_ref, bias_ref, g_ref, o_ref,
               qz_ref, m_ref, x_ref, l_ref, acc_ref, p_ref, *, tile, nsub, lam_init):
    j = pl.program_id(2)
    hd = 2 * DA_HEAD_DIM
    qw = nsub * tile
    t0 = nsub * j

    sub = 2 * tile
    qt = qt_ref[...]
    row = lax.broadcasted_iota(jnp.int32, (hd, tile), 0)
    zero = jnp.zeros((hd, tile), qt.dtype)
    for a in range(nsub):
        qa = qt[:, a * tile:(a + 1) * tile]
        qz_ref[:, a * sub:a * sub + tile] = jnp.where(row < DA_HEAD_DIM, qa, zero)
        qz_ref[:, a * sub + tile:(a + 1) * sub] = jnp.where(row >= DA_HEAD_DIM, qa, zero)

    def tile_off(t):
        return pl.multiple_of(jnp.maximum(t, 0) * tile, tile)

    def bias_block(block):
        b = bias_ref[pl.ds(pl.multiple_of(block * tile, tile), tile), :]
        return jnp.concatenate([b, b], axis=1)

    def bias_cols(t):
        parts = []
        for a in range(nsub):
            rel = t - (t0 + a)
            parts.append(bias_block(jnp.where(
                rel == 0, BIAS_DIAG, jnp.where(rel == -1, BIAS_PREV, jnp.where(rel < -1, BIAS_NONE, BIAS_MASKED)))))
        return jnp.concatenate(parts, axis=1)

    def scores(off, first_sub=0, kinds=None):
        s = jnp.dot(k_ref[pl.ds(off, tile), :], qz_ref[:, first_sub * sub:], preferred_element_type=F32)
        if kinds is None:
            return s
        pieces = [s[:, i * sub:(i + 1) * sub] for i in range(len(kinds))]
        return jnp.concatenate([x if kind is None else x + bias_block(kind) for x, kind in zip(pieces, kinds)], axis=1)

    def values(off, n_keys, p):
        return jnp.dot(vt_ref[:, pl.ds(off, n_keys)], p, preferred_element_type=F32)

    def by_sublane(s):
        return s.reshape(tile // SUBLANES, SUBLANES, s.shape[1])

    def exact_tile(off, bias, first):
        s = scores(off) + bias
        c = jnp.max(s, axis=0, keepdims=True)
        if first:
            m_new = c
        else:
            m_old = m_ref[...]
            m_new = jnp.maximum(m_old, c)
            alpha = jnp.exp2(m_old - m_new)
        p = jnp.exp2(s - m_new)
        rows = jnp.sum(by_sublane(p), axis=0)
        pv = values(off, tile, p.astype(BF16))
        acc_ref[...] = pv if first else alpha * acc_ref[...] + pv
        l_ref[...] = rows if first else alpha * l_ref[...] + rows
        m_ref[...] = m_new

    def diagonal_block():
        acc = lsum = xmax = None
        for c in reversed(range(nsub)):
            kinds = [BIAS_DIAG if a == c else BIAS_PREV if a == c + 1 else None for a in range(c, nsub)]
            s = scores(tile_off(t0 + c), c, kinds)
            p = jnp.exp2(s)
            part = jnp.max(by_sublane(s), axis=0)
            rows = jnp.sum(by_sublane(p), axis=0)
            pv = values(tile_off(t0 + c), tile, p.astype(BF16))
            if acc is None:
                acc, lsum, xmax = pv, rows, part
            else:
                acc = jnp.concatenate([pv[:, :sub], pv[:, sub:] + acc], axis=1)
                lsum = jnp.concatenate([rows[:, :sub], rows[:, sub:] + lsum], axis=1)
                xmax = jnp.concatenate([part[:, :sub], jnp.maximum(part[:, sub:], xmax)], axis=1)
        acc_ref[...], l_ref[...], x_ref[...] = acc, lsum, xmax

    def far_tiles(start, count, last_kinds=None):
        cpart = None
        lpart = None
        for i in range(count):
            s = scores(tile_off(start + i), 0, last_kinds if i == count - 1 else None)
            part = jnp.max(by_sublane(s), axis=0)
            cpart = part if cpart is None else jnp.maximum(cpart, part)
            p = jnp.exp2(s)
            rows = jnp.sum(by_sublane(p), axis=0)
            lpart = rows if lpart is None else lpart + rows
            p_ref[i * tile:(i + 1) * tile, :] = p.astype(BF16)
        n_keys = count * tile
        acc_ref[...] = acc_ref[...] + values(tile_off(start), n_keys, p_ref[:n_keys, :])
        l_ref[...] = l_ref[...] + lpart
        x_ref[...] = jnp.maximum(x_ref[...], cpart)

    def finish():
        lv = lam_ref[...]
        lam = (jnp.exp(jnp.sum(lv[0:1] * lv[1:2], axis=-1, keepdims=True))
               - jnp.exp(jnp.sum(lv[2:3] * lv[3:4], axis=-1, keepdims=True)) + lam_init)
        acc = acc_ref[...] * (1.0 / jnp.sum(l_ref[...], axis=0, keepdims=True))
        ot = jnp.concatenate([acc[:, a * sub:a * sub + tile] - lam * acc[:, a * sub + tile:(a + 1) * sub]
                              for a in range(nsub)], axis=1)
        o = _rms(ot, 0).T * g_ref[...] * (1.0 - lam_init)
        o_ref[...] = o.astype(o_ref.dtype)

    assert nsub % 2 == 0
    n_even = jnp.maximum(t0 - 2, 0)
    diagonal_block()

    @pl.when(j >= 1)
    def _():
        far_tiles(t0 - 2, 2, [BIAS_PREV] + [None] * (nsub - 1))

    def trip(u, carry):
        far_tiles(u * ATTN_TRIP_TILES, ATTN_TRIP_TILES)
        return carry

    n_trips = n_even // ATTN_TRIP_TILES
    lax.fori_loop(0, n_trips, trip, 0)
    done = n_trips * ATTN_TRIP_TILES
    count = ATTN_TRIP_TILES // 2
    while count >= 2:
        take = (n_even - done) >= count

        @pl.when(take)
        def _(done=done, count=count):
            far_tiles(done, count)

        done = done + jnp.where(take, count, 0)
        count //= 2
    finish()

    col_max = jnp.max(x_ref[...], axis=0, keepdims=True)

    @pl.when(jnp.max(jnp.abs(col_max)) > SCORE_LIMIT)
    def _():
        exact_tile(tile_off(t0), bias_cols(t0), first=True)

        def body(u, carry):
            t = u + jnp.where(u >= t0, 1, 0)
            exact_tile(tile_off(t), bias_cols(t), first=False)
            return carry

        lax.fori_loop(0, t0 + nsub - 1, body, 0)
        finish()


def _diff_attention(lam_vecs, qvt, kdn, bias, g_row, batch, seq, lam_init):
    tile = ATTN_TILE
    qw = ATTN_QTILES * tile
    assert seq % qw == 0
    nqt = seq // qw
    hd = 2 * DA_HEAD_DIM
    return pl.pallas_call(
        functools.partial(_attn_body, tile=tile, nsub=ATTN_QTILES, lam_init=lam_init),
        grid=(batch, DA_HEADS, nqt),
        in_specs=[pl.BlockSpec((4, DA_HEAD_DIM), lambda b, h, i: (0, 0)),
                  pl.BlockSpec((hd, qw), lambda b, h, i: (h, b * nqt + i)),
                  pl.BlockSpec((seq, hd), lambda b, h, i: (b, h)),
                  pl.BlockSpec((hd, seq), lambda b, h, i: (DA_HEADS + h, b)),
                  pl.BlockSpec((None,) + bias.shape[1:], lambda b, h, i: (h, 0, 0)),
                  pl.BlockSpec((1, hd), lambda b, h, i: (0, 0))],
        out_specs=pl.BlockSpec((qw, hd), lambda b, h, i: (b * nqt + i, h)),
        out_shape=jax.ShapeDtypeStruct((batch * seq, DA_HEADS * hd), BF16),
        scratch_shapes=[pltpu.VMEM((hd, 2 * qw), BF16),
                        pltpu.VMEM((1, 2 * qw), F32),
                        pltpu.VMEM((SUBLANES, 2 * qw), F32),
                        pltpu.VMEM((SUBLANES, 2 * qw), F32),
                        pltpu.VMEM((hd, 2 * qw), F32),
                        pltpu.VMEM((ATTN_TRIP_TILES * tile, 2 * qw), BF16)],
        compiler_params=_params("parallel", "parallel", "arbitrary"),
        name="diff_attn",
    )(lam_vecs, qvt, kdn, qvt, bias, g_row)


def _shift_rows(x, prev, s):
    if s == 0:
        return x
    rolled = pltpu.roll(x, s, 0)
    prolled = pltpu.roll(prev, s, 0)
    r8 = lax.broadcasted_iota(jnp.int32, prev.shape, 0)
    top = jnp.where(r8 < s, prolled, rolled[:SUBLANES])
    return jnp.concatenate([top, rolled[SUBLANES:]], axis=0)


def _dn_proj_body(x_ref, g_ref, w_ref, cw_ref, o_ref, xq_ref, xk_ref, xv_ref, tail_ref, *, tiles_per_seq):
    width = DN_HEADS * DN_DIM
    rows = x_ref.shape[0]

    @pl.when(pl.program_id(0) % tiles_per_seq == 0)
    def _():
        for x_scr in (xq_ref, xk_ref, xv_ref):
            x_scr[:SUBLANES, :] = jnp.zeros((SUBLANES, width), F32)

    @pl.when(pl.program_id(0) % tiles_per_seq != 0)
    def _():
        for which, x_scr in enumerate((xq_ref, xk_ref, xv_ref)):
            x_scr[:SUBLANES, :] = tail_ref[which]

    n = _normed(x_ref, g_ref)

    def project(group):
        return jnp.dot(n, w_ref[:, group * width:(group + 1) * width], preferred_element_type=F32)

    for which, x_scr in enumerate((xq_ref, xk_ref, xv_ref)):
        xp = project(1 + which)
        x_scr[SUBLANES:, :] = xp
        tail_ref[which] = xp[rows - SUBLANES:]
    for group in (0, 4):
        o_ref[:, group * width:(group + 1) * width] = project(group).astype(o_ref.dtype)
    for which, x_scr in enumerate((xq_ref, xk_ref, xv_ref)):
        for h in range(DN_HEADS):
            cols = slice(h * DN_DIM, (h + 1) * DN_DIM)
            w = 0.5 * cw_ref[:, which * width + h * DN_DIM:which * width + (h + 1) * DN_DIM]
            y = w[DN_CONV - 1:DN_CONV] * x_scr[SUBLANES:, cols]
            for s in range(1, DN_CONV):
                y = y + w[DN_CONV - 1 - s:DN_CONV - s] * x_scr[SUBLANES - s:SUBLANES - s + rows, cols]
            y = y + y * jnp.tanh(y)
            if which == 0:
                y = y * lax.rsqrt(jnp.sum(y * y, axis=-1, keepdims=True) + EPS) * (DN_DIM ** -0.5)
            elif which == 1:
                y = y * lax.rsqrt(jnp.sum(y * y, axis=-1, keepdims=True) + EPS)
            c0 = (1 + which) * width + h * DN_DIM
            o_ref[:, c0:c0 + DN_DIM] = y.astype(o_ref.dtype)


def _dn_proj(x2d, g_row, w_kdn, conv_w, seq, tm):
    t, d = x2d.shape
    width = DN_HEADS * DN_DIM
    x_scr = pltpu.VMEM((SUBLANES + tm, width), F32)
    return pl.pallas_call(
        functools.partial(_dn_proj_body, tiles_per_seq=seq // tm),
        grid=(t // tm,),
        in_specs=[pl.BlockSpec((tm, d), lambda i: (i, 0)),
                  pl.BlockSpec((1, d), lambda i: (0, 0)),
                  pl.BlockSpec(w_kdn.shape, lambda i: (0, 0)),
                  pl.BlockSpec(conv_w.shape, lambda i: (0, 0))],
        out_specs=pl.BlockSpec((tm, w_kdn.shape[1]), lambda i: (i, 0)),
        out_shape=jax.ShapeDtypeStruct((t, w_kdn.shape[1]), BF16),
        scratch_shapes=[x_scr, x_scr, x_scr, pltpu.VMEM((3, SUBLANES, width), F32)],
        compiler_params=_params("arbitrary"),
        name="dn_proj",
    )(x2d, g_row, w_kdn, conv_w)


def _gdn_body(q_ref, k_ref, v_ref, z_ref, ba_ref, alog_ref, dtb_ref, gn_ref, o_ref, state_ref, *, npairs):
    p = DN_PAIR
    c = DN_CHUNK
    heads = range(DN_HEADS)
    head_cols = [slice(h * DN_DIM, (h + 1) * DN_DIM) for h in heads]

    @pl.when(pl.program_id(1) == 0)
    def _():
        state_ref[...] = jnp.zeros(state_ref.shape, F32)

    ri = lax.broadcasted_iota(jnp.int32, (p, p), 0)
    ci = lax.broadcasted_iota(jnp.int32, (p, p), 1)
    same = (ri // c) == (ci // c)
    causal = same & (ri >= ci)
    strict = same & (ri > ci)
    cum01 = jnp.where(causal, 1.0, 0.0).astype(BF16)

    def lane(x, j):
        return jnp.broadcast_to(x[:, j:j + 1], (x.shape[0], DN_DIM))

    def prepare(pr):
        tok = slice(pr * p, (pr + 1) * p)
        ba = ba_ref[tok, :]
        beta_all = _sigmoid(ba)
        xs = ba + dtb_ref[...]
        softplus = jnp.maximum(xs, 0.0) + jnp.log(1.0 + jnp.exp(-jnp.abs(xs)))
        g_all = -jnp.exp(alog_ref[...]) * softplus
        gc_all = _mm_exact_rhs(cum01, g_all)
        glast_all = jnp.concatenate(
            [jnp.broadcast_to(gc_all[c - 1:c], (c, LANES)), jnp.broadcast_to(gc_all[p - 1:p], (c, LANES))], axis=0)
        expg_all = jnp.exp(gc_all)
        tail_all = jnp.exp(glast_all - gc_all)
        dl_all = jnp.exp(glast_all)

        q = [q_ref[tok, cols].astype(F32) for cols in head_cols]
        k = [k_ref[tok, cols].astype(F32) for cols in head_cols]
        v = [v_ref[tok, cols].astype(F32) for cols in head_cols]
        gc = [lane(gc_all, DN_HEADS + h) for h in heads]
        decay = [jnp.exp(jnp.where(causal, x - x.T, MASK_VALUE)) for x in gc]
        exp_g = [lane(expg_all, DN_HEADS + h) for h in heads]
        kb = [k[h] * lane(beta_all, h) for h in heads]
        vb = [v[h] * lane(beta_all, h) for h in heads]
        low = [jnp.where(strict, _mm_nt(kb[h], k[h]) * decay[h], 0.0) for h in heads]
        intra = [jnp.where(causal, _mm_nt(q[h], k[h]) * decay[h], 0.0) for h in heads]
        y = [-x for x in low]
        pw = low
        for _ in range(5):
            pw = [_mm(x, x) for x in pw]
            y = [y[h] + pw[h] + _mm(y[h], pw[h]) for h in heads]
        rhs = [jnp.concatenate([vb[h], kb[h] * exp_g[h]], axis=1) for h in heads]
        sol = [rhs[h] + _mm(y[h], rhs[h]) for h in heads]
        q_dec = [q[h] * exp_g[h] for h in heads]
        kt_t = [(k[h] * lane(tail_all, DN_HEADS + h)).T for h in heads]
        return sol, intra, q_dec, kt_t, dl_all

    prepared = [prepare(pr) for pr in range(npairs)]

    zeros = jnp.zeros((c, DN_DIM), F32)
    for pr in range(npairs):
        sol, intra, q_dec, kt_t, dl_all = prepared[pr]
        tok = slice(pr * p, (pr + 1) * p)
        outs = [[], []]
        for ch in range(2):
            rows = slice(ch * c, (ch + 1) * c)
            st = [state_ref[h] for h in heads]
            v_new = [sol[h][rows, :DN_DIM] - _mm(sol[h][rows, DN_DIM:], st[h]) for h in heads]
            v_pad = [jnp.concatenate([x, zeros] if ch == 0 else [zeros, x], axis=0) for x in v_new]
            outs[ch] = [_mm(q_dec[h][rows], st[h]) + _mm(intra[h][rows], v_pad[h]) for h in heads]
            for h in heads:
                dl = jnp.broadcast_to(dl_all[ch * c:ch * c + 1, DN_HEADS + h:DN_HEADS + h + 1], (1, DN_DIM))
                state_ref[h] = st[h] * dl + _mm(kt_t[h], v_pad[h])
        for h in heads:
            o = jnp.concatenate([outs[0][h], outs[1][h]], axis=0)
            yb = _rms(o, -1) * gn_ref[...] * _silu(z_ref[tok, head_cols[h]].astype(F32))
            o_ref[tok, head_cols[h]] = yb.astype(o_ref.dtype)


def _gdn(kdn, ba, alog_row, dtb_row, gn_row, batch, seq):
    t = batch * seq
    p = GDN_STEP_PAIRS * DN_PAIR
    nt = seq // p
    width = DN_HEADS * DN_DIM
    tok = lambda col: pl.BlockSpec((p, width), lambda b, i: (b * nt + i, col))
    row = pl.BlockSpec((1, LANES), lambda b, i: (0, 0))
    return pl.pallas_call(
        functools.partial(_gdn_body, npairs=GDN_STEP_PAIRS),
        grid=(batch, nt),
        in_specs=[tok(1), tok(2), tok(3), tok(4),
                  pl.BlockSpec((p, LANES), lambda b, i: (b * nt + i, 0)),
                  row, row, row],
        out_specs=pl.BlockSpec((p, width), lambda b, i: (b * nt + i, 0)),
        out_shape=jax.ShapeDtypeStruct((t, width), BF16),
        scratch_shapes=[pltpu.VMEM((DN_HEADS, DN_DIM, DN_DIM), F32)],
        compiler_params=_params("parallel", "arbitrary"),
        name="gdn",
    )(kdn, kdn, kdn, kdn, ba, alog_row, dtb_row, gn_row)


def _merge_body(x_ref, cq_ref, g0_ref, g1_ref, g2_ref, ya_ref, yb_ref, mkt_ref, mv_ref,
                wb_ref, wo_ref, ln_ref, o_ref):
    cq = cq_ref[...]
    heads = []
    for h in range(CA_HEADS):
        qh = cq[:, h * CA_HEAD_DIM:(h + 1) * CA_HEAD_DIM]
        s = jnp.dot(qh, mkt_ref[h], preferred_element_type=F32) * (CA_HEAD_DIM ** -0.5)
        e = jnp.exp(s - jnp.max(s, axis=-1, keepdims=True))
        pr = e / jnp.sum(e, axis=-1, keepdims=True)
        heads.append(jnp.dot(pr.astype(BF16), mv_ref[h], preferred_element_type=F32))
    yc = jnp.concatenate(heads, axis=1)

    mixed = _sigmoid(g0_ref[...].astype(F32)) * jnp.dot(ya_ref[...], wb_ref[0], preferred_element_type=F32)
    mixed = mixed + _sigmoid(g1_ref[...].astype(F32)) * jnp.dot(yb_ref[...], wb_ref[1], preferred_element_type=F32)
    mixed = mixed + _sigmoid(g2_ref[...].astype(F32)) * _mm(yc, wb_ref[2])
    out = _mm(mixed, wo_ref[...])
    o_ref[...] = x_ref[...] + _rms(out, -1) * ln_ref[...]


def _merge(x2d, cg, ya, yb, mkt, mv, wb, wo, ln_row, batch, seq, tm):
    t, d = x2d.shape
    nt = seq // tm
    tok = lambda col: pl.BlockSpec((tm, d), lambda i: (i, col))
    mem = pl.BlockSpec((None, CA_HEADS, CA_HEAD_DIM, CA_HEAD_DIM), lambda i: (i // nt, 0, 0, 0))
    return pl.pallas_call(
        _merge_body,
        grid=(t // tm,),
        in_specs=[tok(0), tok(0), tok(1), tok(2), tok(3), tok(0), tok(0), mem, mem,
                  pl.BlockSpec((3, d, d), lambda i: (0, 0, 0)),
                  pl.BlockSpec((d, d), lambda i: (0, 0)),
                  pl.BlockSpec((1, d), lambda i: (0, 0))],
        out_specs=tok(0),
        out_shape=jax.ShapeDtypeStruct((t, d), F32),
        compiler_params=_params("parallel"),
        name="merge",
    )(x2d, cg, cg, cg, cg, ya, yb, mkt, mv, wb, wo, ln_row)


def _ffn_body(h_ref, pre_ref, post_ref, w1_ref, w2_ref, o_ref, *, chunk):
    hh = h_ref[...]
    n = (_rms(hh, -1) * pre_ref[...]).astype(BF16)
    d_ff = w1_ref.shape[1]
    f = jnp.zeros(hh.shape, F32)
    for c0 in range(0, d_ff, chunk):
        a = jnp.dot(n, w1_ref[:, c0:c0 + chunk], preferred_element_type=F32)
        a = jnp.square(jnp.maximum(a, 0.0))
        f = f + jnp.dot(a.astype(BF16), w2_ref[c0:c0 + chunk, :], preferred_element_type=F32)
    o_ref[...] = hh + _rms(f, -1) * post_ref[...]


def _ffn(h2d, pre_row, post_row, w1, w2, tm):
    t, d = h2d.shape
    d_ff = w1.shape[1]
    return pl.pallas_call(
        functools.partial(_ffn_body, chunk=1024),
        grid=(t // tm,),
        in_specs=[pl.BlockSpec((tm, d), lambda i: (i, 0)),
                  pl.BlockSpec((1, d), lambda i: (0, 0)),
                  pl.BlockSpec((1, d), lambda i: (0, 0)),
                  pl.BlockSpec((d, d_ff), lambda i: (0, 0)),
                  pl.BlockSpec((d_ff, d), lambda i: (0, 0))],
        out_specs=pl.BlockSpec((tm, d), lambda i: (i, 0)),
        out_shape=jax.ShapeDtypeStruct((t, d), F32),
        compiler_params=_params("parallel"),
        name="ffn",
    )(h2d, pre_row, post_row, w1, w2)


def _pad_lanes(vec, offset):
    return jnp.zeros((1, LANES), F32).at[0, offset:offset + vec.shape[0]].set(vec.astype(F32))


def _layer(h2d, mem2d, rel_bias, w_in, conv_w, dn_a_log, dn_dt_bias, dn_norm_g, da_lambda,
           da_subln_g, mem_norm_g, w_mem_kv, w_branch, w_out, ln_mix_pre, ln_mix_post,
           ln_ff_pre, ln_ff_post, w_ff1, w_ff2, batch, seq, mem_len, layer):
    d = h2d.shape[1]
    t = batch * seq
    lam_init = 0.8 - 0.6 * math.exp(-0.3 * layer)
    tm = min(512, seq)
    row = lambda v: v.reshape(1, -1).astype(F32)

    small0 = 7 * d
    rest0 = small0 + 2 * DN_HEADS
    w_q = w_in[:, :d] * (DA_HEAD_DIM ** -0.5 * LOG2E)
    w_qv_t = jnp.concatenate([w_q, w_in[:, 2 * d:3 * d]], axis=1).T.astype(BF16)
    w_kdn = jnp.concatenate([w_in[:, d:2 * d], w_in[:, 3 * d:small0]], axis=1).astype(BF16)
    w_small = jnp.zeros((d, LANES), F32).at[:, :2 * DN_HEADS].set(w_in[:, small0:rest0]).astype(BF16)
    w_rest = w_in[:, rest0:].astype(BF16)
    g_pre = row(ln_mix_pre)

    qvt = _norm_matmul_t(h2d, g_pre, w_qv_t, BF16, tm)
    kdn = _dn_proj(h2d, g_pre, w_kdn, conv_w.astype(F32), seq, min(256, seq))
    cg, ba = _norm_matmul(h2d, g_pre, [w_rest, w_small], [BF16, F32], tm)
    (mkv,) = _norm_matmul(mem2d, row(mem_norm_g), [w_mem_kv.astype(BF16)], [BF16], min(512, batch * mem_len))

    bias = _bias_table(rel_bias.astype(F32), ATTN_TILE)
    y_a = _diff_attention(da_lambda.astype(F32), qvt, kdn, bias, row(da_subln_g), batch, seq, lam_init)

    y_b = _gdn(kdn, ba, _pad_lanes(dn_a_log, DN_HEADS), _pad_lanes(dn_dt_bias, DN_HEADS),
               row(dn_norm_g), batch, seq)

    mkt = mkv[:, :d].reshape(batch, mem_len, CA_HEADS, CA_HEAD_DIM).transpose(0, 2, 3, 1)
    mv = mkv[:, d:].reshape(batch, mem_len, CA_HEADS, CA_HEAD_DIM).transpose(0, 2, 1, 3)
    h1 = _merge(h2d, cg, y_a, y_b, mkt, mv, w_branch.astype(BF16), w_out.astype(BF16),
                row(ln_mix_post), batch, seq, min(512, seq))

    return _ffn(h1, row(ln_ff_pre), row(ln_ff_post), w_ff1.astype(BF16), w_ff2.astype(BF16), min(512, seq))


def kernel(x, mem, rel_bias, w_in, conv_w, dn_a_log, dn_dt_bias, dn_norm_g, da_lambda, da_subln_g,
           mem_norm_g, w_mem_kv, w_branch, w_out, ln_mix_pre, ln_mix_post, ln_ff_pre, ln_ff_post,
           w_ff1, w_ff2):
    batch, seq, d = x.shape
    mem_len = mem.shape[1]
    assert seq % (ATTN_QTILES * ATTN_TILE) == 0 and seq % DN_PAIR == 0
    h2d = x.reshape(batch * seq, d)
    mem2d = mem.reshape(batch * mem_len, d)
    for layer in range(w_in.shape[0]):
        h2d = _layer(h2d, mem2d, rel_bias, w_in[layer], conv_w[layer], dn_a_log[layer], dn_dt_bias[layer],
                     dn_norm_g[layer], da_lambda[layer], da_subln_g[layer], mem_norm_g[layer],
                     w_mem_kv[layer], w_branch[layer], w_out[layer], ln_mix_pre[layer], ln_mix_post[layer],
                     ln_ff_pre[layer], ln_ff_post[layer], w_ff1[layer], w_ff2[layer],
                     batch, seq, mem_len, layer)
    return h2d.reshape(batch, seq, d)
```

```python
import functools
import math

import numpy as np
import jax
import jax.numpy as jnp
from jax import lax
from jax.experimental import pallas as pl
from jax.experimental.pallas import tpu as pltpu

F32 = jnp.float32
BF16 = jnp.bfloat16

EPS = 1e-6
MASK_VALUE = -1e30
LOG2E = math.log2(math.e)
LANES = 128
SUBLANES = 8
VMEM_LIMIT_BYTES = 56 * 1024 * 1024

DA_HEADS = 8
DA_HEAD_DIM = 64
DN_HEADS = 8
DN_DIM = 128
DN_CONV = 4
DN_CHUNK = 64
DN_PAIR = 2 * DN_CHUNK
CA_HEADS = 4
CA_HEAD_DIM = 256
REL_BUCKETS = 32
REL_MAX_EXACT = 16
REL_MAX_DIST = 128

GDN_STEP_PAIRS = 4
ATTN_TILE = 256
ATTN_QTILES = 4
ATTN_TRIP_TILES = 8
SCORE_LIMIT = 60.0


def _params(*sem):
    return pltpu.CompilerParams(dimension_semantics=sem, vmem_limit_bytes=VMEM_LIMIT_BYTES)


def _mm(a, b):
    return jnp.dot(a.astype(BF16), b.astype(BF16), preferred_element_type=F32)


def _mm_nt(a, b):
    return lax.dot_general(a.astype(BF16), b.astype(BF16), (((1,), (1,)), ((), ())),
                           preferred_element_type=F32)


def _mm_exact_rhs(m01, a):
    hi = a.astype(BF16)
    r1 = a - hi.astype(F32)
    mid = r1.astype(BF16)
    lo = (r1 - mid.astype(F32)).astype(BF16)
    dot = functools.partial(jnp.dot, preferred_element_type=F32)
    return dot(m01, hi) + dot(m01, mid) + dot(m01, lo)


def _rms(x, axis):
    return x * lax.rsqrt(jnp.mean(x * x, axis=axis, keepdims=True) + EPS)


def _sigmoid(x):
    return 1.0 / (1.0 + jnp.exp(-x))


def _silu(x):
    h = 0.5 * x
    return h + h * jnp.tanh(h)


def _normed(x_ref, g_ref):
    return (_rms(x_ref[...], -1) * g_ref[...]).astype(BF16)


def _norm_matmul_body(x_ref, g_ref, *refs, chunk):
    n_out = len(refs) // 2
    n = _normed(x_ref, g_ref)
    for w_ref, o_ref in zip(refs[:n_out], refs[n_out:]):
        cols = w_ref.shape[1]
        step = min(chunk, cols)
        for c0 in range(0, cols, step):
            o_ref[:, c0:c0 + step] = jnp.dot(n, w_ref[:, c0:c0 + step],
                                             preferred_element_type=F32).astype(o_ref.dtype)


def _norm_matmul(x2d, g_row, weights, out_dtypes, tm):
    t, d = x2d.shape
    return pl.pallas_call(
        functools.partial(_norm_matmul_body, chunk=1024),
        grid=(t // tm,),
        in_specs=[pl.BlockSpec((tm, d), lambda i: (i, 0)),
                  pl.BlockSpec((1, d), lambda i: (0, 0))]
                 + [pl.BlockSpec(w.shape, lambda i: (0, 0)) for w in weights],
        out_specs=[pl.BlockSpec((tm, w.shape[1]), lambda i: (i, 0)) for w in weights],
        out_shape=[jax.ShapeDtypeStruct((t, w.shape[1]), dt) for w, dt in zip(weights, out_dtypes)],
        compiler_params=_params("parallel"),
        name="norm_matmul",
    )(x2d, g_row, *weights)


def _norm_matmul_t_body(x_ref, g_ref, wt_ref, o_ref, *, chunk):
    n = _normed(x_ref, g_ref)
    for r0 in range(0, wt_ref.shape[0], chunk):
        o_ref[r0:r0 + chunk, :] = lax.dot_general(
            wt_ref[r0:r0 + chunk, :], n, (((1,), (1,)), ((), ())),
            preferred_element_type=F32).astype(o_ref.dtype)


def _norm_matmul_t(x2d, g_row, wt, out_dtype, tm):
    t, d = x2d.shape
    rows = wt.shape[0]
    return pl.pallas_call(
        functools.partial(_norm_matmul_t_body, chunk=512),
        grid=(t // tm,),
        in_specs=[pl.BlockSpec((tm, d), lambda i: (i, 0)),
                  pl.BlockSpec((1, d), lambda i: (0, 0)),
                  pl.BlockSpec((rows, d), lambda i: (0, 0))],
        out_specs=pl.BlockSpec((rows, tm), lambda i: (0, i)),
        out_shape=jax.ShapeDtypeStruct((rows, t), out_dtype),
        compiler_params=_params("parallel"),
        name="norm_matmul_t",
    )(x2d, g_row, wt)


BIAS_PREV, BIAS_DIAG, BIAS_MASKED, BIAS_NONE = 0, 1, 2, 3


def _bucket_table(tile):
    k = np.arange(tile)[:, None]
    q = np.arange(tile)[None, :]

    def bucket(dist):
        nf = np.maximum(dist, 1).astype(np.float64)
        large = REL_MAX_EXACT + np.trunc(
            np.log(nf / REL_MAX_EXACT) / math.log(REL_MAX_DIST / REL_MAX_EXACT)
            * (REL_BUCKETS - REL_MAX_EXACT)).astype(np.int64)
        large = np.minimum(large, REL_BUCKETS - 1)
        return np.where(dist < REL_MAX_EXACT, dist, large)

    prev = bucket(q - k + tile)
    diag = np.where(q >= k, bucket(np.maximum(q - k, 0)), -1)
    masked = np.full((tile, tile), -1)
    far = np.full((tile, tile), REL_BUCKETS - 1)
    return np.concatenate([prev, diag, masked, far], axis=0).astype(np.int32)


def _bias_table_body(rb_ref, bucket_ref, o_ref):
    h = pl.program_id(0)
    far = rb_ref[REL_BUCKETS - 1, h]
    bk = bucket_ref[...]
    out = jnp.full(bk.shape, MASK_VALUE, F32)
    for b in range(REL_BUCKETS):
        out = jnp.where(bk == b, (rb_ref[b, h] - far) * LOG2E, out)
    o_ref[...] = out


def _bias_table(rel_bias, tile):
    heads = rel_bias.shape[1]
    buckets = jnp.asarray(_bucket_table(tile))
    rows = buckets.shape[0]
    return pl.pallas_call(
        _bias_table_body,
        grid=(heads,),
        in_specs=[pl.BlockSpec(memory_space=pltpu.SMEM),
                  pl.BlockSpec((rows, tile), lambda h: (0, 0))],
        out_specs=pl.BlockSpec((None, rows, tile), lambda h: (h, 0, 0)),
        out_shape=jax.ShapeDtypeStruct((heads, rows, tile), F32),
        compiler_params=_params("arbitrary"),
        name="bias_table",
    )(rel_bias, buckets)


def _attn_body(lam_ref, qt_ref, k_ref, vt_ref, bias_ref, g_ref, o_ref,
               qz_ref, m_ref, x_ref, l_ref, acc_ref, p_ref, *, tile, nsub, lam_init):
    j = pl.program_id(2)
    hd = 2 * DA_HEAD_DIM
    qw = nsub * tile
    t0 = nsub * j

    sub = 2 * tile
    qt = qt_ref[...]
    row = lax.broadcasted_iota(jnp.int32, (hd, tile), 0)
    zero = jnp.zeros((hd, tile), qt.dtype)
    for a in range(nsub):
        qa = qt[:, a * tile:(a + 1) * tile]
        qz_ref[:, a * sub:a * sub + tile] = jnp.where(row < DA_HEAD_DIM, qa, zero)
        qz_ref[:, a * sub + tile:(a + 1) * sub] = jnp.where(row >= DA_HEAD_DIM, qa, zero)

    def tile_off(t):
        return pl.multiple_of(jnp.maximum(t, 0) * tile, tile)

    def bias_block(block):
        b = bias_ref[pl.ds(pl.multiple_of(block * tile, tile), tile), :]
        return jnp.concatenate([b, b], axis=1)

    def bias_cols(t):
        parts = []
        for a in range(nsub):
            rel = t - (t0 + a)
            parts.append(bias_block(jnp.where(
                rel == 0, BIAS_DIAG, jnp.where(rel == -1, BIAS_PREV, jnp.where(rel < -1, BIAS_NONE, BIAS_MASKED)))))
        return jnp.concatenate(parts, axis=1)

    def scores(off, first_sub=0, kinds=None):
        s = jnp.dot(k_ref[pl.ds(off, tile), :], qz_ref[:, first_sub * sub:], preferred_element_type=F32)
        if kinds is None:
            return s
        pieces = [s[:, i * sub:(i + 1) * sub] for i in range(len(kinds))]
        return jnp.concatenate([x if kind is None else x + bias_block(kind) for x, kind in zip(pieces, kinds)], axis=1)

    def values(off, n_keys, p):
        return jnp.dot(vt_ref[:, pl.ds(off, n_keys)], p, preferred_element_type=F32)

    def by_sublane(s):
        return s.reshape(tile // SUBLANES, SUBLANES, s.shape[1])

    def exact_tile(off, bias, first):
        s = scores(off) + bias
        c = jnp.max(s, axis=0, keepdims=True)
        if first:
            m_new = c
        else:
            m_old = m_ref[...]
            m_new = jnp.maximum(m_old, c)
            alpha = jnp.exp2(m_old - m_new)
        p = jnp.exp2(s - m_new)
        rows = jnp.sum(by_sublane(p), axis=0)
        pv = values(off, tile, p.astype(BF16))
        acc_ref[...] = pv if first else alpha * acc_ref[...] + pv
        l_ref[...] = rows if first else alpha * l_ref[...] + rows
        m_ref[...] = m_new

    def diagonal_block():
        acc = lsum = xmax = None
        for c in reversed(range(nsub)):
            kinds = [BIAS_DIAG if a == c else BIAS_PREV if a == c + 1 else None for a in range(c, nsub)]
            s = scores(tile_off(t0 + c), c, kinds)
            p = jnp.exp2(s)
            part = jnp.max(by_sublane(s), axis=0)
            rows = jnp.sum(by_sublane(p), axis=0)
            pv = values(tile_off(t0 + c), tile, p.astype(BF16))
            if acc is None:
                acc, lsum, xmax = pv, rows, part
            else:
                acc = jnp.concatenate([pv[:, :sub], pv[:, sub:] + acc], axis=1)
                lsum = jnp.concatenate([rows[:, :sub], rows[:, sub:] + lsum], axis=1)
                xmax = jnp.concatenate([part[:, :sub], jnp.maximum(part[:, sub:], xmax)], axis=1)
        acc_ref[...], l_ref[...], x_ref[...] = acc, lsum, xmax

    def far_tiles(start, count, last_kinds=None):
        cpart = None
        lpart = None
        for i in range(count):
            s = scores(tile_off(start + i), 0, last_kinds if i == count - 1 else None)
            part = jnp.max(by_sublane(s), axis=0)
            cpart = part if cpart is None else jnp.maximum(cpart, part)
            p = jnp.exp2(s)
            rows = jnp.sum(by_sublane(p), axis=0)
            lpart = rows if lpart is None else lpart + rows
            p_ref[i * tile:(i + 1) * tile, :] = p.astype(BF16)
        n_keys = count * tile
        acc_ref[...] = acc_ref[...] + values(tile_off(start), n_keys, p_ref[:n_keys, :])
        l_ref[...] = l_ref[...] + lpart
        x_ref[...] = jnp.maximum(x_ref[...], cpart)

    def finish():
        lv = lam_ref[...]
        lam = (jnp.exp(jnp.sum(lv[0:1] * lv[1:2], axis=-1, keepdims=True))
               - jnp.exp(jnp.sum(lv[2:3] * lv[3:4], axis=-1, keepdims=True)) + lam_init)
        acc = acc_ref[...] * (1.0 / jnp.sum(l_ref[...], axis=0, keepdims=True))
        ot = jnp.concatenate([acc[:, a * sub:a * sub + tile] - lam * acc[:, a * sub + tile:(a + 1) * sub]
                              for a in range(nsub)], axis=1)
        o = _rms(ot, 0).T * g_ref[...] * (1.0 - lam_init)
        o_ref[...] = o.astype(o_ref.dtype)

    assert nsub % 2 == 0
    n_even = jnp.maximum(t0 - 2, 0)
    diagonal_block()

    @pl.when(j >= 1)
    def _():
        far_tiles(t0 - 2, 2, [BIAS_PREV] + [None] * (nsub - 1))

    def trip(u, carry):
        far_tiles(u * ATTN_TRIP_TILES, ATTN_TRIP_TILES)
        return carry

    n_trips = n_even // ATTN_TRIP_TILES
    lax.fori_loop(0, n_trips, trip, 0)
    done = n_trips * ATTN_TRIP_TILES
    count = ATTN_TRIP_TILES // 2
    while count >= 2:
        take = (n_even - done) >= count

        @pl.when(take)
        def _(done=done, count=count):
            far_tiles(done, count)

        done = done + jnp.where(take, count, 0)
        count //= 2
    finish()

    col_max = jnp.max(x_ref[...], axis=0, keepdims=True)

    @pl.when(jnp.max(jnp.abs(col_max)) > SCORE_LIMIT)
    def _():
        exact_tile(tile_off(t0), bias_cols(t0), first=True)

        def body(u, carry):
            t = u + jnp.where(u >= t0, 1, 0)
            exact_tile(tile_off(t), bias_cols(t), first=False)
            return carry

        lax.fori_loop(0, t0 + nsub - 1, body, 0)
        finish()


def _diff_attention(lam_vecs, qvt, kdn, bias, g_row, batch, seq, lam_init):
    tile = ATTN_TILE
    qw = ATTN_QTILES * tile
    assert seq % qw == 0
    nqt = seq // qw
    hd = 2 * DA_HEAD_DIM
    return pl.pallas_call(
        functools.partial(_attn_body, tile=tile, nsub=ATTN_QTILES, lam_init=lam_init),
        grid=(batch, DA_HEADS, nqt),
        in_specs=[pl.BlockSpec((4, DA_HEAD_DIM), lambda b, h, i: (0, 0)),
                  pl.BlockSpec((hd, qw), lambda b, h, i: (h, b * nqt + i)),
                  pl.BlockSpec((seq, hd), lambda b, h, i: (b, h)),
                  pl.BlockSpec((hd, seq), lambda b, h, i: (DA_HEADS + h, b)),
                  pl.BlockSpec((None,) + bias.shape[1:], lambda b, h, i: (h, 0, 0)),
                  pl.BlockSpec((1, hd), lambda b, h, i: (0, 0))],
        out_specs=pl.BlockSpec((qw, hd), lambda b, h, i: (b * nqt + i, h)),
        out_shape=jax.ShapeDtypeStruct((batch * seq, DA_HEADS * hd), BF16),
        scratch_shapes=[pltpu.VMEM((hd, 2 * qw), BF16),
                        pltpu.VMEM((1, 2 * qw), F32),
                        pltpu.VMEM((SUBLANES, 2 * qw), F32),
                        pltpu.VMEM((SUBLANES, 2 * qw), F32),
                        pltpu.VMEM((hd, 2 * qw), F32),
                        pltpu.VMEM((ATTN_TRIP_TILES * tile, 2 * qw), BF16)],
        compiler_params=_params("parallel", "parallel", "arbitrary"),
        name="diff_attn",
    )(lam_vecs, qvt, kdn, qvt, bias, g_row)


def _shift_rows(x, prev, s):
    if s == 0:
        return x
    rolled = pltpu.roll(x, s, 0)
    prolled = pltpu.roll(prev, s, 0)
    r8 = lax.broadcasted_iota(jnp.int32, prev.shape, 0)
    top = jnp.where(r8 < s, prolled, rolled[:SUBLANES])
    return jnp.concatenate([top, rolled[SUBLANES:]], axis=0)


def _dn_proj_body(x_ref, g_ref, w_ref, cw_ref, o_ref, xq_ref, xk_ref, xv_ref, tail_ref, *, tiles_per_seq):
    width = DN_HEADS * DN_DIM
    rows = x_ref.shape[0]

    @pl.when(pl.program_id(0) % tiles_per_seq == 0)
    def _():
        for x_scr in (xq_ref, xk_ref, xv_ref):
            x_scr[:SUBLANES, :] = jnp.zeros((SUBLANES, width), F32)

    @pl.when(pl.program_id(0) % tiles_per_seq != 0)
    def _():
        for which, x_scr in enumerate((xq_ref, xk_ref, xv_ref)):
            x_scr[:SUBLANES, :] = tail_ref[which]

    n = _normed(x_ref, g_ref)

    def project(group):
        return jnp.dot(n, w_ref[:, group * width:(group + 1) * width], preferred_element_type=F32)

    for which, x_scr in enumerate((xq_ref, xk_ref, xv_ref)):
        xp = project(1 + which)
        x_scr[SUBLANES:, :] = xp
        tail_ref[which] = xp[rows - SUBLANES:]
    for group in (0, 4):
        o_ref[:, group * width:(group + 1) * width] = project(group).astype(o_ref.dtype)
    for which, x_scr in enumerate((xq_ref, xk_ref, xv_ref)):
        for h in range(DN_HEADS):
            cols = slice(h * DN_DIM, (h + 1) * DN_DIM)
            w = 0.5 * cw_ref[:, which * width + h * DN_DIM:which * width + (h + 1) * DN_DIM]
            y = w[DN_CONV - 1:DN_CONV] * x_scr[SUBLANES:, cols]
            for s in range(1, DN_CONV):
                y = y + w[DN_CONV - 1 - s:DN_CONV - s] * x_scr[SUBLANES - s:SUBLANES - s + rows, cols]
            y = y + y * jnp.tanh(y)
            if which == 0:
                y = y * lax.rsqrt(jnp.sum(y * y, axis=-1, keepdims=True) + EPS) * (DN_DIM ** -0.5)
            elif which == 1:
                y = y * lax.rsqrt(jnp.sum(y * y, axis=-1, keepdims=True) + EPS)
            c0 = (1 + which) * width + h * DN_DIM
            o_ref[:, c0:c0 + DN_DIM] = y.astype(o_ref.dtype)


def _dn_proj(x2d, g_row, w_kdn, conv_w, seq, tm):
    t, d = x2d.shape
    width = DN_HEADS * DN_DIM
    x_scr = pltpu.VMEM((SUBLANES + tm, width), F32)
    return pl.pallas_call(
        functools.partial(_dn_proj_body, tiles_per_seq=seq // tm),
        grid=(t // tm,),
        in_specs=[pl.BlockSpec((tm, d), lambda i: (i, 0)),
                  pl.BlockSpec((1, d), lambda i: (0, 0)),
                  pl.BlockSpec(w_kdn.shape, lambda i: (0, 0)),
                  pl.BlockSpec(conv_w.shape, lambda i: (0, 0))],
        out_specs=pl.BlockSpec((tm, w_kdn.shape[1]), lambda i: (i, 0)),
        out_shape=jax.ShapeDtypeStruct((t, w_kdn.shape[1]), BF16),
        scratch_shapes=[x_scr, x_scr, x_scr, pltpu.VMEM((3, SUBLANES, width), F32)],
        compiler_params=_params("arbitrary"),
        name="dn_proj",
    )(x2d, g_row, w_kdn, conv_w)


def _gdn_body(q_ref, k_ref, v_ref, z_ref, ba_ref, alog_ref, dtb_ref, gn_ref, o_ref, state_ref, *, npairs):
    p = DN_PAIR
    c = DN_CHUNK
    heads = range(DN_HEADS)
    head_cols = [slice(h * DN_DIM, (h + 1) * DN_DIM) for h in heads]

    @pl.when(pl.program_id(1) == 0)
    def _():
        state_ref[...] = jnp.zeros(state_ref.shape, F32)

    ri = lax.broadcasted_iota(jnp.int32, (p, p), 0)
    ci = lax.broadcasted_iota(jnp.int32, (p, p), 1)
    same = (ri // c) == (ci // c)
    causal = same & (ri >= ci)
    strict = same & (ri > ci)
    cum01 = jnp.where(causal, 1.0, 0.0).astype(BF16)

    def lane(x, j):
        return jnp.broadcast_to(x[:, j:j + 1], (x.shape[0], DN_DIM))

    first_chunk_lanes = lax.broadcasted_iota(jnp.int32, (c, p), 1) < c

    def side_by_side(bd):
        return jnp.where(first_chunk_lanes, bd[:c], bd[c:])

    def block_diagonal(ss):
        return jnp.concatenate([jnp.where(first_chunk_lanes, ss, 0.0), jnp.where(first_chunk_lanes, 0.0, ss)], axis=0)

    def prepare(pr):
        tok = slice(pr * p, (pr + 1) * p)
        ba = ba_ref[tok, :]
        beta_all = _sigmoid(ba)
        xs = ba + dtb_ref[...]
        softplus = jnp.maximum(xs, 0.0) + jnp.log(1.0 + jnp.exp(-jnp.abs(xs)))
        g_all = -jnp.exp(alog_ref[...]) * softplus
        gc_all = _mm_exact_rhs(cum01, g_all)
        glast_all = jnp.concatenate(
            [jnp.broadcast_to(gc_all[c - 1:c], (c, LANES)), jnp.broadcast_to(gc_all[p - 1:p], (c, LANES))], axis=0)
        tail_all = jnp.exp(glast_all - gc_all)
        dl_all = jnp.exp(glast_all)

        q = [q_ref[tok, cols].astype(F32) for cols in head_cols]
        k = [k_ref[tok, cols].astype(F32) for cols in head_cols]
        v = [v_ref[tok, cols].astype(F32) for cols in head_cols]
        gc = [lane(gc_all, DN_HEADS + h) for h in heads]
        decay = [jnp.exp(jnp.where(causal, x - x.T, MASK_VALUE)) for x in gc]
        exp_g = [jnp.exp(x) for x in gc]
        beta = [lane(beta_all, h) for h in heads]
        kb = [k[h] * beta[h] for h in heads]
        vb = [v[h] * beta[h] for h in heads]
        kq = [_mm_nt(jnp.concatenate([kb[h], q[h]], axis=0), k[h]) for h in heads]
        low = [jnp.where(strict, kq[h][:p] * decay[h], 0.0) for h in heads]
        intra = [jnp.where(causal, kq[h][p:] * decay[h], 0.0) for h in heads]
        y = [-side_by_side(x) for x in low]
        pw = [_mm(side_by_side(x), x) for x in low]
        for it in range(5):
            pw_bd = [block_diagonal(x) for x in pw]
            if it < 4:
                both = [_mm(jnp.concatenate([y[h], pw[h]], axis=0), pw_bd[h]) for h in heads]
                y = [y[h] + pw[h] + both[h][:c] for h in heads]
                pw = [both[h][c:] for h in heads]
            else:
                y = [y[h] + pw[h] + _mm(y[h], pw_bd[h]) for h in heads]
        rhs = [jnp.concatenate([vb[h], kb[h] * exp_g[h]], axis=1) for h in heads]
        sol = [rhs[h] + _mm(block_diagonal(y[h]), rhs[h]) for h in heads]
        q_dec = [q[h] * exp_g[h] for h in heads]
        kt_t = [(k[h] * lane(tail_all, DN_HEADS + h)).T for h in heads]
        return sol, intra, q_dec, kt_t, dl_all

    prepared = [prepare(pr) for pr in range(npairs)]

    zeros = jnp.zeros((c, DN_DIM), F32)
    for pr in range(npairs):
        sol, intra, q_dec, kt_t, dl_all = prepared[pr]
        tok = slice(pr * p, (pr + 1) * p)
        outs = [[], []]
        for ch in range(2):
            rows = slice(ch * c, (ch + 1) * c)
            st = [state_ref[h] for h in heads]
            on_state = [_mm(jnp.concatenate([sol[h][rows, DN_DIM:], q_dec[h][rows]], axis=0), st[h]) for h in heads]
            v_new = [sol[h][rows, :DN_DIM] - on_state[h][:c] for h in heads]
            v_pad = [jnp.concatenate([x, zeros] if ch == 0 else [zeros, x], axis=0) for x in v_new]
            on_v = [_mm(jnp.concatenate([intra[h][rows], kt_t[h]], axis=0), v_pad[h]) for h in heads]
            outs[ch] = [on_state[h][c:] + on_v[h][:c] for h in heads]
            for h in heads:
                dl = jnp.broadcast_to(dl_all[ch * c:ch * c + 1, DN_HEADS + h:DN_HEADS + h + 1], (1, DN_DIM))
                state_ref[h] = st[h] * dl + on_v[h][c:]
        for h in heads:
            o = jnp.concatenate([outs[0][h], outs[1][h]], axis=0)
            yb = _rms(o, -1) * gn_ref[...] * _silu(z_ref[tok, head_cols[h]].astype(F32))
            o_ref[tok, head_cols[h]] = yb.astype(o_ref.dtype)


def _gdn(kdn, ba, alog_row, dtb_row, gn_row, batch, seq):
    t = batch * seq
    p = GDN_STEP_PAIRS * DN_PAIR
    nt = seq // p
    width = DN_HEADS * DN_DIM
    tok = lambda col: pl.BlockSpec((p, width), lambda b, i: (b * nt + i, col))
    row = pl.BlockSpec((1, LANES), lambda b, i: (0, 0))
    return pl.pallas_call(
        functools.partial(_gdn_body, npairs=GDN_STEP_PAIRS),
        grid=(batch, nt),
        in_specs=[tok(1), tok(2), tok(3), tok(4),
                  pl.BlockSpec((p, LANES), lambda b, i: (b * nt + i, 0)),
                  row, row, row],
        out_specs=pl.BlockSpec((p, width), lambda b, i: (b * nt + i, 0)),
        out_shape=jax.ShapeDtypeStruct((t, width), BF16),
        scratch_shapes=[pltpu.VMEM((DN_HEADS, DN_DIM, DN_DIM), F32)],
        compiler_params=_params("parallel", "arbitrary"),
        name="gdn",
    )(kdn, kdn, kdn, kdn, ba, alog_row, dtb_row, gn_row)


def _merge_body(x_ref, cq_ref, g0_ref, g1_ref, g2_ref, ya_ref, yb_ref, mkt_ref, mv_ref,
                wb_ref, wo_ref, ln_ref, o_ref):
    cq = cq_ref[...]
    heads = []
    for h in range(CA_HEADS):
        qh = cq[:, h * CA_HEAD_DIM:(h + 1) * CA_HEAD_DIM]
        s = jnp.dot(qh, mkt_ref[h], preferred_element_type=F32) * (CA_HEAD_DIM ** -0.5)
        e = jnp.exp(s - jnp.max(s, axis=-1, keepdims=True))
        pr = e / jnp.sum(e, axis=-1, keepdims=True)
        heads.append(jnp.dot(pr.astype(BF16), mv_ref[h], preferred_element_type=F32))
    yc = jnp.concatenate(heads, axis=1)

    mixed = _sigmoid(g0_ref[...].astype(F32)) * jnp.dot(ya_ref[...], wb_ref[0], preferred_element_type=F32)
    mixed = mixed + _sigmoid(g1_ref[...].astype(F32)) * jnp.dot(yb_ref[...], wb_ref[1], preferred_element_type=F32)
    mixed = mixed + _sigmoid(g2_ref[...].astype(F32)) * _mm(yc, wb_ref[2])
    out = _mm(mixed, wo_ref[...])
    o_ref[...] = x_ref[...] + _rms(out, -1) * ln_ref[...]


def _merge(x2d, cg, ya, yb, mkt, mv, wb, wo, ln_row, batch, seq, tm):
    t, d = x2d.shape
    nt = seq // tm
    tok = lambda col: pl.BlockSpec((tm, d), lambda i: (i, col))
    mem = pl.BlockSpec((None, CA_HEADS, CA_HEAD_DIM, CA_HEAD_DIM), lambda i: (i // nt, 0, 0, 0))
    return pl.pallas_call(
        _merge_body,
        grid=(t // tm,),
        in_specs=[tok(0), tok(0), tok(1), tok(2), tok(3), tok(0), tok(0), mem, mem,
                  pl.BlockSpec((3, d, d), lambda i: (0, 0, 0)),
                  pl.BlockSpec((d, d), lambda i: (0, 0)),
                  pl.BlockSpec((1, d), lambda i: (0, 0))],
        out_specs=tok(0),
        out_shape=jax.ShapeDtypeStruct((t, d), F32),
        compiler_params=_params("parallel"),
        name="merge",
    )(x2d, cg, cg, cg, cg, ya, yb, mkt, mv, wb, wo, ln_row)


def _ffn_body(h_ref, pre_ref, post_ref, w1_ref, w2_ref, o_ref, *, chunk):
    hh = h_ref[...]
    n = (_rms(hh, -1) * pre_ref[...]).astype(BF16)
    d_ff = w1_ref.shape[1]
    f = jnp.zeros(hh.shape, F32)
    for c0 in range(0, d_ff, chunk):
        a = jnp.dot(n, w1_ref[:, c0:c0 + chunk], preferred_element_type=F32)
        a = jnp.square(jnp.maximum(a, 0.0))
        f = f + jnp.dot(a.astype(BF16), w2_ref[c0:c0 + chunk, :], preferred_element_type=F32)
    o_ref[...] = hh + _rms(f, -1) * post_ref[...]


def _ffn(h2d, pre_row, post_row, w1, w2, tm):
    t, d = h2d.shape
    d_ff = w1.shape[1]
    return pl.pallas_call(
        functools.partial(_ffn_body, chunk=1024),
        grid=(t // tm,),
        in_specs=[pl.BlockSpec((tm, d), lambda i: (i, 0)),
                  pl.BlockSpec((1, d), lambda i: (0, 0)),
                  pl.BlockSpec((1, d), lambda i: (0, 0)),
                  pl.BlockSpec((d, d_ff), lambda i: (0, 0)),
                  pl.BlockSpec((d_ff, d), lambda i: (0, 0))],
        out_specs=pl.BlockSpec((tm, d), lambda i: (i, 0)),
        out_shape=jax.ShapeDtypeStruct((t, d), F32),
        compiler_params=_params("parallel"),
        name="ffn",
    )(h2d, pre_row, post_row, w1, w2)


def _pad_lanes(vec, offset):
    return jnp.zeros((1, LANES), F32).at[0, offset:offset + vec.shape[0]].set(vec.astype(F32))


def _layer(h2d, mem2d, rel_bias, w_in, conv_w, dn_a_log, dn_dt_bias, dn_norm_g, da_lambda,
           da_subln_g, mem_norm_g, w_mem_kv, w_branch, w_out, ln_mix_pre, ln_mix_post,
           ln_ff_pre, ln_ff_post, w_ff1, w_ff2, batch, seq, mem_len, layer):
    d = h2d.shape[1]
    t = batch * seq
    lam_init = 0.8 - 0.6 * math.exp(-0.3 * layer)
    tm = min(512, seq)
    row = lambda v: v.reshape(1, -1).astype(F32)

    small0 = 7 * d
    rest0 = small0 + 2 * DN_HEADS
    w_q = w_in[:, :d] * (DA_HEAD_DIM ** -0.5 * LOG2E)
    w_qv_t = jnp.concatenate([w_q, w_in[:, 2 * d:3 * d]], axis=1).T.astype(BF16)
    w_kdn = jnp.concatenate([w_in[:, d:2 * d], w_in[:, 3 * d:small0]], axis=1).astype(BF16)
    w_small = jnp.zeros((d, LANES), F32).at[:, :2 * DN_HEADS].set(w_in[:, small0:rest0]).astype(BF16)
    w_rest = w_in[:, rest0:].astype(BF16)
    g_pre = row(ln_mix_pre)

    qvt = _norm_matmul_t(h2d, g_pre, w_qv_t, BF16, tm)
    kdn = _dn_proj(h2d, g_pre, w_kdn, conv_w.astype(F32), seq, min(256, seq))
    cg, ba = _norm_matmul(h2d, g_pre, [w_rest, w_small], [BF16, F32], tm)
    (mkv,) = _norm_matmul(mem2d, row(mem_norm_g), [w_mem_kv.astype(BF16)], [BF16], min(512, batch * mem_len))

    bias = _bias_table(rel_bias.astype(F32), ATTN_TILE)
    y_a = _diff_attention(da_lambda.astype(F32), qvt, kdn, bias, row(da_subln_g), batch, seq, lam_init)

    y_b = _gdn(kdn, ba, _pad_lanes(dn_a_log, DN_HEADS), _pad_lanes(dn_dt_bias, DN_HEADS),
               row(dn_norm_g), batch, seq)

    mkt = mkv[:, :d].reshape(batch, mem_len, CA_HEADS, CA_HEAD_DIM).transpose(0, 2, 3, 1)
    mv = mkv[:, d:].reshape(batch, mem_len, CA_HEADS, CA_HEAD_DIM).transpose(0, 2, 1, 3)
    h1 = _merge(h2d, cg, y_a, y_b, mkt, mv, w_branch.astype(BF16), w_out.astype(BF16),
                row(ln_mix_post), batch, seq, min(512, seq))

    return _ffn(h1, row(ln_ff_pre), row(ln_ff_post), w_ff1.astype(BF16), w_ff2.astype(BF16), min(512, seq))


def kernel(x, mem, rel_bias, w_in, conv_w, dn_a_log, dn_dt_bias, dn_norm_g, da_lambda, da_subln_g,
           mem_norm_g, w_mem_kv, w_branch, w_out, ln_mix_pre, ln_mix_post, ln_ff_pre, ln_ff_post,
           w_ff1, w_ff2):
    batch, seq, d = x.shape
    mem_len = mem.shape[1]
    assert seq % (ATTN_QTILES * ATTN_TILE) == 0 and seq % DN_PAIR == 0
    h2d = x.reshape(batch * seq, d)
    mem2d = mem.reshape(batch * mem_len, d)
    for layer in range(w_in.shape[0]):
        h2d = _layer(h2d, mem2d, rel_bias, w_in[layer], conv_w[layer], dn_a_log[layer], dn_dt_bias[layer],
                     dn_norm_g[layer], da_lambda[layer], da_subln_g[layer], mem_norm_g[layer],
                     w_mem_kv[layer], w_branch[layer], w_out[layer], ln_mix_pre[layer], ln_mix_post[layer],
                     ln_ff_pre[layer], ln_ff_post[layer], w_ff1[layer], w_ff2[layer],
                     batch, seq, mem_len, layer)
    return h2d.reshape(batch, seq, d)
```

```python
import functools
import math

import numpy as np
import jax
import jax.numpy as jnp
from jax import lax
from jax.experimental import pallas as pl
from jax.experimental.pallas import tpu as pltpu

F32 = jnp.float32
BF16 = jnp.bfloat16

EPS = 1e-6
MASK_VALUE = -1e30
LOG2E = math.log2(math.e)
LANES = 128
SUBLANES = 8
VMEM_LIMIT_BYTES = 56 * 1024 * 1024

DA_HEADS = 8
DA_HEAD_DIM = 64
DN_HEADS = 8
DN_DIM = 128
DN_CONV = 4
DN_CHUNK = 64
DN_PAIR = 2 * DN_CHUNK
CA_HEADS = 4
CA_HEAD_DIM = 256
REL_BUCKETS = 32
REL_MAX_EXACT = 16
REL_MAX_DIST = 128

GDN_STEP_PAIRS = 4
ATTN_TILE = 256
ATTN_QTILES = 4
ATTN_TRIP_TILES = 8
SCORE_LIMIT = 60.0


def _params(*sem):
    return pltpu.CompilerParams(dimension_semantics=sem, vmem_limit_bytes=VMEM_LIMIT_BYTES)


def _mm(a, b):
    return jnp.dot(a.astype(BF16), b.astype(BF16), preferred_element_type=F32)


def _mm_nt(a, b):
    return lax.dot_general(a.astype(BF16), b.astype(BF16), (((1,), (1,)), ((), ())),
                           preferred_element_type=F32)


def _mm_exact_rhs(m01, a):
    hi = a.astype(BF16)
    r1 = a - hi.astype(F32)
    mid = r1.astype(BF16)
    lo = (r1 - mid.astype(F32)).astype(BF16)
    dot = functools.partial(jnp.dot, preferred_element_type=F32)
    return dot(m01, hi) + dot(m01, mid) + dot(m01, lo)


def _rms(x, axis):
    return x * lax.rsqrt(jnp.mean(x * x, axis=axis, keepdims=True) + EPS)


def _sigmoid(x):
    return 1.0 / (1.0 + jnp.exp(-x))


def _silu(x):
    h = 0.5 * x
    return h + h * jnp.tanh(h)


def _normed(x_ref, g_ref):
    return (_rms(x_ref[...], -1) * g_ref[...]).astype(BF16)


def _norm_matmul_body(x_ref, g_ref, *refs, chunk):
    n_out = len(refs) // 2
    n = _normed(x_ref, g_ref)
    for w_ref, o_ref in zip(refs[:n_out], refs[n_out:]):
        cols = w_ref.shape[1]
        step = min(chunk, cols)
        for c0 in range(0, cols, step):
            o_ref[:, c0:c0 + step] = jnp.dot(n, w_ref[:, c0:c0 + step],
                                             preferred_element_type=F32).astype(o_ref.dtype)


def _norm_matmul(x2d, g_row, weights, out_dtypes, tm):
    t, d = x2d.shape
    return pl.pallas_call(
        functools.partial(_norm_matmul_body, chunk=1024),
        grid=(t // tm,),
        in_specs=[pl.BlockSpec((tm, d), lambda i: (i, 0)),
                  pl.BlockSpec((1, d), lambda i: (0, 0))]
                 + [pl.BlockSpec(w.shape, lambda i: (0, 0)) for w in weights],
        out_specs=[pl.BlockSpec((tm, w.shape[1]), lambda i: (i, 0)) for w in weights],
        out_shape=[jax.ShapeDtypeStruct((t, w.shape[1]), dt) for w, dt in zip(weights, out_dtypes)],
        compiler_params=_params("parallel"),
        name="norm_matmul",
    )(x2d, g_row, *weights)


def _norm_matmul_t_body(x_ref, g_ref, wt_ref, o_ref, *, chunk):
    n = _normed(x_ref, g_ref)
    for r0 in range(0, wt_ref.shape[0], chunk):
        o_ref[r0:r0 + chunk, :] = lax.dot_general(
            wt_ref[r0:r0 + chunk, :], n, (((1,), (1,)), ((), ())),
            preferred_element_type=F32).astype(o_ref.dtype)


def _norm_matmul_t(x2d, g_row, wt, out_dtype, tm):
    t, d = x2d.shape
    rows = wt.shape[0]
    return pl.pallas_call(
        functools.partial(_norm_matmul_t_body, chunk=512),
        grid=(t // tm,),
        in_specs=[pl.BlockSpec((tm, d), lambda i: (i, 0)),
                  pl.BlockSpec((1, d), lambda i: (0, 0)),
                  pl.BlockSpec((rows, d), lambda i: (0, 0))],
        out_specs=pl.BlockSpec((rows, tm), lambda i: (0, i)),
        out_shape=jax.ShapeDtypeStruct((rows, t), out_dtype),
        compiler_params=_params("parallel"),
        name="norm_matmul_t",
    )(x2d, g_row, wt)


BIAS_PREV, BIAS_DIAG, BIAS_MASKED, BIAS_NONE = 0, 1, 2, 3


def _bucket_table(tile):
    k = np.arange(tile)[:, None]
    q = np.arange(tile)[None, :]

    def bucket(dist):
        nf = np.maximum(dist, 1).astype(np.float64)
        large = REL_MAX_EXACT + np.trunc(
            np.log(nf / REL_MAX_EXACT) / math.log(REL_MAX_DIST / REL_MAX_EXACT)
            * (REL_BUCKETS - REL_MAX_EXACT)).astype(np.int64)
        large = np.minimum(large, REL_BUCKETS - 1)
        return np.where(dist < REL_MAX_EXACT, dist, large)

    prev = bucket(q - k + tile)
    diag = np.where(q >= k, bucket(np.maximum(q - k, 0)), -1)
    masked = np.full((tile, tile), -1)
    far = np.full((tile, tile), REL_BUCKETS - 1)
    return np.concatenate([prev, diag, masked, far], axis=0).astype(np.int32)


def _bias_table_body(rb_ref, bucket_ref, o_ref):
    h = pl.program_id(0)
    far = rb_ref[REL_BUCKETS - 1, h]
    bk = bucket_ref[...]
    out = jnp.full(bk.shape, MASK_VALUE, F32)
    for b in range(REL_BUCKETS):
        out = jnp.where(bk == b, (rb_ref[b, h] - far) * LOG2E, out)
    o_ref[...] = out


def _bias_table(rel_bias, tile):
    heads = rel_bias.shape[1]
    buckets = jnp.asarray(_bucket_table(tile))
    rows = buckets.shape[0]
    return pl.pallas_call(
        _bias_table_body,
        grid=(heads,),
        in_specs=[pl.BlockSpec(memory_space=pltpu.SMEM),
                  pl.BlockSpec((rows, tile), lambda h: (0, 0))],
        out_specs=pl.BlockSpec((None, rows, tile), lambda h: (h, 0, 0)),
        out_shape=jax.ShapeDtypeStruct((heads, rows, tile), F32),
        compiler_params=_params("arbitrary"),
        name="bias_table",
    )(rel_bias, buckets)


def _attn_body(lam_ref, qt_ref, k_ref, vt_ref, bias_ref, g_ref, o_ref,
               qz_ref, m_ref, x_ref, l_ref, acc_ref, p_ref, *, tile, nsub, lam_init):
    j = pl.program_id(2)
    hd = 2 * DA_HEAD_DIM
    qw = nsub * tile
    t0 = nsub * j

    sub = 2 * tile
    qt = qt_ref[...]
    row = lax.broadcasted_iota(jnp.int32, (hd, tile), 0)
    zero = jnp.zeros((hd, tile), qt.dtype)
    for a in range(nsub):
        qa = qt[:, a * tile:(a + 1) * tile]
        qz_ref[:, a * sub:a * sub + tile] = jnp.where(row < DA_HEAD_DIM, qa, zero)
        qz_ref[:, a * sub + tile:(a + 1) * sub] = jnp.where(row >= DA_HEAD_DIM, qa, zero)

    def tile_off(t):
        return pl.multiple_of(jnp.maximum(t, 0) * tile, tile)

    def bias_block(block):
        b = bias_ref[pl.ds(pl.multiple_of(block * tile, tile), tile), :]
        return jnp.concatenate([b, b], axis=1)

    def bias_cols(t):
        parts = []
        for a in range(nsub):
            rel = t - (t0 + a)
            parts.append(bias_block(jnp.where(
                rel == 0, BIAS_DIAG, jnp.where(rel == -1, BIAS_PREV, jnp.where(rel < -1, BIAS_NONE, BIAS_MASKED)))))
        return jnp.concatenate(parts, axis=1)

    def scores(off, first_sub=0, kinds=None):
        s = jnp.dot(k_ref[pl.ds(off, tile), :], qz_ref[:, first_sub * sub:], preferred_element_type=F32)
        if kinds is None:
            return s
        pieces = [s[:, i * sub:(i + 1) * sub] for i in range(len(kinds))]
        return jnp.concatenate([x if kind is None else x + bias_block(kind) for x, kind in zip(pieces, kinds)], axis=1)

    def values(off, n_keys, p):
        return jnp.dot(vt_ref[:, pl.ds(off, n_keys)], p, preferred_element_type=F32)

    def by_sublane(s):
        return s.reshape(tile // SUBLANES, SUBLANES, s.shape[1])

    def exact_tile(off, bias, first):
        s = scores(off) + bias
        c = jnp.max(s, axis=0, keepdims=True)
        if first:
            m_new = c
        else:
            m_old = m_ref[...]
            m_new = jnp.maximum(m_old, c)
            alpha = jnp.exp2(m_old - m_new)
        p = jnp.exp2(s - m_new)
        rows = jnp.sum(by_sublane(p), axis=0)
        pv = values(off, tile, p.astype(BF16))
        acc_ref[...] = pv if first else alpha * acc_ref[...] + pv
        l_ref[...] = rows if first else alpha * l_ref[...] + rows
        m_ref[...] = m_new

    def diagonal_block():
        acc = lsum = xmax = None
        for c in reversed(range(nsub)):
            kinds = [BIAS_DIAG if a == c else BIAS_PREV if a == c + 1 else None for a in range(c, nsub)]
            s = scores(tile_off(t0 + c), c, kinds)
            p = jnp.exp2(s)
            part = jnp.max(by_sublane(s), axis=0)
            rows = jnp.sum(by_sublane(p), axis=0)
            pv = values(tile_off(t0 + c), tile, p.astype(BF16))
            if acc is None:
                acc, lsum, xmax = pv, rows, part
            else:
                acc = jnp.concatenate([pv[:, :sub], pv[:, sub:] + acc], axis=1)
                lsum = jnp.concatenate([rows[:, :sub], rows[:, sub:] + lsum], axis=1)
                xmax = jnp.concatenate([part[:, :sub], jnp.maximum(part[:, sub:], xmax)], axis=1)
        acc_ref[...], l_ref[...], x_ref[...] = acc, lsum, xmax

    def far_tiles(start, count, last_kinds=None):
        cpart = None
        lpart = None
        for i in range(count):
            s = scores(tile_off(start + i), 0, last_kinds if i == count - 1 else None)
            part = jnp.max(by_sublane(s), axis=0)
            cpart = part if cpart is None else jnp.maximum(cpart, part)
            p = jnp.exp2(s)
            rows = jnp.sum(by_sublane(p), axis=0)
            lpart = rows if lpart is None else lpart + rows
            p_ref[i * tile:(i + 1) * tile, :] = p.astype(BF16)
        n_keys = count * tile
        acc_ref[...] = acc_ref[...] + values(tile_off(start), n_keys, p_ref[:n_keys, :])
        l_ref[...] = l_ref[...] + lpart
        x_ref[...] = jnp.maximum(x_ref[...], cpart)

    def finish():
        lv = lam_ref[...]
        lam = (jnp.exp(jnp.sum(lv[0:1] * lv[1:2], axis=-1, keepdims=True))
               - jnp.exp(jnp.sum(lv[2:3] * lv[3:4], axis=-1, keepdims=True)) + lam_init)
        acc = acc_ref[...] * (1.0 / jnp.sum(l_ref[...], axis=0, keepdims=True))
        ot = jnp.concatenate([acc[:, a * sub:a * sub + tile] - lam * acc[:, a * sub + tile:(a + 1) * sub]
                              for a in range(nsub)], axis=1)
        o = _rms(ot, 0).T * g_ref[...] * (1.0 - lam_init)
        o_ref[...] = o.astype(o_ref.dtype)

    assert nsub % 2 == 0
    n_even = jnp.maximum(t0 - 2, 0)
    diagonal_block()

    @pl.when(j >= 1)
    def _():
        far_tiles(t0 - 2, 2, [BIAS_PREV] + [None] * (nsub - 1))

    def trip(u, carry):
        far_tiles(u * ATTN_TRIP_TILES, ATTN_TRIP_TILES)
        return carry

    n_trips = n_even // ATTN_TRIP_TILES
    lax.fori_loop(0, n_trips, trip, 0)
    done = n_trips * ATTN_TRIP_TILES
    count = ATTN_TRIP_TILES // 2
    while count >= 2:
        take = (n_even - done) >= count

        @pl.when(take)
        def _(done=done, count=count):
            far_tiles(done, count)

        done = done + jnp.where(take, count, 0)
        count //= 2
    finish()

    col_max = jnp.max(x_ref[...], axis=0, keepdims=True)

    @pl.when(jnp.max(jnp.abs(col_max)) > SCORE_LIMIT)
    def _():
        exact_tile(tile_off(t0), bias_cols(t0), first=True)

        def body(u, carry):
            t = u + jnp.where(u >= t0, 1, 0)
            exact_tile(tile_off(t), bias_cols(t), first=False)
            return carry

        lax.fori_loop(0, t0 + nsub - 1, body, 0)
        finish()


def _diff_attention(lam_vecs, qvt, kdn, bias, g_row, batch, seq, lam_init):
    tile = ATTN_TILE
    qw = ATTN_QTILES * tile
    assert seq % qw == 0
    nqt = seq // qw
    hd = 2 * DA_HEAD_DIM
    return pl.pallas_call(
        functools.partial(_attn_body, tile=tile, nsub=ATTN_QTILES, lam_init=lam_init),
        grid=(batch, DA_HEADS, nqt),
        in_specs=[pl.BlockSpec((4, DA_HEAD_DIM), lambda b, h, i: (0, 0)),
                  pl.BlockSpec((hd, qw), lambda b, h, i: (h, b * nqt + i)),
                  pl.BlockSpec((seq, hd), lambda b, h, i: (b, h)),
                  pl.BlockSpec((hd, seq), lambda b, h, i: (DA_HEADS + h, b)),
                  pl.BlockSpec((None,) + bias.shape[1:], lambda b, h, i: (h, 0, 0)),
                  pl.BlockSpec((1, hd), lambda b, h, i: (0, 0))],
        out_specs=pl.BlockSpec((qw, hd), lambda b, h, i: (b * nqt + i, h)),
        out_shape=jax.ShapeDtypeStruct((batch * seq, DA_HEADS * hd), BF16),
        scratch_shapes=[pltpu.VMEM((hd, 2 * qw), BF16),
                        pltpu.VMEM((1, 2 * qw), F32),
                        pltpu.VMEM((SUBLANES, 2 * qw), F32),
                        pltpu.VMEM((SUBLANES, 2 * qw), F32),
                        pltpu.VMEM((hd, 2 * qw), F32),
                        pltpu.VMEM((ATTN_TRIP_TILES * tile, 2 * qw), BF16)],
        compiler_params=_params("parallel", "parallel", "arbitrary"),
        name="diff_attn",
    )(lam_vecs, qvt, kdn, qvt, bias, g_row)


def _shift_rows(x, prev, s):
    if s == 0:
        return x
    rolled = pltpu.roll(x, s, 0)
    prolled = pltpu.roll(prev, s, 0)
    r8 = lax.broadcasted_iota(jnp.int32, prev.shape, 0)
    top = jnp.where(r8 < s, prolled, rolled[:SUBLANES])
    return jnp.concatenate([top, rolled[SUBLANES:]], axis=0)


def _dn_proj_body(x_ref, g_ref, w_ref, cw_ref, o_ref, xq_ref, xk_ref, xv_ref, tail_ref, *, tiles_per_seq):
    assert DN_CONV == 4
    width = DN_HEADS * DN_DIM
    rows = x_ref.shape[0]

    @pl.when(pl.program_id(0) % tiles_per_seq == 0)
    def _():
        for x_scr in (xq_ref, xk_ref, xv_ref):
            x_scr[:SUBLANES, :] = jnp.zeros((SUBLANES, width), F32)

    @pl.when(pl.program_id(0) % tiles_per_seq != 0)
    def _():
        for which, x_scr in enumerate((xq_ref, xk_ref, xv_ref)):
            x_scr[:SUBLANES, :] = tail_ref[which]

    n = _normed(x_ref, g_ref)

    def project(group):
        return jnp.dot(n, w_ref[:, group * width:(group + 1) * width], preferred_element_type=F32)

    for which, x_scr in enumerate((xq_ref, xk_ref, xv_ref)):
        xp = project(1 + which)
        x_scr[SUBLANES:, :] = xp
        tail_ref[which] = xp[rows - SUBLANES:]
    for group in (0, 4):
        o_ref[:, group * width:(group + 1) * width] = project(group).astype(o_ref.dtype)
    for which, x_scr in enumerate((xq_ref, xk_ref, xv_ref)):
        for h in range(DN_HEADS):
            cols = slice(h * DN_DIM, (h + 1) * DN_DIM)
            w = 0.5 * cw_ref[:, which * width + h * DN_DIM:which * width + (h + 1) * DN_DIM]
            xe = x_scr[:, cols]
            xe1 = pltpu.roll(xe, 1, 0)
            u = w[1:2] * xe + w[0:1] * xe1
            y = w[3:4] * xe[SUBLANES:] + w[2:3] * xe1[SUBLANES:] + pltpu.roll(u, 2, 0)[SUBLANES:]
            y = y + y * jnp.tanh(y)
            if which < 2:
                inv = lax.rsqrt(jnp.sum(y * y, axis=-1, keepdims=True) + EPS)
                y = y * (inv * (DN_DIM ** -0.5) if which == 0 else inv)
            c0 = (1 + which) * width + h * DN_DIM
            o_ref[:, c0:c0 + DN_DIM] = y.astype(o_ref.dtype)


def _dn_proj(x2d, g_row, w_kdn, conv_w, seq, tm):
    t, d = x2d.shape
    width = DN_HEADS * DN_DIM
    x_scr = pltpu.VMEM((SUBLANES + tm, width), F32)
    return pl.pallas_call(
        functools.partial(_dn_proj_body, tiles_per_seq=seq // tm),
        grid=(t // tm,),
        in_specs=[pl.BlockSpec((tm, d), lambda i: (i, 0)),
                  pl.BlockSpec((1, d), lambda i: (0, 0)),
                  pl.BlockSpec(w_kdn.shape, lambda i: (0, 0)),
                  pl.BlockSpec(conv_w.shape, lambda i: (0, 0))],
        out_specs=pl.BlockSpec((tm, w_kdn.shape[1]), lambda i: (i, 0)),
        out_shape=jax.ShapeDtypeStruct((t, w_kdn.shape[1]), BF16),
        scratch_shapes=[x_scr, x_scr, x_scr, pltpu.VMEM((3, SUBLANES, width), F32)],
        compiler_params=_params("arbitrary"),
        name="dn_proj",
    )(x2d, g_row, w_kdn, conv_w)


def _gdn_body(q_ref, k_ref, v_ref, z_ref, ba_ref, alog_ref, dtb_ref, gn_ref, o_ref, state_ref, *, npairs):
    p = DN_PAIR
    c = DN_CHUNK
    heads = range(DN_HEADS)
    head_cols = [slice(h * DN_DIM, (h + 1) * DN_DIM) for h in heads]

    @pl.when(pl.program_id(1) == 0)
    def _():
        state_ref[...] = jnp.zeros(state_ref.shape, F32)

    ri = lax.broadcasted_iota(jnp.int32, (p, p), 0)
    ci = lax.broadcasted_iota(jnp.int32, (p, p), 1)
    same = (ri // c) == (ci // c)
    causal = same & (ri >= ci)
    strict = same & (ri > ci)
    cum01 = jnp.where(causal, 1.0, 0.0).astype(BF16)

    def lane(x, j):
        return jnp.broadcast_to(x[:, j:j + 1], (x.shape[0], DN_DIM))

    first_chunk_lanes = lax.broadcasted_iota(jnp.int32, (c, p), 1) < c

    def side_by_side(bd):
        return jnp.where(first_chunk_lanes, bd[:c], bd[c:])

    def block_diagonal(ss):
        return jnp.concatenate([jnp.where(first_chunk_lanes, ss, 0.0), jnp.where(first_chunk_lanes, 0.0, ss)], axis=0)

    def prepare(pr):
        tok = slice(pr * p, (pr + 1) * p)
        ba = ba_ref[tok, :]
        beta_all = _sigmoid(ba)
        xs = ba + dtb_ref[...]
        softplus = jnp.maximum(xs, 0.0) + jnp.log(1.0 + jnp.exp(-jnp.abs(xs)))
        g_all = -jnp.exp(alog_ref[...]) * softplus
        gc_all = _mm_exact_rhs(cum01, g_all)
        glast_all = jnp.concatenate(
            [jnp.broadcast_to(gc_all[c - 1:c], (c, LANES)), jnp.broadcast_to(gc_all[p - 1:p], (c, LANES))], axis=0)
        tail_all = jnp.exp(glast_all - gc_all)
        dl_all = jnp.exp(glast_all)

        q = [q_ref[tok, cols].astype(F32) for cols in head_cols]
        k = [k_ref[tok, cols].astype(F32) for cols in head_cols]
        v = [v_ref[tok, cols].astype(F32) for cols in head_cols]
        gc = [lane(gc_all, DN_HEADS + h) for h in heads]
        decay = [jnp.exp(jnp.where(causal, x - x.T, MASK_VALUE)) for x in gc]
        exp_g = [jnp.exp(x) for x in gc]
        beta = [lane(beta_all, h) for h in heads]
        kb = [k[h] * beta[h] for h in heads]
        vb = [v[h] * beta[h] for h in heads]
        kq = [_mm_nt(jnp.concatenate([kb[h], q[h]], axis=0), k[h]) for h in heads]
        low = [jnp.where(strict, kq[h][:p] * decay[h], 0.0) for h in heads]
        intra = [jnp.where(causal, kq[h][p:] * decay[h], 0.0) for h in heads]
        y = [-side_by_side(x) for x in low]
        pw = [_mm(side_by_side(x), x) for x in low]
        for it in range(5):
            pw_bd = [block_diagonal(x) for x in pw]
            if it < 4:
                both = [_mm(jnp.concatenate([y[h], pw[h]], axis=0), pw_bd[h]) for h in heads]
                y = [y[h] + pw[h] + both[h][:c] for h in heads]
                pw = [both[h][c:] for h in heads]
            else:
                y = [y[h] + pw[h] + _mm(y[h], pw_bd[h]) for h in heads]
        rhs = [jnp.concatenate([vb[h], kb[h] * exp_g[h]], axis=1) for h in heads]
        sol = [rhs[h] + _mm(block_diagonal(y[h]), rhs[h]) for h in heads]
        q_dec = [q[h] * exp_g[h] for h in heads]
        kt_t = [(k[h] * lane(tail_all, DN_HEADS + h)).T for h in heads]
        return sol, intra, q_dec, kt_t, dl_all

    prepared = [prepare(pr) for pr in range(npairs)]

    zeros = jnp.zeros((c, DN_DIM), F32)
    for pr in range(npairs):
        sol, intra, q_dec, kt_t, dl_all = prepared[pr]
        tok = slice(pr * p, (pr + 1) * p)
        outs = [[], []]
        for ch in range(2):
            rows = slice(ch * c, (ch + 1) * c)
            st = [state_ref[h] for h in heads]
            on_state = [_mm(jnp.concatenate([sol[h][rows, DN_DIM:], q_dec[h][rows]], axis=0), st[h]) for h in heads]
            v_new = [sol[h][rows, :DN_DIM] - on_state[h][:c] for h in heads]
            v_pad = [jnp.concatenate([x, zeros] if ch == 0 else [zeros, x], axis=0) for x in v_new]
            on_v = [_mm(jnp.concatenate([intra[h][rows], kt_t[h]], axis=0), v_pad[h]) for h in heads]
            outs[ch] = [on_state[h][c:] + on_v[h][:c] for h in heads]
            for h in heads:
                dl = jnp.broadcast_to(dl_all[ch * c:ch * c + 1, DN_HEADS + h:DN_HEADS + h + 1], (1, DN_DIM))
                state_ref[h] = st[h] * dl + on_v[h][c:]
        for h in heads:
            o = jnp.concatenate([outs[0][h], outs[1][h]], axis=0)
            yb = _rms(o, -1) * gn_ref[...] * _silu(z_ref[tok, head_cols[h]].astype(F32))
            o_ref[tok, head_cols[h]] = yb.astype(o_ref.dtype)


def _gdn(kdn, ba, alog_row, dtb_row, gn_row, batch, seq):
    t = batch * seq
    p = GDN_STEP_PAIRS * DN_PAIR
    nt = seq // p
    width = DN_HEADS * DN_DIM
    tok = lambda col: pl.BlockSpec((p, width), lambda b, i: (b * nt + i, col))
    row = pl.BlockSpec((1, LANES), lambda b, i: (0, 0))
    return pl.pallas_call(
        functools.partial(_gdn_body, npairs=GDN_STEP_PAIRS),
        grid=(batch, nt),
        in_specs=[tok(1), tok(2), tok(3), tok(4),
                  pl.BlockSpec((p, LANES), lambda b, i: (b * nt + i, 0)),
                  row, row, row],
        out_specs=pl.BlockSpec((p, width), lambda b, i: (b * nt + i, 0)),
        out_shape=jax.ShapeDtypeStruct((t, width), BF16),
        scratch_shapes=[pltpu.VMEM((DN_HEADS, DN_DIM, DN_DIM), F32)],
        compiler_params=_params("parallel", "arbitrary"),
        name="gdn",
    )(kdn, kdn, kdn, kdn, ba, alog_row, dtb_row, gn_row)


def _merge_body(x_ref, cq_ref, g0_ref, g1_ref, g2_ref, ya_ref, yb_ref, mkt_ref, mv_ref,
                wb_ref, wo_ref, ln_ref, o_ref):
    cq = cq_ref[...]
    heads = []
    for h in range(CA_HEADS):
        qh = cq[:, h * CA_HEAD_DIM:(h + 1) * CA_HEAD_DIM]
        s = jnp.dot(qh, mkt_ref[h], preferred_element_type=F32) * (CA_HEAD_DIM ** -0.5)
        e = jnp.exp(s - jnp.max(s, axis=-1, keepdims=True))
        pr = e / jnp.sum(e, axis=-1, keepdims=True)
        heads.append(jnp.dot(pr.astype(BF16), mv_ref[h], preferred_element_type=F32))
    yc = jnp.concatenate(heads, axis=1)

    mixed = _sigmoid(g0_ref[...].astype(F32)) * jnp.dot(ya_ref[...], wb_ref[0], preferred_element_type=F32)
    mixed = mixed + _sigmoid(g1_ref[...].astype(F32)) * jnp.dot(yb_ref[...], wb_ref[1], preferred_element_type=F32)
    mixed = mixed + _sigmoid(g2_ref[...].astype(F32)) * _mm(yc, wb_ref[2])
    out = _mm(mixed, wo_ref[...])
    o_ref[...] = x_ref[...] + _rms(out, -1) * ln_ref[...]


def _merge(x2d, cg, ya, yb, mkt, mv, wb, wo, ln_row, batch, seq, tm):
    t, d = x2d.shape
    nt = seq // tm
    tok = lambda col: pl.BlockSpec((tm, d), lambda i: (i, col))
    mem = pl.BlockSpec((None, CA_HEADS, CA_HEAD_DIM, CA_HEAD_DIM), lambda i: (i // nt, 0, 0, 0))
    return pl.pallas_call(
        _merge_body,
        grid=(t // tm,),
        in_specs=[tok(0), tok(0), tok(1), tok(2), tok(3), tok(0), tok(0), mem, mem,
                  pl.BlockSpec((3, d, d), lambda i: (0, 0, 0)),
                  pl.BlockSpec((d, d), lambda i: (0, 0)),
                  pl.BlockSpec((1, d), lambda i: (0, 0))],
        out_specs=tok(0),
        out_shape=jax.ShapeDtypeStruct((t, d), F32),
        compiler_params=_params("parallel"),
        name="merge",
    )(x2d, cg, cg, cg, cg, ya, yb, mkt, mv, wb, wo, ln_row)


def _ffn_body(h_ref, pre_ref, post_ref, w1_ref, w2_ref, o_ref, *, chunk):
    hh = h_ref[...]
    n = (_rms(hh, -1) * pre_ref[...]).astype(BF16)
    d_ff = w1_ref.shape[1]
    f = jnp.zeros(hh.shape, F32)
    for c0 in range(0, d_ff, chunk):
        a = jnp.dot(n, w1_ref[:, c0:c0 + chunk], preferred_element_type=F32)
        a = jnp.square(jnp.maximum(a, 0.0))
        f = f + jnp.dot(a.astype(BF16), w2_ref[c0:c0 + chunk, :], preferred_element_type=F32)
    o_ref[...] = hh + _rms(f, -1) * post_ref[...]


def _ffn(h2d, pre_row, post_row, w1, w2, tm):
    t, d = h2d.shape
    d_ff = w1.shape[1]
    return pl.pallas_call(
        functools.partial(_ffn_body, chunk=1024),
        grid=(t // tm,),
        in_specs=[pl.BlockSpec((tm, d), lambda i: (i, 0)),
                  pl.BlockSpec((1, d), lambda i: (0, 0)),
                  pl.BlockSpec((1, d), lambda i: (0, 0)),
                  pl.BlockSpec((d, d_ff), lambda i: (0, 0)),
                  pl.BlockSpec((d_ff, d), lambda i: (0, 0))],
        out_specs=pl.BlockSpec((tm, d), lambda i: (i, 0)),
        out_shape=jax.ShapeDtypeStruct((t, d), F32),
        compiler_params=_params("parallel"),
        name="ffn",
    )(h2d, pre_row, post_row, w1, w2)


def _pad_lanes(vec, offset):
    return jnp.zeros((1, LANES), F32).at[0, offset:offset + vec.shape[0]].set(vec.astype(F32))


def _layer(h2d, mem2d, rel_bias, w_in, conv_w, dn_a_log, dn_dt_bias, dn_norm_g, da_lambda,
           da_subln_g, mem_norm_g, w_mem_kv, w_branch, w_out, ln_mix_pre, ln_mix_post,
           ln_ff_pre, ln_ff_post, w_ff1, w_ff2, batch, seq, mem_len, layer):
    d = h2d.shape[1]
    lam_init = 0.8 - 0.6 * math.exp(-0.3 * layer)
    tm = min(512, seq)
    row = lambda v: v.reshape(1, -1).astype(F32)

    small0 = 7 * d
    rest0 = small0 + 2 * DN_HEADS
    w_q = w_in[:, :d] * (DA_HEAD_DIM ** -0.5 * LOG2E)
    w_qv_t = jnp.concatenate([w_q, w_in[:, 2 * d:3 * d]], axis=1).T.astype(BF16)
    w_kdn = jnp.concatenate([w_in[:, d:2 * d], w_in[:, 3 * d:small0]], axis=1).astype(BF16)
    w_small = jnp.zeros((d, LANES), F32).at[:, :2 * DN_HEADS].set(w_in[:, small0:rest0]).astype(BF16)
    w_rest = w_in[:, rest0:].astype(BF16)
    g_pre = row(ln_mix_pre)

    qvt = _norm_matmul_t(h2d, g_pre, w_qv_t, BF16, tm)
    kdn = _dn_proj(h2d, g_pre, w_kdn, conv_w.astype(F32), seq, min(256, seq))
    cg, ba = _norm_matmul(h2d, g_pre, [w_rest, w_small], [BF16, F32], tm)
    (mkv,) = _norm_matmul(mem2d, row(mem_norm_g), [w_mem_kv.astype(BF16)], [BF16], min(512, batch * mem_len))

    bias = _bias_table(rel_bias.astype(F32), ATTN_TILE)
    y_a = _diff_attention(da_lambda.astype(F32), qvt, kdn, bias, row(da_subln_g), batch, seq, lam_init)

    y_b = _gdn(kdn, ba, _pad_lanes(dn_a_log, DN_HEADS), _pad_lanes(dn_dt_bias, DN_HEADS),
               row(dn_norm_g), batch, seq)

    mkt = mkv[:, :d].reshape(batch, mem_len, CA_HEADS, CA_HEAD_DIM).transpose(0, 2, 3, 1)
    mv = mkv[:, d:].reshape(batch, mem_len, CA_HEADS, CA_HEAD_DIM).transpose(0, 2, 1, 3)
    h1 = _merge(h2d, cg, y_a, y_b, mkt, mv, w_branch.astype(BF16), w_out.astype(BF16),
                row(ln_mix_post), batch, seq, min(512, seq))

    return _ffn(h1, row(ln_ff_pre), row(ln_ff_post), w_ff1.astype(BF16), w_ff2.astype(BF16), min(512, seq))


def kernel(x, mem, rel_bias, w_in, conv_w, dn_a_log, dn_dt_bias, dn_norm_g, da_lambda, da_subln_g,
           mem_norm_g, w_mem_kv, w_branch, w_out, ln_mix_pre, ln_mix_post, ln_ff_pre, ln_ff_post,
           w_ff1, w_ff2):
    batch, seq, d = x.shape
    mem_len = mem.shape[1]
    assert seq % (ATTN_QTILES * ATTN_TILE) == 0 and seq % DN_PAIR == 0
    h2d = x.reshape(batch * seq, d)
    mem2d = mem.reshape(batch * mem_len, d)
    for layer in range(w_in.shape[0]):
        h2d = _layer(h2d, mem2d, rel_bias, w_in[layer], conv_w[layer], dn_a_log[layer], dn_dt_bias[layer],
                     dn_norm_g[layer], da_lambda[layer], da_subln_g[layer], mem_norm_g[layer],
                     w_mem_kv[layer], w_branch[layer], w_out[layer], ln_mix_pre[layer], ln_mix_post[layer],
                     ln_ff_pre[layer], ln_ff_post[layer], w_ff1[layer], w_ff2[layer],
                     batch, seq, mem_len, layer)
    return h2d.reshape(batch, seq, d)
```

```python
import functools
import math

import numpy as np
import jax
import jax.numpy as jnp
from jax import lax
from jax.experimental import pallas as pl
from jax.experimental.pallas import tpu as pltpu

F32 = jnp.float32
BF16 = jnp.bfloat16

EPS = 1e-6
MASK_VALUE = -1e30
LOG2E = math.log2(math.e)
LANES = 128
SUBLANES = 8
VMEM_LIMIT_BYTES = 56 * 1024 * 1024

DA_HEADS = 8
DA_HEAD_DIM = 64
DN_HEADS = 8
DN_DIM = 128
DN_CONV = 4
DN_CHUNK = 64
DN_PAIR = 2 * DN_CHUNK
CA_HEADS = 4
CA_HEAD_DIM = 256
REL_BUCKETS = 32
REL_MAX_EXACT = 16
REL_MAX_DIST = 128

GDN_STEP_PAIRS = 4
ATTN_TILE = 256
ATTN_QTILES = 4
ATTN_TRIP_TILES = 8
SCORE_LIMIT = 60.0


def _params(*sem):
    return pltpu.CompilerParams(dimension_semantics=sem, vmem_limit_bytes=VMEM_LIMIT_BYTES)


def _mm(a, b):
    return jnp.dot(a.astype(BF16), b.astype(BF16), preferred_element_type=F32)


def _mm_nt(a, b):
    return lax.dot_general(a.astype(BF16), b.astype(BF16), (((1,), (1,)), ((), ())),
                           preferred_element_type=F32)


def _mm_exact_rhs(m01, a):
    hi = a.astype(BF16)
    r1 = a - hi.astype(F32)
    mid = r1.astype(BF16)
    lo = (r1 - mid.astype(F32)).astype(BF16)
    dot = functools.partial(jnp.dot, preferred_element_type=F32)
    return dot(m01, hi) + dot(m01, mid) + dot(m01, lo)


def _rms(x, axis):
    return x * lax.rsqrt(jnp.mean(x * x, axis=axis, keepdims=True) + EPS)


def _sigmoid(x):
    return 1.0 / (1.0 + jnp.exp(-x))


def _silu(x):
    h = 0.5 * x
    return h + h * jnp.tanh(h)


def _normed(x_ref, g_ref):
    return (_rms(x_ref[...], -1) * g_ref[...]).astype(BF16)


def _norm_matmul_body(x_ref, g_ref, *refs, chunk):
    n_out = len(refs) // 2
    n = _normed(x_ref, g_ref)
    for w_ref, o_ref in zip(refs[:n_out], refs[n_out:]):
        cols = w_ref.shape[1]
        step = min(chunk, cols)
        for c0 in range(0, cols, step):
            o_ref[:, c0:c0 + step] = jnp.dot(n, w_ref[:, c0:c0 + step],
                                             preferred_element_type=F32).astype(o_ref.dtype)


def _norm_matmul(x2d, g_row, weights, out_dtypes, tm):
    t, d = x2d.shape
    return pl.pallas_call(
        functools.partial(_norm_matmul_body, chunk=1024),
        grid=(t // tm,),
        in_specs=[pl.BlockSpec((tm, d), lambda i: (i, 0)),
                  pl.BlockSpec((1, d), lambda i: (0, 0))]
                 + [pl.BlockSpec(w.shape, lambda i: (0, 0)) for w in weights],
        out_specs=[pl.BlockSpec((tm, w.shape[1]), lambda i: (i, 0)) for w in weights],
        out_shape=[jax.ShapeDtypeStruct((t, w.shape[1]), dt) for w, dt in zip(weights, out_dtypes)],
        compiler_params=_params("parallel"),
        name="norm_matmul",
    )(x2d, g_row, *weights)


def _norm_matmul_t_body(x_ref, g_ref, wt_ref, o_ref, *, chunk):
    n = _normed(x_ref, g_ref)
    for r0 in range(0, wt_ref.shape[0], chunk):
        o_ref[r0:r0 + chunk, :] = lax.dot_general(
            wt_ref[r0:r0 + chunk, :], n, (((1,), (1,)), ((), ())),
            preferred_element_type=F32).astype(o_ref.dtype)


def _norm_matmul_t(x2d, g_row, wt, out_dtype, tm):
    t, d = x2d.shape
    rows = wt.shape[0]
    return pl.pallas_call(
        functools.partial(_norm_matmul_t_body, chunk=512),
        grid=(t // tm,),
        in_specs=[pl.BlockSpec((tm, d), lambda i: (i, 0)),
                  pl.BlockSpec((1, d), lambda i: (0, 0)),
                  pl.BlockSpec((rows, d), lambda i: (0, 0))],
        out_specs=pl.BlockSpec((rows, tm), lambda i: (0, i)),
        out_shape=jax.ShapeDtypeStruct((rows, t), out_dtype),
        compiler_params=_params("parallel"),
        name="norm_matmul_t",
    )(x2d, g_row, wt)


BIAS_PREV, BIAS_DIAG, BIAS_MASKED, BIAS_NONE = 0, 1, 2, 3


def _bucket_table(tile):
    k = np.arange(tile)[:, None]
    q = np.arange(tile)[None, :]

    def bucket(dist):
        nf = np.maximum(dist, 1).astype(np.float64)
        large = REL_MAX_EXACT + np.trunc(
            np.log(nf / REL_MAX_EXACT) / math.log(REL_MAX_DIST / REL_MAX_EXACT)
            * (REL_BUCKETS - REL_MAX_EXACT)).astype(np.int64)
        large = np.minimum(large, REL_BUCKETS - 1)
        return np.where(dist < REL_MAX_EXACT, dist, large)

    prev = bucket(q - k + tile)
    diag = np.where(q >= k, bucket(np.maximum(q - k, 0)), -1)
    masked = np.full((tile, tile), -1)
    far = np.full((tile, tile), REL_BUCKETS - 1)
    return np.concatenate([prev, diag, masked, far], axis=0).astype(np.int32)


def _bias_table_body(rb_ref, bucket_ref, o_ref):
    h = pl.program_id(0)
    far = rb_ref[REL_BUCKETS - 1, h]
    bk = bucket_ref[...]
    out = jnp.full(bk.shape, MASK_VALUE, F32)
    for b in range(REL_BUCKETS):
        out = jnp.where(bk == b, (rb_ref[b, h] - far) * LOG2E, out)
    o_ref[...] = out


def _bias_table(rel_bias, tile):
    heads = rel_bias.shape[1]
    buckets = jnp.asarray(_bucket_table(tile))
    rows = buckets.shape[0]
    return pl.pallas_call(
        _bias_table_body,
        grid=(heads,),
        in_specs=[pl.BlockSpec(memory_space=pltpu.SMEM),
                  pl.BlockSpec((rows, tile), lambda h: (0, 0))],
        out_specs=pl.BlockSpec((None, rows, tile), lambda h: (h, 0, 0)),
        out_shape=jax.ShapeDtypeStruct((heads, rows, tile), F32),
        compiler_params=_params("arbitrary"),
        name="bias_table",
    )(rel_bias, buckets)


def _attn_body(lam_ref, qt_ref, k_ref, vt_ref, bias_ref, g_ref, o_ref,
               qz_ref, m_ref, x_ref, l_ref, acc_ref, p_ref, *, tile, nsub, lam_init):
    j = pl.program_id(2)
    hd = 2 * DA_HEAD_DIM
    qw = nsub * tile
    t0 = nsub * j

    sub = 2 * tile
    qt = qt_ref[...]
    row = lax.broadcasted_iota(jnp.int32, (hd, tile), 0)
    zero = jnp.zeros((hd, tile), qt.dtype)
    for a in range(nsub):
        qa = qt[:, a * tile:(a + 1) * tile]
        qz_ref[:, a * sub:a * sub + tile] = jnp.where(row < DA_HEAD_DIM, qa, zero)
        qz_ref[:, a * sub + tile:(a + 1) * sub] = jnp.where(row >= DA_HEAD_DIM, qa, zero)

    def tile_off(t):
        return pl.multiple_of(jnp.maximum(t, 0) * tile, tile)

    def bias_block(block):
        b = bias_ref[pl.ds(pl.multiple_of(block * tile, tile), tile), :]
        return jnp.concatenate([b, b], axis=1)

    def bias_cols(t):
        parts = []
        for a in range(nsub):
            rel = t - (t0 + a)
            parts.append(bias_block(jnp.where(
                rel == 0, BIAS_DIAG, jnp.where(rel == -1, BIAS_PREV, jnp.where(rel < -1, BIAS_NONE, BIAS_MASKED)))))
        return jnp.concatenate(parts, axis=1)

    def scores(off, first_sub=0, kinds=None):
        s = jnp.dot(k_ref[pl.ds(off, tile), :], qz_ref[:, first_sub * sub:], preferred_element_type=F32)
        if kinds is None:
            return s
        pieces = [s[:, i * sub:(i + 1) * sub] for i in range(len(kinds))]
        return jnp.concatenate([x if kind is None else x + bias_block(kind) for x, kind in zip(pieces, kinds)], axis=1)

    def values(off, n_keys, p):
        return jnp.dot(vt_ref[:, pl.ds(off, n_keys)], p, preferred_element_type=F32)

    def by_sublane(s):
        return s.reshape(tile // SUBLANES, SUBLANES, s.shape[1])

    def exact_tile(off, bias, first):
        s = scores(off) + bias
        c = jnp.max(s, axis=0, keepdims=True)
        if first:
            m_new = c
        else:
            m_old = m_ref[...]
            m_new = jnp.maximum(m_old, c)
            alpha = jnp.exp2(m_old - m_new)
        p = jnp.exp2(s - m_new)
        rows = jnp.sum(by_sublane(p), axis=0)
        pv = values(off, tile, p.astype(BF16))
        acc_ref[...] = pv if first else alpha * acc_ref[...] + pv
        l_ref[...] = rows if first else alpha * l_ref[...] + rows
        m_ref[...] = m_new

    def diagonal_block():
        acc = lsum = xmax = None
        for c in reversed(range(nsub)):
            kinds = [BIAS_DIAG if a == c else BIAS_PREV if a == c + 1 else None for a in range(c, nsub)]
            s = scores(tile_off(t0 + c), c, kinds)
            p = jnp.exp2(s)
            part = jnp.max(by_sublane(s), axis=0)
            rows = jnp.sum(by_sublane(p), axis=0)
            pv = values(tile_off(t0 + c), tile, p.astype(BF16))
            if acc is None:
                acc, lsum, xmax = pv, rows, part
            else:
                acc = jnp.concatenate([pv[:, :sub], pv[:, sub:] + acc], axis=1)
                lsum = jnp.concatenate([rows[:, :sub], rows[:, sub:] + lsum], axis=1)
                xmax = jnp.concatenate([part[:, :sub], jnp.maximum(part[:, sub:], xmax)], axis=1)
        acc_ref[...], l_ref[...], x_ref[...] = acc, lsum, xmax

    def far_tiles(start, count, last_kinds=None):
        cpart = None
        lpart = None
        for i in range(count):
            s = scores(tile_off(start + i), 0, last_kinds if i == count - 1 else None)
            part = jnp.max(by_sublane(s), axis=0)
            cpart = part if cpart is None else jnp.maximum(cpart, part)
            p = jnp.exp2(s)
            rows = jnp.sum(by_sublane(p), axis=0)
            lpart = rows if lpart is None else lpart + rows
            p_ref[i * tile:(i + 1) * tile, :] = p.astype(BF16)
        n_keys = count * tile
        acc_ref[...] = acc_ref[...] + values(tile_off(start), n_keys, p_ref[:n_keys, :])
        l_ref[...] = l_ref[...] + lpart
        x_ref[...] = jnp.maximum(x_ref[...], cpart)

    def finish():
        lv = lam_ref[...]
        lam = (jnp.exp(jnp.sum(lv[0:1] * lv[1:2], axis=-1, keepdims=True))
               - jnp.exp(jnp.sum(lv[2:3] * lv[3:4], axis=-1, keepdims=True)) + lam_init)
        acc = acc_ref[...] * (1.0 / jnp.sum(l_ref[...], axis=0, keepdims=True))
        ot = jnp.concatenate([acc[:, a * sub:a * sub + tile] - lam * acc[:, a * sub + tile:(a + 1) * sub]
                              for a in range(nsub)], axis=1)
        o = _rms(ot, 0).T * g_ref[...] * (1.0 - lam_init)
        o_ref[...] = o.astype(o_ref.dtype)

    assert nsub % 2 == 0
    n_even = jnp.maximum(t0 - 2, 0)
    diagonal_block()

    @pl.when(j >= 1)
    def _():
        far_tiles(t0 - 2, 2, [BIAS_PREV] + [None] * (nsub - 1))

    def trip(u, carry):
        far_tiles(u * ATTN_TRIP_TILES, ATTN_TRIP_TILES)
        return carry

    n_trips = n_even // ATTN_TRIP_TILES
    lax.fori_loop(0, n_trips, trip, 0)
    done = n_trips * ATTN_TRIP_TILES
    count = ATTN_TRIP_TILES // 2
    while count >= 2:
        take = (n_even - done) >= count

        @pl.when(take)
        def _(done=done, count=count):
            far_tiles(done, count)

        done = done + jnp.where(take, count, 0)
        count //= 2
    finish()

    col_max = jnp.max(x_ref[...], axis=0, keepdims=True)

    @pl.when(jnp.max(jnp.abs(col_max)) > SCORE_LIMIT)
    def _():
        exact_tile(tile_off(t0), bias_cols(t0), first=True)

        def body(u, carry):
            t = u + jnp.where(u >= t0, 1, 0)
            exact_tile(tile_off(t), bias_cols(t), first=False)
            return carry

        lax.fori_loop(0, t0 + nsub - 1, body, 0)
        finish()


def _diff_attention(lam_vecs, qvt, kdn, bias, g_row, batch, seq, lam_init):
    tile = ATTN_TILE
    qw = ATTN_QTILES * tile
    assert seq % qw == 0
    nqt = seq // qw
    hd = 2 * DA_HEAD_DIM
    return pl.pallas_call(
        functools.partial(_attn_body, tile=tile, nsub=ATTN_QTILES, lam_init=lam_init),
        grid=(batch, DA_HEADS, nqt),
        in_specs=[pl.BlockSpec((4, DA_HEAD_DIM), lambda b, h, i: (0, 0)),
                  pl.BlockSpec((hd, qw), lambda b, h, i: (h, b * nqt + i)),
                  pl.BlockSpec((seq, hd), lambda b, h, i: (b, h)),
                  pl.BlockSpec((hd, seq), lambda b, h, i: (DA_HEADS + h, b)),
                  pl.BlockSpec((None,) + bias.shape[1:], lambda b, h, i: (h, 0, 0)),
                  pl.BlockSpec((1, hd), lambda b, h, i: (0, 0))],
        out_specs=pl.BlockSpec((qw, hd), lambda b, h, i: (b * nqt + i, h)),
        out_shape=jax.ShapeDtypeStruct((batch * seq, DA_HEADS * hd), BF16),
        scratch_shapes=[pltpu.VMEM((hd, 2 * qw), BF16),
                        pltpu.VMEM((1, 2 * qw), F32),
                        pltpu.VMEM((SUBLANES, 2 * qw), F32),
                        pltpu.VMEM((SUBLANES, 2 * qw), F32),
                        pltpu.VMEM((hd, 2 * qw), F32),
                        pltpu.VMEM((ATTN_TRIP_TILES * tile, 2 * qw), BF16)],
        compiler_params=_params("parallel", "parallel", "arbitrary"),
        name="diff_attn",
    )(lam_vecs, qvt, kdn, qvt, bias, g_row)


def _shift_rows(x, prev, s):
    if s == 0:
        return x
    rolled = pltpu.roll(x, s, 0)
    prolled = pltpu.roll(prev, s, 0)
    r8 = lax.broadcasted_iota(jnp.int32, prev.shape, 0)
    top = jnp.where(r8 < s, prolled, rolled[:SUBLANES])
    return jnp.concatenate([top, rolled[SUBLANES:]], axis=0)


def _dn_proj_body(x_ref, g_ref, w_ref, cw_ref, o_ref, xq_ref, xk_ref, xv_ref, tail_ref, *, tiles_per_seq):
    assert DN_CONV == 4
    width = DN_HEADS * DN_DIM
    rows = x_ref.shape[0]

    @pl.when(pl.program_id(0) % tiles_per_seq == 0)
    def _():
        for x_scr in (xq_ref, xk_ref, xv_ref):
            x_scr[:SUBLANES, :] = jnp.zeros((SUBLANES, width), F32)

    @pl.when(pl.program_id(0) % tiles_per_seq != 0)
    def _():
        for which, x_scr in enumerate((xq_ref, xk_ref, xv_ref)):
            x_scr[:SUBLANES, :] = tail_ref[which]

    n = _normed(x_ref, g_ref)

    def project(group):
        return jnp.dot(n, w_ref[:, group * width:(group + 1) * width], preferred_element_type=F32)

    for which, x_scr in enumerate((xq_ref, xk_ref, xv_ref)):
        xp = project(1 + which)
        x_scr[SUBLANES:, :] = xp
        tail_ref[which] = xp[rows - SUBLANES:]
    for group in (0, 4):
        o_ref[:, group * width:(group + 1) * width] = project(group).astype(o_ref.dtype)
    for which, x_scr in enumerate((xq_ref, xk_ref, xv_ref)):
        for h in range(DN_HEADS):
            cols = slice(h * DN_DIM, (h + 1) * DN_DIM)
            w = 0.5 * cw_ref[:, which * width + h * DN_DIM:which * width + (h + 1) * DN_DIM]
            xe = x_scr[:, cols]
            xe1 = pltpu.roll(xe, 1, 0)
            u = w[1:2] * xe + w[0:1] * xe1
            y = w[3:4] * xe[SUBLANES:] + w[2:3] * xe1[SUBLANES:] + pltpu.roll(u, 2, 0)[SUBLANES:]
            y = y + y * jnp.tanh(y)
            if which < 2:
                inv = lax.rsqrt(jnp.sum(y * y, axis=-1, keepdims=True) + EPS)
                y = y * (inv * (DN_DIM ** -0.5) if which == 0 else inv)
            c0 = (1 + which) * width + h * DN_DIM
            o_ref[:, c0:c0 + DN_DIM] = y.astype(o_ref.dtype)


def _dn_proj(x2d, g_row, w_kdn, conv_w, seq, tm):
    t, d = x2d.shape
    width = DN_HEADS * DN_DIM
    x_scr = pltpu.VMEM((SUBLANES + tm, width), F32)
    return pl.pallas_call(
        functools.partial(_dn_proj_body, tiles_per_seq=seq // tm),
        grid=(t // tm,),
        in_specs=[pl.BlockSpec((tm, d), lambda i: (i, 0)),
                  pl.BlockSpec((1, d), lambda i: (0, 0)),
                  pl.BlockSpec(w_kdn.shape, lambda i: (0, 0)),
                  pl.BlockSpec(conv_w.shape, lambda i: (0, 0))],
        out_specs=pl.BlockSpec((tm, w_kdn.shape[1]), lambda i: (i, 0)),
        out_shape=jax.ShapeDtypeStruct((t, w_kdn.shape[1]), BF16),
        scratch_shapes=[x_scr, x_scr, x_scr, pltpu.VMEM((3, SUBLANES, width), F32)],
        compiler_params=_params("arbitrary"),
        name="dn_proj",
    )(x2d, g_row, w_kdn, conv_w)


def _gdn_body(q_ref, k_ref, v_ref, z_ref, ba_ref, alog_ref, dtb_ref, gn_ref, o_ref, state_ref, *, npairs):
    p = DN_PAIR
    c = DN_CHUNK
    heads = range(DN_HEADS)
    head_cols = [slice(h * DN_DIM, (h + 1) * DN_DIM) for h in heads]

    @pl.when(pl.program_id(1) == 0)
    def _():
        state_ref[...] = jnp.zeros(state_ref.shape, F32)

    ri = lax.broadcasted_iota(jnp.int32, (p, p), 0)
    ci = lax.broadcasted_iota(jnp.int32, (p, p), 1)
    same = (ri // c) == (ci // c)
    causal = same & (ri >= ci)
    strict = same & (ri > ci)
    cum01 = jnp.where(causal, 1.0, 0.0).astype(BF16)

    def lane(x, j):
        return jnp.broadcast_to(x[:, j:j + 1], (x.shape[0], DN_DIM))

    first_chunk_lanes = lax.broadcasted_iota(jnp.int32, (c, p), 1) < c

    def side_by_side(bd):
        return jnp.where(first_chunk_lanes, bd[:c], bd[c:])

    def block_diagonal(ss):
        return jnp.concatenate([jnp.where(first_chunk_lanes, ss, 0.0), jnp.where(first_chunk_lanes, 0.0, ss)], axis=0)

    def prepare(pr):
        tok = slice(pr * p, (pr + 1) * p)
        ba = ba_ref[tok, :]
        beta_all = _sigmoid(ba)
        xs = ba + dtb_ref[...]
        softplus = jnp.maximum(xs, 0.0) + jnp.log(1.0 + jnp.exp(-jnp.abs(xs)))
        g_all = -jnp.exp(alog_ref[...]) * softplus
        gc_all = _mm_exact_rhs(cum01, g_all)
        glast_all = jnp.concatenate(
            [jnp.broadcast_to(gc_all[c - 1:c], (c, LANES)), jnp.broadcast_to(gc_all[p - 1:p], (c, LANES))], axis=0)
        tail_all = jnp.exp(glast_all - gc_all)
        dl_all = jnp.exp(glast_all)

        q = [q_ref[tok, cols].astype(F32) for cols in head_cols]
        k = [k_ref[tok, cols].astype(F32) for cols in head_cols]
        v = [v_ref[tok, cols].astype(F32) for cols in head_cols]
        gc = [lane(gc_all, DN_HEADS + h) for h in heads]
        gc_all_t = gc_all.T
        gr = [jnp.broadcast_to(gc_all_t[DN_HEADS + h:DN_HEADS + h + 1], (p, p)) for h in heads]
        decay = [jnp.exp(jnp.where(causal, gc[h] - gr[h], MASK_VALUE)) for h in heads]
        exp_g = [jnp.exp(x) for x in gc]
        beta = [lane(beta_all, h) for h in heads]
        kb = [k[h] * beta[h] for h in heads]
        vb = [v[h] * beta[h] for h in heads]
        kq = [_mm_nt(jnp.concatenate([kb[h], q[h]], axis=0), k[h]) for h in heads]
        low = [jnp.where(strict, kq[h][:p] * decay[h], 0.0) for h in heads]
        intra = [jnp.where(causal, kq[h][p:] * decay[h], 0.0) for h in heads]
        y = [-side_by_side(x) for x in low]
        pw = [_mm(side_by_side(x), x) for x in low]
        for it in range(5):
            pw_bd = [block_diagonal(x) for x in pw]
            if it < 4:
                both = [_mm(jnp.concatenate([y[h], pw[h]], axis=0), pw_bd[h]) for h in heads]
                y = [y[h] + pw[h] + both[h][:c] for h in heads]
                pw = [both[h][c:] for h in heads]
            else:
                y = [y[h] + pw[h] + _mm(y[h], pw_bd[h]) for h in heads]
        rhs = [jnp.concatenate([vb[h], kb[h] * exp_g[h]], axis=1) for h in heads]
        sol = [rhs[h] + _mm(block_diagonal(y[h]), rhs[h]) for h in heads]
        q_dec = [q[h] * exp_g[h] for h in heads]
        tail_all_t = tail_all.T
        kt_t = [k[h].T * jnp.broadcast_to(tail_all_t[DN_HEADS + h:DN_HEADS + h + 1], (DN_DIM, p))
                for h in heads]
        return sol, intra, q_dec, kt_t, dl_all

    prepared = [prepare(pr) for pr in range(npairs)]

    zeros = jnp.zeros((c, DN_DIM), F32)
    for pr in range(npairs):
        sol, intra, q_dec, kt_t, dl_all = prepared[pr]
        tok = slice(pr * p, (pr + 1) * p)
        outs = [[], []]
        for ch in range(2):
            rows = slice(ch * c, (ch + 1) * c)
            st = [state_ref[h] for h in heads]
            on_state = [_mm(jnp.concatenate([sol[h][rows, DN_DIM:], q_dec[h][rows]], axis=0), st[h]) for h in heads]
            v_new = [sol[h][rows, :DN_DIM] - on_state[h][:c] for h in heads]
            v_pad = [jnp.concatenate([x, zeros] if ch == 0 else [zeros, x], axis=0) for x in v_new]
            on_v = [_mm(jnp.concatenate([intra[h][rows], kt_t[h]], axis=0), v_pad[h]) for h in heads]
            outs[ch] = [on_state[h][c:] + on_v[h][:c] for h in heads]
            for h in heads:
                dl = jnp.broadcast_to(dl_all[ch * c:ch * c + 1, DN_HEADS + h:DN_HEADS + h + 1], (1, DN_DIM))
                state_ref[h] = st[h] * dl + on_v[h][c:]
        for h in heads:
            o = jnp.concatenate([outs[0][h], outs[1][h]], axis=0)
            yb = _rms(o, -1) * gn_ref[...] * _silu(z_ref[tok, head_cols[h]].astype(F32))
            o_ref[tok, head_cols[h]] = yb.astype(o_ref.dtype)


def _gdn(kdn, ba, alog_row, dtb_row, gn_row, batch, seq):
    t = batch * seq
    p = GDN_STEP_PAIRS * DN_PAIR
    nt = seq // p
    width = DN_HEADS * DN_DIM
    tok = lambda col: pl.BlockSpec((p, width), lambda b, i: (b * nt + i, col))
    row = pl.BlockSpec((1, LANES), lambda b, i: (0, 0))
    return pl.pallas_call(
        functools.partial(_gdn_body, npairs=GDN_STEP_PAIRS),
        grid=(batch, nt),
        in_specs=[tok(1), tok(2), tok(3), tok(4),
                  pl.BlockSpec((p, LANES), lambda b, i: (b * nt + i, 0)),
                  row, row, row],
        out_specs=pl.BlockSpec((p, width), lambda b, i: (b * nt + i, 0)),
        out_shape=jax.ShapeDtypeStruct((t, width), BF16),
        scratch_shapes=[pltpu.VMEM((DN_HEADS, DN_DIM, DN_DIM), F32)],
        compiler_params=_params("parallel", "arbitrary"),
        name="gdn",
    )(kdn, kdn, kdn, kdn, ba, alog_row, dtb_row, gn_row)


def _merge_body(x_ref, cq_ref, g0_ref, g1_ref, g2_ref, ya_ref, yb_ref, mkt_ref, mv_ref,
                wb_ref, wo_ref, ln_ref, o_ref):
    cq = cq_ref[...]
    heads = []
    for h in range(CA_HEADS):
        qh = cq[:, h * CA_HEAD_DIM:(h + 1) * CA_HEAD_DIM]
        s = jnp.dot(qh, mkt_ref[h], preferred_element_type=F32) * (CA_HEAD_DIM ** -0.5)
        e = jnp.exp(s - jnp.max(s, axis=-1, keepdims=True))
        pr = e / jnp.sum(e, axis=-1, keepdims=True)
        heads.append(jnp.dot(pr.astype(BF16), mv_ref[h], preferred_element_type=F32))
    yc = jnp.concatenate(heads, axis=1)

    mixed = _sigmoid(g0_ref[...].astype(F32)) * jnp.dot(ya_ref[...], wb_ref[0], preferred_element_type=F32)
    mixed = mixed + _sigmoid(g1_ref[...].astype(F32)) * jnp.dot(yb_ref[...], wb_ref[1], preferred_element_type=F32)
    mixed = mixed + _sigmoid(g2_ref[...].astype(F32)) * _mm(yc, wb_ref[2])
    out = _mm(mixed, wo_ref[...])
    o_ref[...] = x_ref[...] + _rms(out, -1) * ln_ref[...]


def _merge(x2d, cg, ya, yb, mkt, mv, wb, wo, ln_row, batch, seq, tm):
    t, d = x2d.shape
    nt = seq // tm
    tok = lambda col: pl.BlockSpec((tm, d), lambda i: (i, col))
    mem = pl.BlockSpec((None, CA_HEADS, CA_HEAD_DIM, CA_HEAD_DIM), lambda i: (i // nt, 0, 0, 0))
    return pl.pallas_call(
        _merge_body,
        grid=(t // tm,),
        in_specs=[tok(0), tok(0), tok(1), tok(2), tok(3), tok(0), tok(0), mem, mem,
                  pl.BlockSpec((3, d, d), lambda i: (0, 0, 0)),
                  pl.BlockSpec((d, d), lambda i: (0, 0)),
                  pl.BlockSpec((1, d), lambda i: (0, 0))],
        out_specs=tok(0),
        out_shape=jax.ShapeDtypeStruct((t, d), F32),
        compiler_params=_params("parallel"),
        name="merge",
    )(x2d, cg, cg, cg, cg, ya, yb, mkt, mv, wb, wo, ln_row)


def _ffn_body(h_ref, pre_ref, post_ref, w1_ref, w2_ref, o_ref, *, chunk):
    hh = h_ref[...]
    n = (_rms(hh, -1) * pre_ref[...]).astype(BF16)
    d_ff = w1_ref.shape[1]
    f = jnp.zeros(hh.shape, F32)
    for c0 in range(0, d_ff, chunk):
        a = jnp.dot(n, w1_ref[:, c0:c0 + chunk], preferred_element_type=F32)
        a = jnp.square(jnp.maximum(a, 0.0))
        f = f + jnp.dot(a.astype(BF16), w2_ref[c0:c0 + chunk, :], preferred_element_type=F32)
    o_ref[...] = hh + _rms(f, -1) * post_ref[...]


def _ffn(h2d, pre_row, post_row, w1, w2, tm):
    t, d = h2d.shape
    d_ff = w1.shape[1]
    return pl.pallas_call(
        functools.partial(_ffn_body, chunk=1024),
        grid=(t // tm,),
        in_specs=[pl.BlockSpec((tm, d), lambda i: (i, 0)),
                  pl.BlockSpec((1, d), lambda i: (0, 0)),
                  pl.BlockSpec((1, d), lambda i: (0, 0)),
                  pl.BlockSpec((d, d_ff), lambda i: (0, 0)),
                  pl.BlockSpec((d_ff, d), lambda i: (0, 0))],
        out_specs=pl.BlockSpec((tm, d), lambda i: (i, 0)),
        out_shape=jax.ShapeDtypeStruct((t, d), F32),
        compiler_params=_params("parallel"),
        name="ffn",
    )(h2d, pre_row, post_row, w1, w2)


def _pad_lanes(vec, offset):
    return jnp.zeros((1, LANES), F32).at[0, offset:offset + vec.shape[0]].set(vec.astype(F32))


def _layer(h2d, mem2d, rel_bias, w_in, conv_w, dn_a_log, dn_dt_bias, dn_norm_g, da_lambda,
           da_subln_g, mem_norm_g, w_mem_kv, w_branch, w_out, ln_mix_pre, ln_mix_post,
           ln_ff_pre, ln_ff_post, w_ff1, w_ff2, batch, seq, mem_len, layer):
    d = h2d.shape[1]
    lam_init = 0.8 - 0.6 * math.exp(-0.3 * layer)
    tm = min(512, seq)
    row = lambda v: v.reshape(1, -1).astype(F32)

    small0 = 7 * d
    rest0 = small0 + 2 * DN_HEADS
    w_q = w_in[:, :d] * (DA_HEAD_DIM ** -0.5 * LOG2E)
    w_qv_t = jnp.concatenate([w_q, w_in[:, 2 * d:3 * d]], axis=1).T.astype(BF16)
    w_kdn = jnp.concatenate([w_in[:, d:2 * d], w_in[:, 3 * d:small0]], axis=1).astype(BF16)
    w_small = jnp.zeros((d, LANES), F32).at[:, :2 * DN_HEADS].set(w_in[:, small0:rest0]).astype(BF16)
    w_rest = w_in[:, rest0:].astype(BF16)
    g_pre = row(ln_mix_pre)

    qvt = _norm_matmul_t(h2d, g_pre, w_qv_t, BF16, tm)
    kdn = _dn_proj(h2d, g_pre, w_kdn, conv_w.astype(F32), seq, min(256, seq))
    cg, ba = _norm_matmul(h2d, g_pre, [w_rest, w_small], [BF16, F32], tm)
    (mkv,) = _norm_matmul(mem2d, row(mem_norm_g), [w_mem_kv.astype(BF16)], [BF16], min(512, batch * mem_len))

    bias = _bias_table(rel_bias.astype(F32), ATTN_TILE)
    y_a = _diff_attention(da_lambda.astype(F32), qvt, kdn, bias, row(da_subln_g), batch, seq, lam_init)

    y_b = _gdn(kdn, ba, _pad_lanes(dn_a_log, DN_HEADS), _pad_lanes(dn_dt_bias, DN_HEADS),
               row(dn_norm_g), batch, seq)

    mkt = mkv[:, :d].reshape(batch, mem_len, CA_HEADS, CA_HEAD_DIM).transpose(0, 2, 3, 1)
    mv = mkv[:, d:].reshape(batch, mem_len, CA_HEADS, CA_HEAD_DIM).transpose(0, 2, 1, 3)
    h1 = _merge(h2d, cg, y_a, y_b, mkt, mv, w_branch.astype(BF16), w_out.astype(BF16),
                row(ln_mix_post), batch, seq, min(512, seq))

    return _ffn(h1, row(ln_ff_pre), row(ln_ff_post), w_ff1.astype(BF16), w_ff2.astype(BF16), min(512, seq))


def kernel(x, mem, rel_bias, w_in, conv_w, dn_a_log, dn_dt_bias, dn_norm_g, da_lambda, da_subln_g,
           mem_norm_g, w_mem_kv, w_branch, w_out, ln_mix_pre, ln_mix_post, ln_ff_pre, ln_ff_post,
           w_ff1, w_ff2):
    batch, seq, d = x.shape
    mem_len = mem.shape[1]
    assert seq % (ATTN_QTILES * ATTN_TILE) == 0 and seq % DN_PAIR == 0
    h2d = x.reshape(batch * seq, d)
    mem2d = mem.reshape(batch * mem_len, d)
    for layer in range(w_in.shape[0]):
        h2d = _layer(h2d, mem2d, rel_bias, w_in[layer], conv_w[layer], dn_a_log[layer], dn_dt_bias[layer],
                     dn_norm_g[layer], da_lambda[layer], da_subln_g[layer], mem_norm_g[layer],
                     w_mem_kv[layer], w_branch[layer], w_out[layer], ln_mix_pre[layer], ln_mix_post[layer],
                     ln_ff_pre[layer], ln_ff_post[layer], w_ff1[layer], w_ff2[layer],
                     batch, seq, mem_len, layer)
    return h2d.reshape(batch, seq, d)
```

```python
import functools
import math

import numpy as np
import jax
import jax.numpy as jnp
from jax import lax
from jax.experimental import pallas as pl
from jax.experimental.pallas import tpu as pltpu

F32 = jnp.float32
BF16 = jnp.bfloat16

EPS = 1e-6
MASK_VALUE = -1e30
LOG2E = math.log2(math.e)
LANES = 128
SUBLANES = 8
VMEM_LIMIT_BYTES = 56 * 1024 * 1024

DA_HEADS = 8
DA_HEAD_DIM = 64
DN_HEADS = 8
DN_DIM = 128
DN_CONV = 4
DN_CHUNK = 64
DN_PAIR = 2 * DN_CHUNK
CA_HEADS = 4
CA_HEAD_DIM = 256
REL_BUCKETS = 32
REL_MAX_EXACT = 16
REL_MAX_DIST = 128

DN_PROJ_SLICE = 256
GDN_STEP_PAIRS = 4
ATTN_TILE = 256
ATTN_QTILES = 4
ATTN_TRIP_TILES = 8
SCORE_LIMIT = 60.0


def _params(*sem):
    return pltpu.CompilerParams(dimension_semantics=sem, vmem_limit_bytes=VMEM_LIMIT_BYTES)


def _mm(a, b):
    return jnp.dot(a.astype(BF16), b.astype(BF16), preferred_element_type=F32)


def _mm_nt(a, b):
    return lax.dot_general(a.astype(BF16), b.astype(BF16), (((1,), (1,)), ((), ())),
                           preferred_element_type=F32)


def _mm_exact_rhs(m01, a):
    hi = a.astype(BF16)
    r1 = a - hi.astype(F32)
    mid = r1.astype(BF16)
    lo = (r1 - mid.astype(F32)).astype(BF16)
    dot = functools.partial(jnp.dot, preferred_element_type=F32)
    return dot(m01, hi) + dot(m01, mid) + dot(m01, lo)


def _rms(x, axis):
    return x * lax.rsqrt(jnp.mean(x * x, axis=axis, keepdims=True) + EPS)


def _sigmoid(x):
    return 1.0 / (1.0 + jnp.exp(-x))


def _silu(x):
    h = 0.5 * x
    return h + h * jnp.tanh(h)


def _normed(x_ref, g_ref):
    return (_rms(x_ref[...], -1) * g_ref[...]).astype(BF16)


def _norm_matmul_body(x_ref, g_ref, *refs, chunk):
    n_out = len(refs) // 2
    n = _normed(x_ref, g_ref)
    for w_ref, o_ref in zip(refs[:n_out], refs[n_out:]):
        cols = w_ref.shape[1]
        step = min(chunk, cols)
        for c0 in range(0, cols, step):
            o_ref[:, c0:c0 + step] = jnp.dot(n, w_ref[:, c0:c0 + step],
                                             preferred_element_type=F32).astype(o_ref.dtype)


def _norm_matmul(x2d, g_row, weights, out_dtypes, tm):
    t, d = x2d.shape
    return pl.pallas_call(
        functools.partial(_norm_matmul_body, chunk=1024),
        grid=(t // tm,),
        in_specs=[pl.BlockSpec((tm, d), lambda i: (i, 0)),
                  pl.BlockSpec((1, d), lambda i: (0, 0))]
                 + [pl.BlockSpec(w.shape, lambda i: (0, 0)) for w in weights],
        out_specs=[pl.BlockSpec((tm, w.shape[1]), lambda i: (i, 0)) for w in weights],
        out_shape=[jax.ShapeDtypeStruct((t, w.shape[1]), dt) for w, dt in zip(weights, out_dtypes)],
        compiler_params=_params("parallel"),
        name="norm_matmul",
    )(x2d, g_row, *weights)


def _norm_matmul_t_body(x_ref, g_ref, wt_ref, o_ref, *, chunk):
    n = _normed(x_ref, g_ref)
    for r0 in range(0, wt_ref.shape[0], chunk):
        o_ref[r0:r0 + chunk, :] = lax.dot_general(
            wt_ref[r0:r0 + chunk, :], n, (((1,), (1,)), ((), ())),
            preferred_element_type=F32).astype(o_ref.dtype)


def _norm_matmul_t(x2d, g_row, wt, out_dtype, tm):
    t, d = x2d.shape
    rows = wt.shape[0]
    return pl.pallas_call(
        functools.partial(_norm_matmul_t_body, chunk=512),
        grid=(t // tm,),
        in_specs=[pl.BlockSpec((tm, d), lambda i: (i, 0)),
                  pl.BlockSpec((1, d), lambda i: (0, 0)),
                  pl.BlockSpec((rows, d), lambda i: (0, 0))],
        out_specs=pl.BlockSpec((rows, tm), lambda i: (0, i)),
        out_shape=jax.ShapeDtypeStruct((rows, t), out_dtype),
        compiler_params=_params("parallel"),
        name="norm_matmul_t",
    )(x2d, g_row, wt)


BIAS_PREV, BIAS_DIAG, BIAS_MASKED, BIAS_NONE = 0, 1, 2, 3


def _bucket_table(tile):
    k = np.arange(tile)[:, None]
    q = np.arange(tile)[None, :]

    def bucket(dist):
        nf = np.maximum(dist, 1).astype(np.float64)
        large = REL_MAX_EXACT + np.trunc(
            np.log(nf / REL_MAX_EXACT) / math.log(REL_MAX_DIST / REL_MAX_EXACT)
            * (REL_BUCKETS - REL_MAX_EXACT)).astype(np.int64)
        large = np.minimum(large, REL_BUCKETS - 1)
        return np.where(dist < REL_MAX_EXACT, dist, large)

    prev = bucket(q - k + tile)
    diag = np.where(q >= k, bucket(np.maximum(q - k, 0)), -1)
    masked = np.full((tile, tile), -1)
    far = np.full((tile, tile), REL_BUCKETS - 1)
    return np.concatenate([prev, diag, masked, far], axis=0).astype(np.int32)


def _bias_table_body(rb_ref, bucket_ref, o_ref):
    h = pl.program_id(0)
    far = rb_ref[REL_BUCKETS - 1, h]
    bk = bucket_ref[...]
    out = jnp.full(bk.shape, MASK_VALUE, F32)
    for b in range(REL_BUCKETS):
        out = jnp.where(bk == b, (rb_ref[b, h] - far) * LOG2E, out)
    o_ref[...] = out


def _bias_table(rel_bias, tile):
    heads = rel_bias.shape[1]
    buckets = jnp.asarray(_bucket_table(tile))
    rows = buckets.shape[0]
    return pl.pallas_call(
        _bias_table_body,
        grid=(heads,),
        in_specs=[pl.BlockSpec(memory_space=pltpu.SMEM),
                  pl.BlockSpec((rows, tile), lambda h: (0, 0))],
        out_specs=pl.BlockSpec((None, rows, tile), lambda h: (h, 0, 0)),
        out_shape=jax.ShapeDtypeStruct((heads, rows, tile), F32),
        compiler_params=_params("arbitrary"),
        name="bias_table",
    )(rel_bias, buckets)


def _attn_body(lam_ref, qt_ref, k_ref, vt_ref, bias_ref, g_ref, o_ref,
               qz_ref, m_ref, x_ref, l_ref, acc_ref, p_ref, *, tile, nsub, lam_init):
    j = pl.program_id(2)
    hd = 2 * DA_HEAD_DIM
    qw = nsub * tile
    t0 = nsub * j

    sub = 2 * tile
    qt = qt_ref[...]
    row = lax.broadcasted_iota(jnp.int32, (hd, tile), 0)
    zero = jnp.zeros((hd, tile), qt.dtype)
    for a in range(nsub):
        qa = qt[:, a * tile:(a + 1) * tile]
        qz_ref[:, a * sub:a * sub + tile] = jnp.where(row < DA_HEAD_DIM, qa, zero)
        qz_ref[:, a * sub + tile:(a + 1) * sub] = jnp.where(row >= DA_HEAD_DIM, qa, zero)

    def tile_off(t):
        return pl.multiple_of(jnp.maximum(t, 0) * tile, tile)

    def bias_block(block):
        b = bias_ref[pl.ds(pl.multiple_of(block * tile, tile), tile), :]
        return jnp.concatenate([b, b], axis=1)

    def bias_cols(t):
        parts = []
        for a in range(nsub):
            rel = t - (t0 + a)
            parts.append(bias_block(jnp.where(
                rel == 0, BIAS_DIAG, jnp.where(rel == -1, BIAS_PREV, jnp.where(rel < -1, BIAS_NONE, BIAS_MASKED)))))
        return jnp.concatenate(parts, axis=1)

    def scores(off, first_sub=0, kinds=None):
        s = jnp.dot(k_ref[pl.ds(off, tile), :], qz_ref[:, first_sub * sub:], preferred_element_type=F32)
        if kinds is None:
            return s
        pieces = [s[:, i * sub:(i + 1) * sub] for i in range(len(kinds))]
        return jnp.concatenate([x if kind is None else x + bias_block(kind) for x, kind in zip(pieces, kinds)], axis=1)

    def values(off, n_keys, p):
        return jnp.dot(vt_ref[:, pl.ds(off, n_keys)], p, preferred_element_type=F32)

    def by_sublane(s):
        return s.reshape(tile // SUBLANES, SUBLANES, s.shape[1])

    def exact_tile(off, bias, first):
        s = scores(off) + bias
        c = jnp.max(s, axis=0, keepdims=True)
        if first:
            m_new = c
        else:
            m_old = m_ref[...]
            m_new = jnp.maximum(m_old, c)
            alpha = jnp.exp2(m_old - m_new)
        p = jnp.exp2(s - m_new)
        rows = jnp.sum(by_sublane(p), axis=0)
        pv = values(off, tile, p.astype(BF16))
        acc_ref[...] = pv if first else alpha * acc_ref[...] + pv
        l_ref[...] = rows if first else alpha * l_ref[...] + rows
        m_ref[...] = m_new

    def diagonal_block():
        acc = lsum = xmax = None
        for c in reversed(range(nsub)):
            kinds = [BIAS_DIAG if a == c else BIAS_PREV if a == c + 1 else None for a in range(c, nsub)]
            s = scores(tile_off(t0 + c), c, kinds)
            p = jnp.exp2(s)
            part = jnp.max(by_sublane(s), axis=0)
            rows = jnp.sum(by_sublane(p), axis=0)
            pv = values(tile_off(t0 + c), tile, p.astype(BF16))
            if acc is None:
                acc, lsum, xmax = pv, rows, part
            else:
                acc = jnp.concatenate([pv[:, :sub], pv[:, sub:] + acc], axis=1)
                lsum = jnp.concatenate([rows[:, :sub], rows[:, sub:] + lsum], axis=1)
                xmax = jnp.concatenate([part[:, :sub], jnp.maximum(part[:, sub:], xmax)], axis=1)
        acc_ref[...], l_ref[...], x_ref[...] = acc, lsum, xmax

    def far_tiles(start, count, last_kinds=None):
        cpart = None
        lpart = None
        for i in range(count):
            s = scores(tile_off(start + i), 0, last_kinds if i == count - 1 else None)
            part = jnp.max(by_sublane(s), axis=0)
            cpart = part if cpart is None else jnp.maximum(cpart, part)
            p = jnp.exp2(s)
            rows = jnp.sum(by_sublane(p), axis=0)
            lpart = rows if lpart is None else lpart + rows
            p_ref[i * tile:(i + 1) * tile, :] = p.astype(BF16)
        n_keys = count * tile
        acc_ref[...] = acc_ref[...] + values(tile_off(start), n_keys, p_ref[:n_keys, :])
        l_ref[...] = l_ref[...] + lpart
        x_ref[...] = jnp.maximum(x_ref[...], cpart)

    def finish():
        lv = lam_ref[...]
        lam = (jnp.exp(jnp.sum(lv[0:1] * lv[1:2], axis=-1, keepdims=True))
               - jnp.exp(jnp.sum(lv[2:3] * lv[3:4], axis=-1, keepdims=True)) + lam_init)
        acc = acc_ref[...] * (1.0 / jnp.sum(l_ref[...], axis=0, keepdims=True))
        ot = jnp.concatenate([acc[:, a * sub:a * sub + tile] - lam * acc[:, a * sub + tile:(a + 1) * sub]
                              for a in range(nsub)], axis=1)
        o = _rms(ot, 0).T * g_ref[...] * (1.0 - lam_init)
        o_ref[...] = o.astype(o_ref.dtype)

    assert nsub % 2 == 0
    n_even = jnp.maximum(t0 - 2, 0)
    diagonal_block()

    @pl.when(j >= 1)
    def _():
        far_tiles(t0 - 2, 2, [BIAS_PREV] + [None] * (nsub - 1))

    def trip(u, carry):
        far_tiles(u * ATTN_TRIP_TILES, ATTN_TRIP_TILES)
        return carry

    n_trips = n_even // ATTN_TRIP_TILES
    lax.fori_loop(0, n_trips, trip, 0)
    done = n_trips * ATTN_TRIP_TILES
    count = ATTN_TRIP_TILES // 2
    while count >= 2:
        take = (n_even - done) >= count

        @pl.when(take)
        def _(done=done, count=count):
            far_tiles(done, count)

        done = done + jnp.where(take, count, 0)
        count //= 2
    finish()

    col_max = jnp.max(x_ref[...], axis=0, keepdims=True)

    @pl.when(jnp.max(jnp.abs(col_max)) > SCORE_LIMIT)
    def _():
        exact_tile(tile_off(t0), bias_cols(t0), first=True)

        def body(u, carry):
            t = u + jnp.where(u >= t0, 1, 0)
            exact_tile(tile_off(t), bias_cols(t), first=False)
            return carry

        lax.fori_loop(0, t0 + nsub - 1, body, 0)
        finish()


def _diff_attention(lam_vecs, qvt, kdn, bias, g_row, batch, seq, lam_init):
    tile = ATTN_TILE
    qw = ATTN_QTILES * tile
    assert seq % qw == 0
    nqt = seq // qw
    hd = 2 * DA_HEAD_DIM
    return pl.pallas_call(
        functools.partial(_attn_body, tile=tile, nsub=ATTN_QTILES, lam_init=lam_init),
        grid=(batch, DA_HEADS, nqt),
        in_specs=[pl.BlockSpec((4, DA_HEAD_DIM), lambda b, h, i: (0, 0)),
                  pl.BlockSpec((hd, qw), lambda b, h, i: (h, b * nqt + i)),
                  pl.BlockSpec((seq, hd), lambda b, h, i: (b, h)),
                  pl.BlockSpec((hd, seq), lambda b, h, i: (DA_HEADS + h, b)),
                  pl.BlockSpec((None,) + bias.shape[1:], lambda b, h, i: (h, 0, 0)),
                  pl.BlockSpec((1, hd), lambda b, h, i: (0, 0))],
        out_specs=pl.BlockSpec((qw, hd), lambda b, h, i: (b * nqt + i, h)),
        out_shape=jax.ShapeDtypeStruct((batch * seq, DA_HEADS * hd), BF16),
        scratch_shapes=[pltpu.VMEM((hd, 2 * qw), BF16),
                        pltpu.VMEM((1, 2 * qw), F32),
                        pltpu.VMEM((SUBLANES, 2 * qw), F32),
                        pltpu.VMEM((SUBLANES, 2 * qw), F32),
                        pltpu.VMEM((hd, 2 * qw), F32),
                        pltpu.VMEM((ATTN_TRIP_TILES * tile, 2 * qw), BF16)],
        compiler_params=_params("parallel", "parallel", "arbitrary"),
        name="diff_attn",
    )(lam_vecs, qvt, kdn, qvt, bias, g_row)


def _shift_rows(x, prev, s):
    if s == 0:
        return x
    rolled = pltpu.roll(x, s, 0)
    prolled = pltpu.roll(prev, s, 0)
    r8 = lax.broadcasted_iota(jnp.int32, prev.shape, 0)
    top = jnp.where(r8 < s, prolled, rolled[:SUBLANES])
    return jnp.concatenate([top, rolled[SUBLANES:]], axis=0)


def _dn_proj_body(x_ref, g_ref, w_ref, cw_ref, o_ref, xq_ref, xk_ref, xv_ref, tail_ref, *, tiles_per_seq):
    assert DN_CONV == 4
    width = DN_HEADS * DN_DIM
    rows = x_ref.shape[0]

    @pl.when(pl.program_id(0) % tiles_per_seq == 0)
    def _():
        for x_scr in (xq_ref, xk_ref, xv_ref):
            x_scr[:SUBLANES, :] = jnp.zeros((SUBLANES, width), F32)

    @pl.when(pl.program_id(0) % tiles_per_seq != 0)
    def _():
        for which, x_scr in enumerate((xq_ref, xk_ref, xv_ref)):
            x_scr[:SUBLANES, :] = tail_ref[which]

    n = _normed(x_ref, g_ref)

    def project(c0, ncols):
        return jnp.dot(n, w_ref[:, c0:c0 + ncols], preferred_element_type=F32)

    step = DN_PROJ_SLICE
    plain = [g * width + c0 for g in (0, 4) for c0 in range(0, width, step)]
    slices = [(which, c0) for which in range(3) for c0 in range(0, width, step)]
    for i, (which, c0) in enumerate(slices):
        x_scr = (xq_ref, xk_ref, xv_ref)[which]
        xp = project((1 + which) * width + c0, step)
        x_scr[SUBLANES:, c0:c0 + step] = xp
        tail_ref[which, :, c0:c0 + step] = xp[rows - SUBLANES:]
        j = i - (len(slices) - len(plain))
        if j >= 0:
            o_ref[:, plain[j]:plain[j] + step] = project(plain[j], step).astype(o_ref.dtype)
        for h in range(c0 // DN_DIM, (c0 + step) // DN_DIM):
            cols = slice(h * DN_DIM, (h + 1) * DN_DIM)
            w = 0.5 * cw_ref[:, which * width + h * DN_DIM:which * width + (h + 1) * DN_DIM]
            xe = x_scr[:, cols]
            xe1 = pltpu.roll(xe, 1, 0)
            u = w[1:2] * xe + w[0:1] * xe1
            y = w[3:4] * xe[SUBLANES:] + w[2:3] * xe1[SUBLANES:] + pltpu.roll(u, 2, 0)[SUBLANES:]
            y = y + y * jnp.tanh(y)
            if which < 2:
                inv = lax.rsqrt(jnp.sum(y * y, axis=-1, keepdims=True) + EPS)
                y = y * (inv * (DN_DIM ** -0.5) if which == 0 else inv)
            c0 = (1 + which) * width + h * DN_DIM
            o_ref[:, c0:c0 + DN_DIM] = y.astype(o_ref.dtype)


def _dn_proj(x2d, g_row, w_kdn, conv_w, seq, tm):
    t, d = x2d.shape
    width = DN_HEADS * DN_DIM
    x_scr = pltpu.VMEM((SUBLANES + tm, width), F32)
    return pl.pallas_call(
        functools.partial(_dn_proj_body, tiles_per_seq=seq // tm),
        grid=(t // tm,),
        in_specs=[pl.BlockSpec((tm, d), lambda i: (i, 0)),
                  pl.BlockSpec((1, d), lambda i: (0, 0)),
                  pl.BlockSpec(w_kdn.shape, lambda i: (0, 0)),
                  pl.BlockSpec(conv_w.shape, lambda i: (0, 0))],
        out_specs=pl.BlockSpec((tm, w_kdn.shape[1]), lambda i: (i, 0)),
        out_shape=jax.ShapeDtypeStruct((t, w_kdn.shape[1]), BF16),
        scratch_shapes=[x_scr, x_scr, x_scr, pltpu.VMEM((3, SUBLANES, width), F32)],
        compiler_params=_params("arbitrary"),
        name="dn_proj",
    )(x2d, g_row, w_kdn, conv_w)


def _gdn_body(q_ref, k_ref, v_ref, z_ref, ba_ref, alog_ref, dtb_ref, gn_ref, o_ref, state_ref, *, npairs):
    p = DN_PAIR
    c = DN_CHUNK
    heads = range(DN_HEADS)
    head_cols = [slice(h * DN_DIM, (h + 1) * DN_DIM) for h in heads]

    @pl.when(pl.program_id(1) == 0)
    def _():
        state_ref[...] = jnp.zeros(state_ref.shape, F32)

    ri = lax.broadcasted_iota(jnp.int32, (p, p), 0)
    ci = lax.broadcasted_iota(jnp.int32, (p, p), 1)
    same = (ri // c) == (ci // c)
    causal = same & (ri >= ci)
    strict = same & (ri > ci)
    cum01 = jnp.where(causal, 1.0, 0.0).astype(BF16)

    def lane(x, j):
        return jnp.broadcast_to(x[:, j:j + 1], (x.shape[0], DN_DIM))

    first_chunk_lanes = lax.broadcasted_iota(jnp.int32, (c, p), 1) < c

    def side_by_side(bd):
        return jnp.where(first_chunk_lanes, bd[:c], bd[c:])

    def block_diagonal(ss):
        return jnp.concatenate([jnp.where(first_chunk_lanes, ss, 0.0), jnp.where(first_chunk_lanes, 0.0, ss)], axis=0)

    def prepare(pr):
        tok = slice(pr * p, (pr + 1) * p)
        ba = ba_ref[tok, :]
        beta_all = _sigmoid(ba)
        xs = ba + dtb_ref[...]
        softplus = jnp.maximum(xs, 0.0) + jnp.log(1.0 + jnp.exp(-jnp.abs(xs)))
        g_all = -jnp.exp(alog_ref[...]) * softplus
        gc_all = _mm_exact_rhs(cum01, g_all)
        glast_all = jnp.concatenate(
            [jnp.broadcast_to(gc_all[c - 1:c], (c, LANES)), jnp.broadcast_to(gc_all[p - 1:p], (c, LANES))], axis=0)
        tail_all = jnp.exp(glast_all - gc_all)
        dl_all = jnp.exp(glast_all)

        q = [q_ref[tok, cols].astype(F32) for cols in head_cols]
        k = [k_ref[tok, cols].astype(F32) for cols in head_cols]
        v = [v_ref[tok, cols].astype(F32) for cols in head_cols]
        gc = [lane(gc_all, DN_HEADS + h) for h in heads]
        gc_all_t = gc_all.T
        gr = [jnp.broadcast_to(gc_all_t[DN_HEADS + h:DN_HEADS + h + 1], (p, p)) for h in heads]
        decay = [jnp.exp(jnp.where(causal, gc[h] - gr[h], MASK_VALUE)) for h in heads]
        exp_g = [jnp.exp(x) for x in gc]
        beta = [lane(beta_all, h) for h in heads]
        kb = [k[h] * beta[h] for h in heads]
        vb = [v[h] * beta[h] for h in heads]
        kq = [_mm_nt(jnp.concatenate([kb[h], q[h]], axis=0), k[h]) for h in heads]
        yield None
        low = [jnp.where(strict, kq[h][:p] * decay[h], 0.0) for h in heads]
        intra = [jnp.where(causal, kq[h][p:] * decay[h], 0.0) for h in heads]
        y = [-side_by_side(x) for x in low]
        pw = [_mm(side_by_side(x), x) for x in low]
        yield None
        for it in range(5):
            pw_bd = [block_diagonal(x) for x in pw]
            if it < 4:
                both = [_mm(jnp.concatenate([y[h], pw[h]], axis=0), pw_bd[h]) for h in heads]
                y = [y[h] + pw[h] + both[h][:c] for h in heads]
                pw = [both[h][c:] for h in heads]
            else:
                y = [y[h] + pw[h] + _mm(y[h], pw_bd[h]) for h in heads]
            yield None
        rhs = [jnp.concatenate([vb[h], kb[h] * exp_g[h]], axis=1) for h in heads]
        sol = [rhs[h] + _mm(block_diagonal(y[h]), rhs[h]) for h in heads]
        q_dec = [q[h] * exp_g[h] for h in heads]
        tail_all_t = tail_all.T
        kt_t = [k[h].T * jnp.broadcast_to(tail_all_t[DN_HEADS + h:DN_HEADS + h + 1], (DN_DIM, p))
                for h in heads]
        yield sol, intra, q_dec, kt_t, dl_all

    zeros = jnp.zeros((c, DN_DIM), F32)

    def advance(pr, operands):
        sol, intra, q_dec, kt_t, dl_all = operands
        tok = slice(pr * p, (pr + 1) * p)
        outs = [[], []]
        for ch in range(2):
            rows = slice(ch * c, (ch + 1) * c)
            st = [state_ref[h] for h in heads]
            on_state = [_mm(jnp.concatenate([sol[h][rows, DN_DIM:], q_dec[h][rows]], axis=0), st[h]) for h in heads]
            yield None
            v_new = [sol[h][rows, :DN_DIM] - on_state[h][:c] for h in heads]
            v_pad = [jnp.concatenate([x, zeros] if ch == 0 else [zeros, x], axis=0) for x in v_new]
            on_v = [_mm(jnp.concatenate([intra[h][rows], kt_t[h]], axis=0), v_pad[h]) for h in heads]
            outs[ch] = [on_state[h][c:] + on_v[h][:c] for h in heads]
            for h in heads:
                dl = jnp.broadcast_to(dl_all[ch * c:ch * c + 1, DN_HEADS + h:DN_HEADS + h + 1], (1, DN_DIM))
                state_ref[h] = st[h] * dl + on_v[h][c:]
            yield None
        for h in heads:
            o = jnp.concatenate([outs[0][h], outs[1][h]], axis=0)
            yb = _rms(o, -1) * gn_ref[...] * _silu(z_ref[tok, head_cols[h]].astype(F32))
            o_ref[tok, head_cols[h]] = yb.astype(o_ref.dtype)

    def drain(gen):
        result = None
        for item in gen:
            result = item if item is not None else result
        return result

    operands = drain(prepare(0))
    for pr in range(npairs):
        chain = advance(pr, operands)
        nxt = prepare(pr + 1) if pr + 1 < npairs else iter(())
        operands = None
        chain_done = False
        while not chain_done:
            for _ in range(2):
                item = next(nxt, None)
                operands = item if item is not None else operands
            chain_done = next(chain, "done") == "done"
        rest = drain(nxt)
        operands = rest if rest is not None else operands


def _gdn(kdn, ba, alog_row, dtb_row, gn_row, batch, seq):
    t = batch * seq
    p = GDN_STEP_PAIRS * DN_PAIR
    nt = seq // p
    width = DN_HEADS * DN_DIM
    tok = lambda col: pl.BlockSpec((p, width), lambda b, i: (b * nt + i, col))
    row = pl.BlockSpec((1, LANES), lambda b, i: (0, 0))
    return pl.pallas_call(
        functools.partial(_gdn_body, npairs=GDN_STEP_PAIRS),
        grid=(batch, nt),
        in_specs=[tok(1), tok(2), tok(3), tok(4),
                  pl.BlockSpec((p, LANES), lambda b, i: (b * nt + i, 0)),
                  row, row, row],
        out_specs=pl.BlockSpec((p, width), lambda b, i: (b * nt + i, 0)),
        out_shape=jax.ShapeDtypeStruct((t, width), BF16),
        scratch_shapes=[pltpu.VMEM((DN_HEADS, DN_DIM, DN_DIM), F32)],
        compiler_params=_params("parallel", "arbitrary"),
        name="gdn",
    )(kdn, kdn, kdn, kdn, ba, alog_row, dtb_row, gn_row)


def _merge_body(x_ref, cq_ref, g0_ref, g1_ref, g2_ref, ya_ref, yb_ref, mkt_ref, mv_ref,
                wb_ref, wo_ref, ln_ref, o_ref):
    cq = cq_ref[...]
    heads = []
    for h in range(CA_HEADS):
        qh = cq[:, h * CA_HEAD_DIM:(h + 1) * CA_HEAD_DIM]
        s = jnp.dot(qh, mkt_ref[h], preferred_element_type=F32) * (CA_HEAD_DIM ** -0.5)
        e = jnp.exp(s - jnp.max(s, axis=-1, keepdims=True))
        pr = e / jnp.sum(e, axis=-1, keepdims=True)
        heads.append(jnp.dot(pr.astype(BF16), mv_ref[h], preferred_element_type=F32))
    yc = jnp.concatenate(heads, axis=1)

    mixed = _sigmoid(g0_ref[...].astype(F32)) * jnp.dot(ya_ref[...], wb_ref[0], preferred_element_type=F32)
    mixed = mixed + _sigmoid(g1_ref[...].astype(F32)) * jnp.dot(yb_ref[...], wb_ref[1], preferred_element_type=F32)
    mixed = mixed + _sigmoid(g2_ref[...].astype(F32)) * _mm(yc, wb_ref[2])
    out = _mm(mixed, wo_ref[...])
    o_ref[...] = x_ref[...] + _rms(out, -1) * ln_ref[...]


def _merge(x2d, cg, ya, yb, mkt, mv, wb, wo, ln_row, batch, seq, tm):
    t, d = x2d.shape
    nt = seq // tm
    tok = lambda col: pl.BlockSpec((tm, d), lambda i: (i, col))
    mem = pl.BlockSpec((None, CA_HEADS, CA_HEAD_DIM, CA_HEAD_DIM), lambda i: (i // nt, 0, 0, 0))
    return pl.pallas_call(
        _merge_body,
        grid=(t // tm,),
        in_specs=[tok(0), tok(0), tok(1), tok(2), tok(3), tok(0), tok(0), mem, mem,
                  pl.BlockSpec((3, d, d), lambda i: (0, 0, 0)),
                  pl.BlockSpec((d, d), lambda i: (0, 0)),
                  pl.BlockSpec((1, d), lambda i: (0, 0))],
        out_specs=tok(0),
        out_shape=jax.ShapeDtypeStruct((t, d), F32),
        compiler_params=_params("parallel"),
        name="merge",
    )(x2d, cg, cg, cg, cg, ya, yb, mkt, mv, wb, wo, ln_row)


def _ffn_body(h_ref, pre_ref, post_ref, w1_ref, w2_ref, o_ref, *, chunk):
    hh = h_ref[...]
    n = (_rms(hh, -1) * pre_ref[...]).astype(BF16)
    d_ff = w1_ref.shape[1]
    f = jnp.zeros(hh.shape, F32)
    for c0 in range(0, d_ff, chunk):
        a = jnp.dot(n, w1_ref[:, c0:c0 + chunk], preferred_element_type=F32)
        a = jnp.square(jnp.maximum(a, 0.0))
        f = f + jnp.dot(a.astype(BF16), w2_ref[c0:c0 + chunk, :], preferred_element_type=F32)
    o_ref[...] = hh + _rms(f, -1) * post_ref[...]


def _ffn(h2d, pre_row, post_row, w1, w2, tm):
    t, d = h2d.shape
    d_ff = w1.shape[1]
    return pl.pallas_call(
        functools.partial(_ffn_body, chunk=1024),
        grid=(t // tm,),
        in_specs=[pl.BlockSpec((tm, d), lambda i: (i, 0)),
                  pl.BlockSpec((1, d), lambda i: (0, 0)),
                  pl.BlockSpec((1, d), lambda i: (0, 0)),
                  pl.BlockSpec((d, d_ff), lambda i: (0, 0)),
                  pl.BlockSpec((d_ff, d), lambda i: (0, 0))],
        out_specs=pl.BlockSpec((tm, d), lambda i: (i, 0)),
        out_shape=jax.ShapeDtypeStruct((t, d), F32),
        compiler_params=_params("parallel"),
        name="ffn",
    )(h2d, pre_row, post_row, w1, w2)


def _pad_lanes(vec, offset):
    return jnp.zeros((1, LANES), F32).at[0, offset:offset + vec.shape[0]].set(vec.astype(F32))


def _layer(h2d, mem2d, rel_bias, w_in, conv_w, dn_a_log, dn_dt_bias, dn_norm_g, da_lambda,
           da_subln_g, mem_norm_g, w_mem_kv, w_branch, w_out, ln_mix_pre, ln_mix_post,
           ln_ff_pre, ln_ff_post, w_ff1, w_ff2, batch, seq, mem_len, layer):
    d = h2d.shape[1]
    lam_init = 0.8 - 0.6 * math.exp(-0.3 * layer)
    tm = min(512, seq)
    row = lambda v: v.reshape(1, -1).astype(F32)

    small0 = 7 * d
    rest0 = small0 + 2 * DN_HEADS
    w_q = w_in[:, :d] * (DA_HEAD_DIM ** -0.5 * LOG2E)
    w_qv_t = jnp.concatenate([w_q, w_in[:, 2 * d:3 * d]], axis=1).T.astype(BF16)
    w_kdn = jnp.concatenate([w_in[:, d:2 * d], w_in[:, 3 * d:small0]], axis=1).astype(BF16)
    w_small = jnp.zeros((d, LANES), F32).at[:, :2 * DN_HEADS].set(w_in[:, small0:rest0]).astype(BF16)
    w_rest = w_in[:, rest0:].astype(BF16)
    g_pre = row(ln_mix_pre)

    qvt = _norm_matmul_t(h2d, g_pre, w_qv_t, BF16, tm)
    kdn = _dn_proj(h2d, g_pre, w_kdn, conv_w.astype(F32), seq, tm)
    cg, ba = _norm_matmul(h2d, g_pre, [w_rest, w_small], [BF16, F32], tm)
    (mkv,) = _norm_matmul(mem2d, row(mem_norm_g), [w_mem_kv.astype(BF16)], [BF16], min(512, batch * mem_len))

    bias = _bias_table(rel_bias.astype(F32), ATTN_TILE)
    y_a = _diff_attention(da_lambda.astype(F32), qvt, kdn, bias, row(da_subln_g), batch, seq, lam_init)

    y_b = _gdn(kdn, ba, _pad_lanes(dn_a_log, DN_HEADS), _pad_lanes(dn_dt_bias, DN_HEADS),
               row(dn_norm_g), batch, seq)

    mkt = mkv[:, :d].reshape(batch, mem_len, CA_HEADS, CA_HEAD_DIM).transpose(0, 2, 3, 1)
    mv = mkv[:, d:].reshape(batch, mem_len, CA_HEADS, CA_HEAD_DIM).transpose(0, 2, 1, 3)
    h1 = _merge(h2d, cg, y_a, y_b, mkt, mv, w_branch.astype(BF16), w_out.astype(BF16),
                row(ln_mix_post), batch, seq, min(512, seq))

    return _ffn(h1, row(ln_ff_pre), row(ln_ff_post), w_ff1.astype(BF16), w_ff2.astype(BF16), min(512, seq))


def kernel(x, mem, rel_bias, w_in, conv_w, dn_a_log, dn_dt_bias, dn_norm_g, da_lambda, da_subln_g,
           mem_norm_g, w_mem_kv, w_branch, w_out, ln_mix_pre, ln_mix_post, ln_ff_pre, ln_ff_post,
           w_ff1, w_ff2):
    batch, seq, d = x.shape
    mem_len = mem.shape[1]
    assert seq % (ATTN_QTILES * ATTN_TILE) == 0 and seq % DN_PAIR == 0
    h2d = x.reshape(batch * seq, d)
    mem2d = mem.reshape(batch * mem_len, d)
    for layer in range(w_in.shape[0]):
        h2d = _layer(h2d, mem2d, rel_bias, w_in[layer], conv_w[layer], dn_a_log[layer], dn_dt_bias[layer],
                     dn_norm_g[layer], da_lambda[layer], da_subln_g[layer], mem_norm_g[layer],
                     w_mem_kv[layer], w_branch[layer], w_out[layer], ln_mix_pre[layer], ln_mix_post[layer],
                     ln_ff_pre[layer], ln_ff_post[layer], w_ff1[layer], w_ff2[layer],
                     batch, seq, mem_len, layer)
    return h2d.reshape(batch, seq, d)
```

```python
import functools
import math

import numpy as np
import jax
import jax.numpy as jnp
from jax import lax
from jax.experimental import pallas as pl
from jax.experimental.pallas import tpu as pltpu

F32 = jnp.float32
BF16 = jnp.bfloat16

EPS = 1e-6
MASK_VALUE = -1e30
LOG2E = math.log2(math.e)
LANES = 128
SUBLANES = 8
VMEM_LIMIT_BYTES = 56 * 1024 * 1024

DA_HEADS = 8
DA_HEAD_DIM = 64
DN_HEADS = 8
DN_DIM = 128
DN_CONV = 4
DN_CHUNK = 64
DN_PAIR = 2 * DN_CHUNK
CA_HEADS = 4
CA_HEAD_DIM = 256
REL_BUCKETS = 32
REL_MAX_EXACT = 16
REL_MAX_DIST = 128

DN_PROJ_SLICE = 256
GDN_STEP_PAIRS = 4
ATTN_TILE = 256
ATTN_QTILES = 4
ATTN_TRIP_TILES = 8
SCORE_LIMIT = 60.0


def _params(*sem):
    return pltpu.CompilerParams(dimension_semantics=sem, vmem_limit_bytes=VMEM_LIMIT_BYTES)


def _mm(a, b):
    return jnp.dot(a.astype(BF16), b.astype(BF16), preferred_element_type=F32)


def _mm_nt(a, b):
    return lax.dot_general(a.astype(BF16), b.astype(BF16), (((1,), (1,)), ((), ())),
                           preferred_element_type=F32)


def _mm_exact_rhs(m01, a):
    hi = a.astype(BF16)
    r1 = a - hi.astype(F32)
    mid = r1.astype(BF16)
    lo = (r1 - mid.astype(F32)).astype(BF16)
    dot = functools.partial(jnp.dot, preferred_element_type=F32)
    return dot(m01, hi) + dot(m01, mid) + dot(m01, lo)


def _rms(x, axis):
    return x * lax.rsqrt(jnp.mean(x * x, axis=axis, keepdims=True) + EPS)


def _sigmoid(x):
    return 1.0 / (1.0 + jnp.exp(-x))


def _silu(x):
    h = 0.5 * x
    return h + h * jnp.tanh(h)


def _normed(x_ref, g_ref):
    return (_rms(x_ref[...], -1) * g_ref[...]).astype(BF16)


def _norm_matmul_body(x_ref, g_ref, *refs, chunk):
    n_out = len(refs) // 2
    n = _normed(x_ref, g_ref)
    for w_ref, o_ref in zip(refs[:n_out], refs[n_out:]):
        cols = w_ref.shape[1]
        step = min(chunk, cols)
        for c0 in range(0, cols, step):
            o_ref[:, c0:c0 + step] = jnp.dot(n, w_ref[:, c0:c0 + step],
                                             preferred_element_type=F32).astype(o_ref.dtype)


def _norm_matmul(x2d, g_row, weights, out_dtypes, tm):
    t, d = x2d.shape
    return pl.pallas_call(
        functools.partial(_norm_matmul_body, chunk=1024),
        grid=(t // tm,),
        in_specs=[pl.BlockSpec((tm, d), lambda i: (i, 0)),
                  pl.BlockSpec((1, d), lambda i: (0, 0))]
                 + [pl.BlockSpec(w.shape, lambda i: (0, 0)) for w in weights],
        out_specs=[pl.BlockSpec((tm, w.shape[1]), lambda i: (i, 0)) for w in weights],
        out_shape=[jax.ShapeDtypeStruct((t, w.shape[1]), dt) for w, dt in zip(weights, out_dtypes)],
        compiler_params=_params("parallel"),
        name="norm_matmul",
    )(x2d, g_row, *weights)


def _norm_matmul_t_body(x_ref, g_ref, wt_ref, o_ref, *, chunk):
    n = _normed(x_ref, g_ref)
    for r0 in range(0, wt_ref.shape[0], chunk):
        o_ref[r0:r0 + chunk, :] = lax.dot_general(
            wt_ref[r0:r0 + chunk, :], n, (((1,), (1,)), ((), ())),
            preferred_element_type=F32).astype(o_ref.dtype)


def _norm_matmul_t(x2d, g_row, wt, out_dtype, tm):
    t, d = x2d.shape
    rows = wt.shape[0]
    return pl.pallas_call(
        functools.partial(_norm_matmul_t_body, chunk=512),
        grid=(t // tm,),
        in_specs=[pl.BlockSpec((tm, d), lambda i: (i, 0)),
                  pl.BlockSpec((1, d), lambda i: (0, 0)),
                  pl.BlockSpec((rows, d), lambda i: (0, 0))],
        out_specs=pl.BlockSpec((rows, tm), lambda i: (0, i)),
        out_shape=jax.ShapeDtypeStruct((rows, t), out_dtype),
        compiler_params=_params("parallel"),
        name="norm_matmul_t",
    )(x2d, g_row, wt)


BIAS_PREV, BIAS_DIAG, BIAS_MASKED, BIAS_NONE = 0, 1, 2, 3


def _bucket_table(tile):
    k = np.arange(tile)[:, None]
    q = np.arange(tile)[None, :]

    def bucket(dist):
        nf = np.maximum(dist, 1).astype(np.float64)
        large = REL_MAX_EXACT + np.trunc(
            np.log(nf / REL_MAX_EXACT) / math.log(REL_MAX_DIST / REL_MAX_EXACT)
            * (REL_BUCKETS - REL_MAX_EXACT)).astype(np.int64)
        large = np.minimum(large, REL_BUCKETS - 1)
        return np.where(dist < REL_MAX_EXACT, dist, large)

    prev = bucket(q - k + tile)
    diag = np.where(q >= k, bucket(np.maximum(q - k, 0)), -1)
    masked = np.full((tile, tile), -1)
    far = np.full((tile, tile), REL_BUCKETS - 1)
    return np.concatenate([prev, diag, masked, far], axis=0).astype(np.int32)


def _bias_table_body(rb_ref, bucket_ref, o_ref):
    h = pl.program_id(0)
    far = rb_ref[REL_BUCKETS - 1, h]
    bk = bucket_ref[...]
    out = jnp.full(bk.shape, MASK_VALUE, F32)
    for b in range(REL_BUCKETS):
        out = jnp.where(bk == b, (rb_ref[b, h] - far) * LOG2E, out)
    o_ref[...] = out


def _bias_table(rel_bias, tile):
    heads = rel_bias.shape[1]
    buckets = jnp.asarray(_bucket_table(tile))
    rows = buckets.shape[0]
    return pl.pallas_call(
        _bias_table_body,
        grid=(heads,),
        in_specs=[pl.BlockSpec(memory_space=pltpu.SMEM),
                  pl.BlockSpec((rows, tile), lambda h: (0, 0))],
        out_specs=pl.BlockSpec((None, rows, tile), lambda h: (h, 0, 0)),
        out_shape=jax.ShapeDtypeStruct((heads, rows, tile), F32),
        compiler_params=_params("arbitrary"),
        name="bias_table",
    )(rel_bias, buckets)


def _attn_body(lam_ref, qt_ref, k_ref, vt_ref, bias_ref, g_ref, o_ref,
               qz_ref, m_ref, x_ref, l_ref, acc_ref, p_ref, *, tile, nsub, lam_init):
    j = pl.program_id(2)
    hd = 2 * DA_HEAD_DIM
    qw = nsub * tile
    t0 = nsub * j

    sub = 2 * tile
    qt = qt_ref[...]
    row = lax.broadcasted_iota(jnp.int32, (hd, tile), 0)
    zero = jnp.zeros((hd, tile), qt.dtype)
    for a in range(nsub):
        qa = qt[:, a * tile:(a + 1) * tile]
        qz_ref[:, a * sub:a * sub + tile] = jnp.where(row < DA_HEAD_DIM, qa, zero)
        qz_ref[:, a * sub + tile:(a + 1) * sub] = jnp.where(row >= DA_HEAD_DIM, qa, zero)

    def tile_off(t):
        return pl.multiple_of(jnp.maximum(t, 0) * tile, tile)

    def bias_block(block):
        b = bias_ref[pl.ds(pl.multiple_of(block * tile, tile), tile), :]
        return jnp.concatenate([b, b], axis=1)

    def bias_cols(t):
        parts = []
        for a in range(nsub):
            rel = t - (t0 + a)
            parts.append(bias_block(jnp.where(
                rel == 0, BIAS_DIAG, jnp.where(rel == -1, BIAS_PREV, jnp.where(rel < -1, BIAS_NONE, BIAS_MASKED)))))
        return jnp.concatenate(parts, axis=1)

    def scores(off, first_sub=0, kinds=None):
        s = jnp.dot(k_ref[pl.ds(off, tile), :], qz_ref[:, first_sub * sub:], preferred_element_type=F32)
        if kinds is None:
            return s
        pieces = [s[:, i * sub:(i + 1) * sub] for i in range(len(kinds))]
        return jnp.concatenate([x if kind is None else x + bias_block(kind) for x, kind in zip(pieces, kinds)], axis=1)

    def values(off, n_keys, p):
        return jnp.dot(vt_ref[:, pl.ds(off, n_keys)], p, preferred_element_type=F32)

    def by_sublane(s):
        return s.reshape(tile // SUBLANES, SUBLANES, s.shape[1])

    def exact_tile(off, bias, first):
        s = scores(off) + bias
        c = jnp.max(s, axis=0, keepdims=True)
        if first:
            m_new = c
        else:
            m_old = m_ref[...]
            m_new = jnp.maximum(m_old, c)
            alpha = jnp.exp2(m_old - m_new)
        p = jnp.exp2(s - m_new)
        rows = jnp.sum(by_sublane(p), axis=0)
        pv = values(off, tile, p.astype(BF16))
        acc_ref[...] = pv if first else alpha * acc_ref[...] + pv
        l_ref[...] = rows if first else alpha * l_ref[...] + rows
        m_ref[...] = m_new

    def diagonal_block():
        acc = lsum = xmax = None
        order = list(reversed(range(nsub)))
        s_of = {}
        for c in order:
            kinds = [BIAS_DIAG if a == c else BIAS_PREV if a == c + 1 else None for a in range(c, nsub)]
            s_of[c] = scores(tile_off(t0 + c), c, kinds)
        for c in order:
            s = s_of[c]
            p = jnp.exp2(s)
            part = jnp.max(by_sublane(s), axis=0)
            rows = jnp.sum(by_sublane(p), axis=0)
            pv = values(tile_off(t0 + c), tile, p.astype(BF16))
            if acc is None:
                acc, lsum, xmax = pv, rows, part
            else:
                acc = jnp.concatenate([pv[:, :sub], pv[:, sub:] + acc], axis=1)
                lsum = jnp.concatenate([rows[:, :sub], rows[:, sub:] + lsum], axis=1)
                xmax = jnp.concatenate([part[:, :sub], jnp.maximum(part[:, sub:], xmax)], axis=1)
        acc_ref[...], l_ref[...], x_ref[...] = acc, lsum, xmax

    def far_tiles(start, count, last_kinds=None):
        cpart = None
        lpart = None
        for i in range(count):
            s = scores(tile_off(start + i), 0, last_kinds if i == count - 1 else None)
            part = jnp.max(by_sublane(s), axis=0)
            cpart = part if cpart is None else jnp.maximum(cpart, part)
            p = jnp.exp2(s)
            rows = jnp.sum(by_sublane(p), axis=0)
            lpart = rows if lpart is None else lpart + rows
            p_ref[i * tile:(i + 1) * tile, :] = p.astype(BF16)
        n_keys = count * tile
        acc_ref[...] = acc_ref[...] + values(tile_off(start), n_keys, p_ref[:n_keys, :])
        l_ref[...] = l_ref[...] + lpart
        x_ref[...] = jnp.maximum(x_ref[...], cpart)

    def finish():
        lv = lam_ref[...]
        lam = (jnp.exp(jnp.sum(lv[0:1] * lv[1:2], axis=-1, keepdims=True))
               - jnp.exp(jnp.sum(lv[2:3] * lv[3:4], axis=-1, keepdims=True)) + lam_init)
        acc = acc_ref[...] * (1.0 / jnp.sum(l_ref[...], axis=0, keepdims=True))
        ot = jnp.concatenate([acc[:, a * sub:a * sub + tile] - lam * acc[:, a * sub + tile:(a + 1) * sub]
                              for a in range(nsub)], axis=1)
        o = _rms(ot, 0).T * g_ref[...] * (1.0 - lam_init)
        o_ref[...] = o.astype(o_ref.dtype)

    assert nsub % 2 == 0
    n_even = jnp.maximum(t0 - 2, 0)
    diagonal_block()

    @pl.when(j >= 1)
    def _():
        far_tiles(t0 - 2, 2, [BIAS_PREV] + [None] * (nsub - 1))

    def trip(u, carry):
        far_tiles(u * ATTN_TRIP_TILES, ATTN_TRIP_TILES)
        return carry

    n_trips = n_even // ATTN_TRIP_TILES
    lax.fori_loop(0, n_trips, trip, 0)
    done = n_trips * ATTN_TRIP_TILES
    count = ATTN_TRIP_TILES // 2
    while count >= 2:
        take = (n_even - done) >= count

        @pl.when(take)
        def _(done=done, count=count):
            far_tiles(done, count)

        done = done + jnp.where(take, count, 0)
        count //= 2
    finish()

    col_max = jnp.max(x_ref[...], axis=0, keepdims=True)

    @pl.when(jnp.max(jnp.abs(col_max)) > SCORE_LIMIT)
    def _():
        exact_tile(tile_off(t0), bias_cols(t0), first=True)

        def body(u, carry):
            t = u + jnp.where(u >= t0, 1, 0)
            exact_tile(tile_off(t), bias_cols(t), first=False)
            return carry

        lax.fori_loop(0, t0 + nsub - 1, body, 0)
        finish()


def _diff_attention(lam_vecs, qvt, kdn, bias, g_row, batch, seq, lam_init):
    tile = ATTN_TILE
    qw = ATTN_QTILES * tile
    assert seq % qw == 0
    nqt = seq // qw
    hd = 2 * DA_HEAD_DIM
    return pl.pallas_call(
        functools.partial(_attn_body, tile=tile, nsub=ATTN_QTILES, lam_init=lam_init),
        grid=(batch, DA_HEADS, nqt),
        in_specs=[pl.BlockSpec((4, DA_HEAD_DIM), lambda b, h, i: (0, 0)),
                  pl.BlockSpec((hd, qw), lambda b, h, i: (h, b * nqt + i)),
                  pl.BlockSpec((seq, hd), lambda b, h, i: (b, h)),
                  pl.BlockSpec((hd, seq), lambda b, h, i: (DA_HEADS + h, b)),
                  pl.BlockSpec((None,) + bias.shape[1:], lambda b, h, i: (h, 0, 0)),
                  pl.BlockSpec((1, hd), lambda b, h, i: (0, 0))],
        out_specs=pl.BlockSpec((qw, hd), lambda b, h, i: (b * nqt + i, h)),
        out_shape=jax.ShapeDtypeStruct((batch * seq, DA_HEADS * hd), BF16),
        scratch_shapes=[pltpu.VMEM((hd, 2 * qw), BF16),
                        pltpu.VMEM((1, 2 * qw), F32),
                        pltpu.VMEM((SUBLANES, 2 * qw), F32),
                        pltpu.VMEM((SUBLANES, 2 * qw), F32),
                        pltpu.VMEM((hd, 2 * qw), F32),
                        pltpu.VMEM((ATTN_TRIP_TILES * tile, 2 * qw), BF16)],
        compiler_params=_params("parallel", "parallel", "arbitrary"),
        name="diff_attn",
    )(lam_vecs, qvt, kdn, qvt, bias, g_row)


def _shift_rows(x, prev, s):
    if s == 0:
        return x
    rolled = pltpu.roll(x, s, 0)
    prolled = pltpu.roll(prev, s, 0)
    r8 = lax.broadcasted_iota(jnp.int32, prev.shape, 0)
    top = jnp.where(r8 < s, prolled, rolled[:SUBLANES])
    return jnp.concatenate([top, rolled[SUBLANES:]], axis=0)


def _dn_proj_body(x_ref, g_ref, w_ref, cw_ref, o_ref, xq_ref, xk_ref, xv_ref, tail_ref, *, tiles_per_seq):
    assert DN_CONV == 4
    width = DN_HEADS * DN_DIM
    rows = x_ref.shape[0]

    @pl.when(pl.program_id(0) % tiles_per_seq == 0)
    def _():
        for x_scr in (xq_ref, xk_ref, xv_ref):
            x_scr[:SUBLANES, :] = jnp.zeros((SUBLANES, width), F32)

    @pl.when(pl.program_id(0) % tiles_per_seq != 0)
    def _():
        for which, x_scr in enumerate((xq_ref, xk_ref, xv_ref)):
            x_scr[:SUBLANES, :] = tail_ref[which]

    n = _normed(x_ref, g_ref)

    def project(c0, ncols):
        return jnp.dot(n, w_ref[:, c0:c0 + ncols], preferred_element_type=F32)

    step = DN_PROJ_SLICE
    plain = [g * width + c0 for g in (0, 4) for c0 in range(0, width, step)]
    slices = [(which, c0) for which in range(3) for c0 in range(0, width, step)]
    for i, (which, c0) in enumerate(slices):
        x_scr = (xq_ref, xk_ref, xv_ref)[which]
        xp = project((1 + which) * width + c0, step)
        x_scr[SUBLANES:, c0:c0 + step] = xp
        tail_ref[which, :, c0:c0 + step] = xp[rows - SUBLANES:]
        j = i - (len(slices) - len(plain))
        if j >= 0:
            o_ref[:, plain[j]:plain[j] + step] = project(plain[j], step).astype(o_ref.dtype)
        for h in range(c0 // DN_DIM, (c0 + step) // DN_DIM):
            cols = slice(h * DN_DIM, (h + 1) * DN_DIM)
            w = 0.5 * cw_ref[:, which * width + h * DN_DIM:which * width + (h + 1) * DN_DIM]
            xe = x_scr[:, cols]
            xe1 = pltpu.roll(xe, 1, 0)
            u = w[1:2] * xe + w[0:1] * xe1
            y = w[3:4] * xe[SUBLANES:] + w[2:3] * xe1[SUBLANES:] + pltpu.roll(u, 2, 0)[SUBLANES:]
            y = y + y * jnp.tanh(y)
            if which < 2:
                inv = lax.rsqrt(jnp.sum(y * y, axis=-1, keepdims=True) + EPS)
                y = y * (inv * (DN_DIM ** -0.5) if which == 0 else inv)
            c0 = (1 + which) * width + h * DN_DIM
            o_ref[:, c0:c0 + DN_DIM] = y.astype(o_ref.dtype)


def _dn_proj(x2d, g_row, w_kdn, conv_w, seq, tm):
    t, d = x2d.shape
    width = DN_HEADS * DN_DIM
    x_scr = pltpu.VMEM((SUBLANES + tm, width), F32)
    return pl.pallas_call(
        functools.partial(_dn_proj_body, tiles_per_seq=seq // tm),
        grid=(t // tm,),
        in_specs=[pl.BlockSpec((tm, d), lambda i: (i, 0)),
                  pl.BlockSpec((1, d), lambda i: (0, 0)),
                  pl.BlockSpec(w_kdn.shape, lambda i: (0, 0)),
                  pl.BlockSpec(conv_w.shape, lambda i: (0, 0))],
        out_specs=pl.BlockSpec((tm, w_kdn.shape[1]), lambda i: (i, 0)),
        out_shape=jax.ShapeDtypeStruct((t, w_kdn.shape[1]), BF16),
        scratch_shapes=[x_scr, x_scr, x_scr, pltpu.VMEM((3, SUBLANES, width), F32)],
        compiler_params=_params("arbitrary"),
        name="dn_proj",
    )(x2d, g_row, w_kdn, conv_w)


def _gdn_body(q_ref, k_ref, v_ref, z_ref, ba_ref, alog_ref, dtb_ref, gn_ref, o_ref, state_ref, *, npairs):
    p = DN_PAIR
    c = DN_CHUNK
    heads = range(DN_HEADS)
    head_cols = [slice(h * DN_DIM, (h + 1) * DN_DIM) for h in heads]

    @pl.when(pl.program_id(1) == 0)
    def _():
        state_ref[...] = jnp.zeros(state_ref.shape, F32)

    ri = lax.broadcasted_iota(jnp.int32, (p, p), 0)
    ci = lax.broadcasted_iota(jnp.int32, (p, p), 1)
    same = (ri // c) == (ci // c)
    causal = same & (ri >= ci)
    strict = same & (ri > ci)
    cum01 = jnp.where(causal, 1.0, 0.0).astype(BF16)

    def lane(x, j):
        return jnp.broadcast_to(x[:, j:j + 1], (x.shape[0], DN_DIM))

    first_chunk_lanes = lax.broadcasted_iota(jnp.int32, (c, p), 1) < c

    def side_by_side(bd):
        return jnp.where(first_chunk_lanes, bd[:c], bd[c:])

    def block_diagonal(ss):
        return jnp.concatenate([jnp.where(first_chunk_lanes, ss, 0.0), jnp.where(first_chunk_lanes, 0.0, ss)], axis=0)

    def prepare(pr):
        tok = slice(pr * p, (pr + 1) * p)
        ba = ba_ref[tok, :]
        beta_all = _sigmoid(ba)
        xs = ba + dtb_ref[...]
        softplus = jnp.maximum(xs, 0.0) + jnp.log(1.0 + jnp.exp(-jnp.abs(xs)))
        g_all = -jnp.exp(alog_ref[...]) * softplus
        gc_all = _mm_exact_rhs(cum01, g_all)
        glast_all = jnp.concatenate(
            [jnp.broadcast_to(gc_all[c - 1:c], (c, LANES)), jnp.broadcast_to(gc_all[p - 1:p], (c, LANES))], axis=0)
        tail_all = jnp.exp(glast_all - gc_all)
        dl_all = jnp.exp(glast_all)

        q = [q_ref[tok, cols].astype(F32) for cols in head_cols]
        k = [k_ref[tok, cols].astype(F32) for cols in head_cols]
        v = [v_ref[tok, cols].astype(F32) for cols in head_cols]
        gc = [lane(gc_all, DN_HEADS + h) for h in heads]
        gc_all_t = gc_all.T
        gr = [jnp.broadcast_to(gc_all_t[DN_HEADS + h:DN_HEADS + h + 1], (p, p)) for h in heads]
        decay = [jnp.exp(jnp.where(causal, gc[h] - gr[h], MASK_VALUE)) for h in heads]
        exp_g = [jnp.exp(x) for x in gc]
        beta = [lane(beta_all, h) for h in heads]
        kb = [k[h] * beta[h] for h in heads]
        vb = [v[h] * beta[h] for h in heads]
        kq = [_mm_nt(jnp.concatenate([kb[h], q[h]], axis=0), k[h]) for h in heads]
        yield None
        low = [jnp.where(strict, kq[h][:p] * decay[h], 0.0) for h in heads]
        intra = [jnp.where(causal, kq[h][p:] * decay[h], 0.0) for h in heads]
        y = [-side_by_side(x) for x in low]
        pw = [_mm(side_by_side(x), x) for x in low]
        yield None
        for it in range(5):
            pw_bd = [block_diagonal(x) for x in pw]
            if it < 4:
                both = [_mm(jnp.concatenate([y[h], pw[h]], axis=0), pw_bd[h]) for h in heads]
                y = [y[h] + pw[h] + both[h][:c] for h in heads]
                pw = [both[h][c:] for h in heads]
            else:
                y = [y[h] + pw[h] + _mm(y[h], pw_bd[h]) for h in heads]
            yield None
        rhs = [jnp.concatenate([vb[h], kb[h] * exp_g[h]], axis=1) for h in heads]
        sol = [rhs[h] + _mm(block_diagonal(y[h]), rhs[h]) for h in heads]
        q_dec = [q[h] * exp_g[h] for h in heads]
        tail_all_t = tail_all.T
        kt_t = [k[h].T * jnp.broadcast_to(tail_all_t[DN_HEADS + h:DN_HEADS + h + 1], (DN_DIM, p))
                for h in heads]
        yield sol, intra, q_dec, kt_t, dl_all

    zeros = jnp.zeros((c, DN_DIM), F32)

    def advance(pr, operands):
        sol, intra, q_dec, kt_t, dl_all = operands
        tok = slice(pr * p, (pr + 1) * p)
        outs = [[], []]
        for ch in range(2):
            rows = slice(ch * c, (ch + 1) * c)
            st = [state_ref[h] for h in heads]
            on_state = [_mm(jnp.concatenate([sol[h][rows, DN_DIM:], q_dec[h][rows]], axis=0), st[h]) for h in heads]
            yield None
            v_new = [sol[h][rows, :DN_DIM] - on_state[h][:c] for h in heads]
            v_pad = [jnp.concatenate([x, zeros] if ch == 0 else [zeros, x], axis=0) for x in v_new]
            on_v = [_mm(jnp.concatenate([intra[h][rows], kt_t[h]], axis=0), v_pad[h]) for h in heads]
            outs[ch] = [on_state[h][c:] + on_v[h][:c] for h in heads]
            for h in heads:
                dl = jnp.broadcast_to(dl_all[ch * c:ch * c + 1, DN_HEADS + h:DN_HEADS + h + 1], (1, DN_DIM))
                state_ref[h] = st[h] * dl + on_v[h][c:]
            yield None
        for h in heads:
            o = jnp.concatenate([outs[0][h], outs[1][h]], axis=0)
            yb = _rms(o, -1) * gn_ref[...] * _silu(z_ref[tok, head_cols[h]].astype(F32))
            o_ref[tok, head_cols[h]] = yb.astype(o_ref.dtype)

    def drain(gen):
        result = None
        for item in gen:
            result = item if item is not None else result
        return result

    operands = drain(prepare(0))
    for pr in range(npairs):
        chain = advance(pr, operands)
        nxt = prepare(pr + 1) if pr + 1 < npairs else iter(())
        operands = None
        chain_done = False
        while not chain_done:
            for _ in range(2):
                item = next(nxt, None)
                operands = item if item is not None else operands
            chain_done = next(chain, "done") == "done"
        rest = drain(nxt)
        operands = rest if rest is not None else operands


def _gdn(kdn, ba, alog_row, dtb_row, gn_row, batch, seq):
    t = batch * seq
    p = GDN_STEP_PAIRS * DN_PAIR
    nt = seq // p
    width = DN_HEADS * DN_DIM
    tok = lambda col: pl.BlockSpec((p, width), lambda b, i: (b * nt + i, col))
    row = pl.BlockSpec((1, LANES), lambda b, i: (0, 0))
    return pl.pallas_call(
        functools.partial(_gdn_body, npairs=GDN_STEP_PAIRS),
        grid=(batch, nt),
        in_specs=[tok(1), tok(2), tok(3), tok(4),
                  pl.BlockSpec((p, LANES), lambda b, i: (b * nt + i, 0)),
                  row, row, row],
        out_specs=pl.BlockSpec((p, width), lambda b, i: (b * nt + i, 0)),
        out_shape=jax.ShapeDtypeStruct((t, width), BF16),
        scratch_shapes=[pltpu.VMEM((DN_HEADS, DN_DIM, DN_DIM), F32)],
        compiler_params=_params("parallel", "arbitrary"),
        name="gdn",
    )(kdn, kdn, kdn, kdn, ba, alog_row, dtb_row, gn_row)


def _merge_body(x_ref, cq_ref, g0_ref, g1_ref, g2_ref, ya_ref, yb_ref, mkt_ref, mv_ref,
                wb_ref, wo_ref, ln_ref, o_ref):
    cq = cq_ref[...]
    scores = [jnp.dot(cq[:, h * CA_HEAD_DIM:(h + 1) * CA_HEAD_DIM], mkt_ref[h], preferred_element_type=F32)
              for h in range(CA_HEADS)]
    proj_a = jnp.dot(ya_ref[...], wb_ref[0], preferred_element_type=F32)
    proj_b = jnp.dot(yb_ref[...], wb_ref[1], preferred_element_type=F32)
    heads = []
    for h in range(CA_HEADS):
        s = scores[h] * (CA_HEAD_DIM ** -0.5)
        e = jnp.exp(s - jnp.max(s, axis=-1, keepdims=True))
        pr = e / jnp.sum(e, axis=-1, keepdims=True)
        heads.append(jnp.dot(pr.astype(BF16), mv_ref[h], preferred_element_type=F32))
    yc = jnp.concatenate(heads, axis=1)

    mixed = _sigmoid(g0_ref[...].astype(F32)) * proj_a
    mixed = mixed + _sigmoid(g1_ref[...].astype(F32)) * proj_b
    mixed = mixed + _sigmoid(g2_ref[...].astype(F32)) * _mm(yc, wb_ref[2])
    out = _mm(mixed, wo_ref[...])
    o_ref[...] = x_ref[...] + _rms(out, -1) * ln_ref[...]


def _merge(x2d, cg, ya, yb, mkt, mv, wb, wo, ln_row, batch, seq, tm):
    t, d = x2d.shape
    nt = seq // tm
    tok = lambda col: pl.BlockSpec((tm, d), lambda i: (i, col))
    mem = pl.BlockSpec((None, CA_HEADS, CA_HEAD_DIM, CA_HEAD_DIM), lambda i: (i // nt, 0, 0, 0))
    return pl.pallas_call(
        _merge_body,
        grid=(t // tm,),
        in_specs=[tok(0), tok(0), tok(1), tok(2), tok(3), tok(0), tok(0), mem, mem,
                  pl.BlockSpec((3, d, d), lambda i: (0, 0, 0)),
                  pl.BlockSpec((d, d), lambda i: (0, 0)),
                  pl.BlockSpec((1, d), lambda i: (0, 0))],
        out_specs=tok(0),
        out_shape=jax.ShapeDtypeStruct((t, d), F32),
        compiler_params=_params("parallel"),
        name="merge",
    )(x2d, cg, cg, cg, cg, ya, yb, mkt, mv, wb, wo, ln_row)


def _ffn_body(h_ref, pre_ref, post_ref, w1_ref, w2_ref, o_ref, *, chunk):
    hh = h_ref[...]
    n = (_rms(hh, -1) * pre_ref[...]).astype(BF16)
    d_ff = w1_ref.shape[1]
    f = jnp.zeros(hh.shape, F32)
    for c0 in range(0, d_ff, chunk):
        a = jnp.dot(n, w1_ref[:, c0:c0 + chunk], preferred_element_type=F32)
        a = jnp.square(jnp.maximum(a, 0.0))
        f = f + jnp.dot(a.astype(BF16), w2_ref[c0:c0 + chunk, :], preferred_element_type=F32)
    o_ref[...] = hh + _rms(f, -1) * post_ref[...]


def _ffn(h2d, pre_row, post_row, w1, w2, tm):
    t, d = h2d.shape
    d_ff = w1.shape[1]
    return pl.pallas_call(
        functools.partial(_ffn_body, chunk=1024),
        grid=(t // tm,),
        in_specs=[pl.BlockSpec((tm, d), lambda i: (i, 0)),
                  pl.BlockSpec((1, d), lambda i: (0, 0)),
                  pl.BlockSpec((1, d), lambda i: (0, 0)),
                  pl.BlockSpec((d, d_ff), lambda i: (0, 0)),
                  pl.BlockSpec((d_ff, d), lambda i: (0, 0))],
        out_specs=pl.BlockSpec((tm, d), lambda i: (i, 0)),
        out_shape=jax.ShapeDtypeStruct((t, d), F32),
        compiler_params=_params("parallel"),
        name="ffn",
    )(h2d, pre_row, post_row, w1, w2)


def _pad_lanes(vec, offset):
    return jnp.zeros((1, LANES), F32).at[0, offset:offset + vec.shape[0]].set(vec.astype(F32))


def _layer(h2d, mem2d, rel_bias, w_in, conv_w, dn_a_log, dn_dt_bias, dn_norm_g, da_lambda,
           da_subln_g, mem_norm_g, w_mem_kv, w_branch, w_out, ln_mix_pre, ln_mix_post,
           ln_ff_pre, ln_ff_post, w_ff1, w_ff2, batch, seq, mem_len, layer):
    d = h2d.shape[1]
    lam_init = 0.8 - 0.6 * math.exp(-0.3 * layer)
    tm = min(512, seq)
    row = lambda v: v.reshape(1, -1).astype(F32)

    small0 = 7 * d
    rest0 = small0 + 2 * DN_HEADS
    w_q = w_in[:, :d] * (DA_HEAD_DIM ** -0.5 * LOG2E)
    w_qv_t = jnp.concatenate([w_q, w_in[:, 2 * d:3 * d]], axis=1).T.astype(BF16)
    w_kdn = jnp.concatenate([w_in[:, d:2 * d], w_in[:, 3 * d:small0]], axis=1).astype(BF16)
    w_small = jnp.zeros((d, LANES), F32).at[:, :2 * DN_HEADS].set(w_in[:, small0:rest0]).astype(BF16)
    w_rest = w_in[:, rest0:].astype(BF16)
    g_pre = row(ln_mix_pre)

    qvt = _norm_matmul_t(h2d, g_pre, w_qv_t, BF16, tm)
    kdn = _dn_proj(h2d, g_pre, w_kdn, conv_w.astype(F32), seq, tm)
    cg, ba = _norm_matmul(h2d, g_pre, [w_rest, w_small], [BF16, F32], tm)
    (mkv,) = _norm_matmul(mem2d, row(mem_norm_g), [w_mem_kv.astype(BF16)], [BF16], min(512, batch * mem_len))

    bias = _bias_table(rel_bias.astype(F32), ATTN_TILE)
    y_a = _diff_attention(da_lambda.astype(F32), qvt, kdn, bias, row(da_subln_g), batch, seq, lam_init)

    y_b = _gdn(kdn, ba, _pad_lanes(dn_a_log, DN_HEADS), _pad_lanes(dn_dt_bias, DN_HEADS),
               row(dn_norm_g), batch, seq)

    mkt = mkv[:, :d].reshape(batch, mem_len, CA_HEADS, CA_HEAD_DIM).transpose(0, 2, 3, 1)
    mv = mkv[:, d:].reshape(batch, mem_len, CA_HEADS, CA_HEAD_DIM).transpose(0, 2, 1, 3)
    h1 = _merge(h2d, cg, y_a, y_b, mkt, mv, w_branch.astype(BF16), w_out.astype(BF16),
                row(ln_mix_post), batch, seq, min(512, seq))

    return _ffn(h1, row(ln_ff_pre), row(ln_ff_post), w_ff1.astype(BF16), w_ff2.astype(BF16), min(512, seq))


def kernel(x, mem, rel_bias, w_in, conv_w, dn_a_log, dn_dt_bias, dn_norm_g, da_lambda, da_subln_g,
           mem_norm_g, w_mem_kv, w_branch, w_out, ln_mix_pre, ln_mix_post, ln_ff_pre, ln_ff_post,
           w_ff1, w_ff2):
    batch, seq, d = x.shape
    mem_len = mem.shape[1]
    assert seq % (ATTN_QTILES * ATTN_TILE) == 0 and seq % DN_PAIR == 0
    h2d = x.reshape(batch * seq, d)
    mem2d = mem.reshape(batch * mem_len, d)
    for layer in range(w_in.shape[0]):
        h2d = _layer(h2d, mem2d, rel_bias, w_in[layer], conv_w[layer], dn_a_log[layer], dn_dt_bias[layer],
                     dn_norm_g[layer], da_lambda[layer], da_subln_g[layer], mem_norm_g[layer],
                     w_mem_kv[layer], w_branch[layer], w_out[layer], ln_mix_pre[layer], ln_mix_post[layer],
                     ln_ff_pre[layer], ln_ff_post[layer], w_ff1[layer], w_ff2[layer],
                     batch, seq, mem_len, layer)
    return h2d.reshape(batch, seq, d)
```

```python
import functools
import math

import numpy as np
import jax
import jax.numpy as jnp
from jax import lax
from jax.experimental import pallas as pl
from jax.experimental.pallas import tpu as pltpu

F32 = jnp.float32
BF16 = jnp.bfloat16

EPS = 1e-6
MASK_VALUE = -1e30
LOG2E = math.log2(math.e)
LANES = 128
SUBLANES = 8
VMEM_LIMIT_BYTES = 56 * 1024 * 1024

DA_HEADS = 8
DA_HEAD_DIM = 64
DN_HEADS = 8
DN_DIM = 128
DN_CONV = 4
DN_CHUNK = 64
DN_PAIR = 2 * DN_CHUNK
CA_HEADS = 4
CA_HEAD_DIM = 256
REL_BUCKETS = 32
REL_MAX_EXACT = 16
REL_MAX_DIST = 128

DN_PROJ_SLICE = 256
GDN_STEP_PAIRS = 4
ATTN_TILE = 256
ATTN_QTILES = 4
ATTN_TRIP_TILES = 8
SCORE_LIMIT = 60.0


def _params(*sem):
    return pltpu.CompilerParams(dimension_semantics=sem, vmem_limit_bytes=VMEM_LIMIT_BYTES)


def _mm(a, b):
    return jnp.dot(a.astype(BF16), b.astype(BF16), preferred_element_type=F32)


def _mm_nt(a, b):
    return lax.dot_general(a.astype(BF16), b.astype(BF16), (((1,), (1,)), ((), ())),
                           preferred_element_type=F32)


def _mm_exact_rhs(m01, a):
    hi = a.astype(BF16)
    r1 = a - hi.astype(F32)
    mid = r1.astype(BF16)
    lo = (r1 - mid.astype(F32)).astype(BF16)
    dot = functools.partial(jnp.dot, preferred_element_type=F32)
    return dot(m01, hi) + dot(m01, mid) + dot(m01, lo)


def _rms(x, axis):
    return x * lax.rsqrt(jnp.mean(x * x, axis=axis, keepdims=True) + EPS)


def _sigmoid(x):
    return 1.0 / (1.0 + jnp.exp(-x))


def _silu(x):
    h = 0.5 * x
    return h + h * jnp.tanh(h)


def _normed(x_ref, g_ref):
    return (_rms(x_ref[...], -1) * g_ref[...]).astype(BF16)


def _norm_matmul_body(x_ref, g_ref, *refs, chunk):
    n_out = len(refs) // 2
    n = _normed(x_ref, g_ref)
    for w_ref, o_ref in zip(refs[:n_out], refs[n_out:]):
        cols = w_ref.shape[1]
        step = min(chunk, cols)
        for c0 in range(0, cols, step):
            o_ref[:, c0:c0 + step] = jnp.dot(n, w_ref[:, c0:c0 + step],
                                             preferred_element_type=F32).astype(o_ref.dtype)


def _norm_matmul(x2d, g_row, weights, out_dtypes, tm):
    t, d = x2d.shape
    return pl.pallas_call(
        functools.partial(_norm_matmul_body, chunk=1024),
        grid=(t // tm,),
        in_specs=[pl.BlockSpec((tm, d), lambda i: (i, 0)),
                  pl.BlockSpec((1, d), lambda i: (0, 0))]
                 + [pl.BlockSpec(w.shape, lambda i: (0, 0)) for w in weights],
        out_specs=[pl.BlockSpec((tm, w.shape[1]), lambda i: (i, 0)) for w in weights],
        out_shape=[jax.ShapeDtypeStruct((t, w.shape[1]), dt) for w, dt in zip(weights, out_dtypes)],
        compiler_params=_params("parallel"),
        name="norm_matmul",
    )(x2d, g_row, *weights)


def _norm_matmul_t_body(x_ref, g_ref, wt_ref, o_ref, *, chunk):
    n = _normed(x_ref, g_ref)
    for r0 in range(0, wt_ref.shape[0], chunk):
        o_ref[r0:r0 + chunk, :] = lax.dot_general(
            wt_ref[r0:r0 + chunk, :], n, (((1,), (1,)), ((), ())),
            preferred_element_type=F32).astype(o_ref.dtype)


def _norm_matmul_t(x2d, g_row, wt, out_dtype, tm):
    t, d = x2d.shape
    rows = wt.shape[0]
    return pl.pallas_call(
        functools.partial(_norm_matmul_t_body, chunk=512),
        grid=(t // tm,),
        in_specs=[pl.BlockSpec((tm, d), lambda i: (i, 0)),
                  pl.BlockSpec((1, d), lambda i: (0, 0)),
                  pl.BlockSpec((rows, d), lambda i: (0, 0))],
        out_specs=pl.BlockSpec((rows, tm), lambda i: (0, i)),
        out_shape=jax.ShapeDtypeStruct((rows, t), out_dtype),
        compiler_params=_params("parallel"),
        name="norm_matmul_t",
    )(x2d, g_row, wt)


BIAS_PREV, BIAS_DIAG, BIAS_MASKED, BIAS_NONE = 0, 1, 2, 3


def _bucket_table(tile):
    k = np.arange(tile)[:, None]
    q = np.arange(tile)[None, :]

    def bucket(dist):
        nf = np.maximum(dist, 1).astype(np.float64)
        large = REL_MAX_EXACT + np.trunc(
            np.log(nf / REL_MAX_EXACT) / math.log(REL_MAX_DIST / REL_MAX_EXACT)
            * (REL_BUCKETS - REL_MAX_EXACT)).astype(np.int64)
        large = np.minimum(large, REL_BUCKETS - 1)
        return np.where(dist < REL_MAX_EXACT, dist, large)

    prev = bucket(q - k + tile)
    diag = np.where(q >= k, bucket(np.maximum(q - k, 0)), -1)
    masked = np.full((tile, tile), -1)
    far = np.full((tile, tile), REL_BUCKETS - 1)
    return np.concatenate([prev, diag, masked, far], axis=0).astype(np.int32)


def _bias_table_body(rb_ref, bucket_ref, o_ref):
    h = pl.program_id(0)
    far = rb_ref[REL_BUCKETS - 1, h]
    bk = bucket_ref[...]
    out = jnp.full(bk.shape, MASK_VALUE, F32)
    for b in range(REL_BUCKETS):
        out = jnp.where(bk == b, (rb_ref[b, h] - far) * LOG2E, out)
    o_ref[...] = out


def _bias_table(rel_bias, tile):
    heads = rel_bias.shape[1]
    buckets = jnp.asarray(_bucket_table(tile))
    rows = buckets.shape[0]
    return pl.pallas_call(
        _bias_table_body,
        grid=(heads,),
        in_specs=[pl.BlockSpec(memory_space=pltpu.SMEM),
                  pl.BlockSpec((rows, tile), lambda h: (0, 0))],
        out_specs=pl.BlockSpec((None, rows, tile), lambda h: (h, 0, 0)),
        out_shape=jax.ShapeDtypeStruct((heads, rows, tile), F32),
        compiler_params=_params("arbitrary"),
        name="bias_table",
    )(rel_bias, buckets)


def _attn_body(lam_ref, qt_ref, k_ref, vt_ref, bias_ref, g_ref, o_ref,
               qz_ref, m_ref, x_ref, l_ref, acc_ref, p_ref, *, tile, nsub, lam_init):
    j = pl.program_id(2)
    hd = 2 * DA_HEAD_DIM
    qw = nsub * tile
    t0 = nsub * j

    sub = 2 * tile
    qt = qt_ref[...]
    row = lax.broadcasted_iota(jnp.int32, (hd, tile), 0)
    zero = jnp.zeros((hd, tile), qt.dtype)
    for a in range(nsub):
        qa = qt[:, a * tile:(a + 1) * tile]
        qz_ref[:, a * sub:a * sub + tile] = jnp.where(row < DA_HEAD_DIM, qa, zero)
        qz_ref[:, a * sub + tile:(a + 1) * sub] = jnp.where(row >= DA_HEAD_DIM, qa, zero)

    def tile_off(t):
        return pl.multiple_of(jnp.maximum(t, 0) * tile, tile)

    def bias_block(block):
        b = bias_ref[pl.ds(pl.multiple_of(block * tile, tile), tile), :]
        return jnp.concatenate([b, b], axis=1)

    def bias_cols(t):
        parts = []
        for a in range(nsub):
            rel = t - (t0 + a)
            parts.append(bias_block(jnp.where(
                rel == 0, BIAS_DIAG, jnp.where(rel == -1, BIAS_PREV, jnp.where(rel < -1, BIAS_NONE, BIAS_MASKED)))))
        return jnp.concatenate(parts, axis=1)

    def scores(off, first_sub=0, kinds=None):
        s = jnp.dot(k_ref[pl.ds(off, tile), :], qz_ref[:, first_sub * sub:], preferred_element_type=F32)
        if kinds is None:
            return s
        pieces = [s[:, i * sub:(i + 1) * sub] for i in range(len(kinds))]
        return jnp.concatenate([x if kind is None else x + bias_block(kind) for x, kind in zip(pieces, kinds)], axis=1)

    def values(off, n_keys, p):
        return jnp.dot(vt_ref[:, pl.ds(off, n_keys)], p, preferred_element_type=F32)

    def by_sublane(s):
        return s.reshape(tile // SUBLANES, SUBLANES, s.shape[1])

    def exact_tile(off, bias, first):
        s = scores(off) + bias
        c = jnp.max(s, axis=0, keepdims=True)
        if first:
            m_new = c
        else:
            m_old = m_ref[...]
            m_new = jnp.maximum(m_old, c)
            alpha = jnp.exp2(m_old - m_new)
        p = jnp.exp2(s - m_new)
        rows = jnp.sum(by_sublane(p), axis=0)
        pv = values(off, tile, p.astype(BF16))
        acc_ref[...] = pv if first else alpha * acc_ref[...] + pv
        l_ref[...] = rows if first else alpha * l_ref[...] + rows
        m_ref[...] = m_new

    def diagonal_block(with_previous):
        acc = lsum = xmax = None
        order = list(reversed(range(nsub)))
        s_of = {}
        for c in order:
            kinds = [BIAS_DIAG if a == c else BIAS_PREV if a == c + 1 else None for a in range(c, nsub)]
            s_of[c] = scores(tile_off(t0 + c), c, kinds)
        if with_previous:
            before = [scores(tile_off(t0 - 2)), scores(tile_off(t0 - 1), 0, [BIAS_PREV] + [None] * (nsub - 1))]
        for c in order:
            s = s_of[c]
            p = jnp.exp2(s)
            part = jnp.max(by_sublane(s), axis=0)
            rows = jnp.sum(by_sublane(p), axis=0)
            pv = values(tile_off(t0 + c), tile, p.astype(BF16))
            if acc is None:
                acc, lsum, xmax = pv, rows, part
            else:
                acc = jnp.concatenate([pv[:, :sub], pv[:, sub:] + acc], axis=1)
                lsum = jnp.concatenate([rows[:, :sub], rows[:, sub:] + lsum], axis=1)
                xmax = jnp.concatenate([part[:, :sub], jnp.maximum(part[:, sub:], xmax)], axis=1)
        if with_previous:
            for i, s in enumerate(before):
                p = jnp.exp2(s)
                xmax = jnp.maximum(xmax, jnp.max(by_sublane(s), axis=0))
                lsum = lsum + jnp.sum(by_sublane(p), axis=0)
                p_ref[i * tile:(i + 1) * tile, :] = p.astype(BF16)
            acc = acc + values(tile_off(t0 - 2), 2 * tile, p_ref[:2 * tile, :])
        acc_ref[...], l_ref[...], x_ref[...] = acc, lsum, xmax

    def far_tiles(start, count):
        cpart = None
        lpart = None
        for i in range(count):
            s = scores(tile_off(start + i))
            part = jnp.max(by_sublane(s), axis=0)
            cpart = part if cpart is None else jnp.maximum(cpart, part)
            p = jnp.exp2(s)
            rows = jnp.sum(by_sublane(p), axis=0)
            lpart = rows if lpart is None else lpart + rows
            p_ref[i * tile:(i + 1) * tile, :] = p.astype(BF16)
        n_keys = count * tile
        acc_ref[...] = acc_ref[...] + values(tile_off(start), n_keys, p_ref[:n_keys, :])
        l_ref[...] = l_ref[...] + lpart
        x_ref[...] = jnp.maximum(x_ref[...], cpart)

    def finish():
        lv = lam_ref[...]
        lam = (jnp.exp(jnp.sum(lv[0:1] * lv[1:2], axis=-1, keepdims=True))
               - jnp.exp(jnp.sum(lv[2:3] * lv[3:4], axis=-1, keepdims=True)) + lam_init)
        acc = acc_ref[...] * (1.0 / jnp.sum(l_ref[...], axis=0, keepdims=True))
        ot = jnp.concatenate([acc[:, a * sub:a * sub + tile] - lam * acc[:, a * sub + tile:(a + 1) * sub]
                              for a in range(nsub)], axis=1)
        o = _rms(ot, 0).T * g_ref[...] * (1.0 - lam_init)
        o_ref[...] = o.astype(o_ref.dtype)

    assert nsub % 2 == 0
    n_even = jnp.maximum(t0 - 2, 0)

    @pl.when(j == 0)
    def _():
        diagonal_block(with_previous=False)

    @pl.when(j >= 1)
    def _():
        diagonal_block(with_previous=True)

    def trip(u, carry):
        far_tiles(u * ATTN_TRIP_TILES, ATTN_TRIP_TILES)
        return carry

    n_trips = n_even // ATTN_TRIP_TILES
    lax.fori_loop(0, n_trips, trip, 0)
    done = n_trips * ATTN_TRIP_TILES
    count = ATTN_TRIP_TILES // 2
    while count >= 2:
        take = (n_even - done) >= count

        @pl.when(take)
        def _(done=done, count=count):
            far_tiles(done, count)

        done = done + jnp.where(take, count, 0)
        count //= 2
    finish()

    col_max = jnp.max(x_ref[...], axis=0, keepdims=True)

    @pl.when(jnp.max(jnp.abs(col_max)) > SCORE_LIMIT)
    def _():
        exact_tile(tile_off(t0), bias_cols(t0), first=True)

        def body(u, carry):
            t = u + jnp.where(u >= t0, 1, 0)
            exact_tile(tile_off(t), bias_cols(t), first=False)
            return carry

        lax.fori_loop(0, t0 + nsub - 1, body, 0)
        finish()


def _diff_attention(lam_vecs, qvt, kdn, bias, g_row, batch, seq, lam_init):
    tile = ATTN_TILE
    qw = ATTN_QTILES * tile
    assert seq % qw == 0
    nqt = seq // qw
    hd = 2 * DA_HEAD_DIM
    return pl.pallas_call(
        functools.partial(_attn_body, tile=tile, nsub=ATTN_QTILES, lam_init=lam_init),
        grid=(batch, DA_HEADS, nqt),
        in_specs=[pl.BlockSpec((4, DA_HEAD_DIM), lambda b, h, i: (0, 0)),
                  pl.BlockSpec((hd, qw), lambda b, h, i: (h, b * nqt + i)),
                  pl.BlockSpec((seq, hd), lambda b, h, i: (b, h)),
                  pl.BlockSpec((hd, seq), lambda b, h, i: (DA_HEADS + h, b)),
                  pl.BlockSpec((None,) + bias.shape[1:], lambda b, h, i: (h, 0, 0)),
                  pl.BlockSpec((1, hd), lambda b, h, i: (0, 0))],
        out_specs=pl.BlockSpec((qw, hd), lambda b, h, i: (b * nqt + i, h)),
        out_shape=jax.ShapeDtypeStruct((batch * seq, DA_HEADS * hd), BF16),
        scratch_shapes=[pltpu.VMEM((hd, 2 * qw), BF16),
                        pltpu.VMEM((1, 2 * qw), F32),
                        pltpu.VMEM((SUBLANES, 2 * qw), F32),
                        pltpu.VMEM((SUBLANES, 2 * qw), F32),
                        pltpu.VMEM((hd, 2 * qw), F32),
                        pltpu.VMEM((ATTN_TRIP_TILES * tile, 2 * qw), BF16)],
        compiler_params=_params("parallel", "parallel", "arbitrary"),
        name="diff_attn",
    )(lam_vecs, qvt, kdn, qvt, bias, g_row)


def _shift_rows(x, prev, s):
    if s == 0:
        return x
    rolled = pltpu.roll(x, s, 0)
    prolled = pltpu.roll(prev, s, 0)
    r8 = lax.broadcasted_iota(jnp.int32, prev.shape, 0)
    top = jnp.where(r8 < s, prolled, rolled[:SUBLANES])
    return jnp.concatenate([top, rolled[SUBLANES:]], axis=0)


def _dn_proj_body(x_ref, g_ref, w_ref, cw_ref, o_ref, xq_ref, xk_ref, xv_ref, tail_ref, *, tiles_per_seq):
    assert DN_CONV == 4
    width = DN_HEADS * DN_DIM
    rows = x_ref.shape[0]

    @pl.when(pl.program_id(0) % tiles_per_seq == 0)
    def _():
        for x_scr in (xq_ref, xk_ref, xv_ref):
            x_scr[:SUBLANES, :] = jnp.zeros((SUBLANES, width), F32)

    @pl.when(pl.program_id(0) % tiles_per_seq != 0)
    def _():
        for which, x_scr in enumerate((xq_ref, xk_ref, xv_ref)):
            x_scr[:SUBLANES, :] = tail_ref[which]

    n = _normed(x_ref, g_ref)

    def project(c0, ncols):
        return jnp.dot(n, w_ref[:, c0:c0 + ncols], preferred_element_type=F32)

    step = DN_PROJ_SLICE
    plain = [g * width + c0 for g in (0, 4) for c0 in range(0, width, step)]
    slices = [(which, c0) for which in range(3) for c0 in range(0, width, step)]
    for i, (which, c0) in enumerate(slices):
        x_scr = (xq_ref, xk_ref, xv_ref)[which]
        xp = project((1 + which) * width + c0, step)
        x_scr[SUBLANES:, c0:c0 + step] = xp
        tail_ref[which, :, c0:c0 + step] = xp[rows - SUBLANES:]
        j = i - (len(slices) - len(plain))
        if j >= 0:
            o_ref[:, plain[j]:plain[j] + step] = project(plain[j], step).astype(o_ref.dtype)
        for h in range(c0 // DN_DIM, (c0 + step) // DN_DIM):
            cols = slice(h * DN_DIM, (h + 1) * DN_DIM)
            w = 0.5 * cw_ref[:, which * width + h * DN_DIM:which * width + (h + 1) * DN_DIM]
            xe = x_scr[:, cols]
            xe1 = pltpu.roll(xe, 1, 0)
            u = w[1:2] * xe + w[0:1] * xe1
            y = w[3:4] * xe[SUBLANES:] + w[2:3] * xe1[SUBLANES:] + pltpu.roll(u, 2, 0)[SUBLANES:]
            y = y + y * jnp.tanh(y)
            if which < 2:
                inv = lax.rsqrt(jnp.sum(y * y, axis=-1, keepdims=True) + EPS)
                y = y * (inv * (DN_DIM ** -0.5) if which == 0 else inv)
            c0 = (1 + which) * width + h * DN_DIM
            o_ref[:, c0:c0 + DN_DIM] = y.astype(o_ref.dtype)


def _dn_proj(x2d, g_row, w_kdn, conv_w, seq, tm):
    t, d = x2d.shape
    width = DN_HEADS * DN_DIM
    x_scr = pltpu.VMEM((SUBLANES + tm, width), F32)
    return pl.pallas_call(
        functools.partial(_dn_proj_body, tiles_per_seq=seq // tm),
        grid=(t // tm,),
        in_specs=[pl.BlockSpec((tm, d), lambda i: (i, 0)),
                  pl.BlockSpec((1, d), lambda i: (0, 0)),
                  pl.BlockSpec(w_kdn.shape, lambda i: (0, 0)),
                  pl.BlockSpec(conv_w.shape, lambda i: (0, 0))],
        out_specs=pl.BlockSpec((tm, w_kdn.shape[1]), lambda i: (i, 0)),
        out_shape=jax.ShapeDtypeStruct((t, w_kdn.shape[1]), BF16),
        scratch_shapes=[x_scr, x_scr, x_scr, pltpu.VMEM((3, SUBLANES, width), F32)],
        compiler_params=_params("arbitrary"),
        name="dn_proj",
    )(x2d, g_row, w_kdn, conv_w)


def _gdn_body(q_ref, k_ref, v_ref, z_ref, ba_ref, alog_ref, dtb_ref, gn_ref, o_ref, state_ref, *, npairs):
    p = DN_PAIR
    c = DN_CHUNK
    heads = range(DN_HEADS)
    head_cols = [slice(h * DN_DIM, (h + 1) * DN_DIM) for h in heads]

    @pl.when(pl.program_id(1) == 0)
    def _():
        state_ref[...] = jnp.zeros(state_ref.shape, F32)

    ri = lax.broadcasted_iota(jnp.int32, (p, p), 0)
    ci = lax.broadcasted_iota(jnp.int32, (p, p), 1)
    same = (ri // c) == (ci // c)
    causal = same & (ri >= ci)
    strict = same & (ri > ci)
    cum01 = jnp.where(causal, 1.0, 0.0).astype(BF16)

    def lane(x, j):
        return jnp.broadcast_to(x[:, j:j + 1], (x.shape[0], DN_DIM))

    first_chunk_lanes = lax.broadcasted_iota(jnp.int32, (c, p), 1) < c

    def side_by_side(bd):
        return jnp.where(first_chunk_lanes, bd[:c], bd[c:])

    def block_diagonal(ss):
        return jnp.concatenate([jnp.where(first_chunk_lanes, ss, 0.0), jnp.where(first_chunk_lanes, 0.0, ss)], axis=0)

    def prepare(pr):
        tok = slice(pr * p, (pr + 1) * p)
        ba = ba_ref[tok, :]
        beta_all = _sigmoid(ba)
        xs = ba + dtb_ref[...]
        softplus = jnp.maximum(xs, 0.0) + jnp.log(1.0 + jnp.exp(-jnp.abs(xs)))
        g_all = -jnp.exp(alog_ref[...]) * softplus
        gc_all = _mm_exact_rhs(cum01, g_all)
        glast_all = jnp.concatenate(
            [jnp.broadcast_to(gc_all[c - 1:c], (c, LANES)), jnp.broadcast_to(gc_all[p - 1:p], (c, LANES))], axis=0)
        tail_all = jnp.exp(glast_all - gc_all)
        dl_all = jnp.exp(glast_all)

        q = [q_ref[tok, cols].astype(F32) for cols in head_cols]
        k = [k_ref[tok, cols].astype(F32) for cols in head_cols]
        v = [v_ref[tok, cols].astype(F32) for cols in head_cols]
        gc = [lane(gc_all, DN_HEADS + h) for h in heads]
        gc_all_t = gc_all.T
        gr = [jnp.broadcast_to(gc_all_t[DN_HEADS + h:DN_HEADS + h + 1], (p, p)) for h in heads]
        decay = [jnp.exp(jnp.where(causal, gc[h] - gr[h], MASK_VALUE)) for h in heads]
        exp_g = [jnp.exp(x) for x in gc]
        beta = [lane(beta_all, h) for h in heads]
        kb = [k[h] * beta[h] for h in heads]
        vb = [v[h] * beta[h] for h in heads]
        kq = [_mm_nt(jnp.concatenate([kb[h], q[h]], axis=0), k[h]) for h in heads]
        yield None
        low = [jnp.where(strict, kq[h][:p] * decay[h], 0.0) for h in heads]
        intra = [jnp.where(causal, kq[h][p:] * decay[h], 0.0) for h in heads]
        y = [-side_by_side(x) for x in low]
        pw = [_mm(side_by_side(x), x) for x in low]
        yield None
        for it in range(5):
            pw_bd = [block_diagonal(x) for x in pw]
            if it < 4:
                both = [_mm(jnp.concatenate([y[h], pw[h]], axis=0), pw_bd[h]) for h in heads]
                y = [y[h] + pw[h] + both[h][:c] for h in heads]
                pw = [both[h][c:] for h in heads]
            else:
                y = [y[h] + pw[h] + _mm(y[h], pw_bd[h]) for h in heads]
            yield None
        rhs = [jnp.concatenate([vb[h], kb[h] * exp_g[h]], axis=1) for h in heads]
        sol = [rhs[h] + _mm(block_diagonal(y[h]), rhs[h]) for h in heads]
        q_dec = [q[h] * exp_g[h] for h in heads]
        tail_all_t = tail_all.T
        kt_t = [k[h].T * jnp.broadcast_to(tail_all_t[DN_HEADS + h:DN_HEADS + h + 1], (DN_DIM, p))
                for h in heads]
        yield sol, intra, q_dec, kt_t, dl_all

    zeros = jnp.zeros((c, DN_DIM), F32)

    def advance(pr, operands):
        sol, intra, q_dec, kt_t, dl_all = operands
        tok = slice(pr * p, (pr + 1) * p)
        outs = [[], []]
        for ch in range(2):
            rows = slice(ch * c, (ch + 1) * c)
            st = [state_ref[h] for h in heads]
            on_state = [_mm(jnp.concatenate([sol[h][rows, DN_DIM:], q_dec[h][rows]], axis=0), st[h]) for h in heads]
            yield None
            v_new = [sol[h][rows, :DN_DIM] - on_state[h][:c] for h in heads]
            v_pad = [jnp.concatenate([x, zeros] if ch == 0 else [zeros, x], axis=0) for x in v_new]
            on_v = [_mm(jnp.concatenate([intra[h][rows], kt_t[h]], axis=0), v_pad[h]) for h in heads]
            outs[ch] = [on_state[h][c:] + on_v[h][:c] for h in heads]
            for h in heads:
                dl = jnp.broadcast_to(dl_all[ch * c:ch * c + 1, DN_HEADS + h:DN_HEADS + h + 1], (1, DN_DIM))
                state_ref[h] = st[h] * dl + on_v[h][c:]
            yield None
        for h in heads:
            o = jnp.concatenate([outs[0][h], outs[1][h]], axis=0)
            yb = _rms(o, -1) * gn_ref[...] * _silu(z_ref[tok, head_cols[h]].astype(F32))
            o_ref[tok, head_cols[h]] = yb.astype(o_ref.dtype)

    def drain(gen):
        result = None
        for item in gen:
            result = item if item is not None else result
        return result

    operands = drain(prepare(0))
    for pr in range(npairs):
        chain = advance(pr, operands)
        nxt = prepare(pr + 1) if pr + 1 < npairs else iter(())
        operands = None
        chain_done = False
        while not chain_done:
            for _ in range(2):
                item = next(nxt, None)
                operands = item if item is not None else operands
            chain_done = next(chain, "done") == "done"
        rest = drain(nxt)
        operands = rest if rest is not None else operands


def _gdn(kdn, ba, alog_row, dtb_row, gn_row, batch, seq):
    t = batch * seq
    p = GDN_STEP_PAIRS * DN_PAIR
    nt = seq // p
    width = DN_HEADS * DN_DIM
    tok = lambda col: pl.BlockSpec((p, width), lambda b, i: (b * nt + i, col))
    row = pl.BlockSpec((1, LANES), lambda b, i: (0, 0))
    return pl.pallas_call(
        functools.partial(_gdn_body, npairs=GDN_STEP_PAIRS),
        grid=(batch, nt),
        in_specs=[tok(1), tok(2), tok(3), tok(4),
                  pl.BlockSpec((p, LANES), lambda b, i: (b * nt + i, 0)),
                  row, row, row],
        out_specs=pl.BlockSpec((p, width), lambda b, i: (b * nt + i, 0)),
        out_shape=jax.ShapeDtypeStruct((t, width), BF16),
        scratch_shapes=[pltpu.VMEM((DN_HEADS, DN_DIM, DN_DIM), F32)],
        compiler_params=_params("parallel", "arbitrary"),
        name="gdn",
    )(kdn, kdn, kdn, kdn, ba, alog_row, dtb_row, gn_row)


def _merge_body(x_ref, cq_ref, g0_ref, g1_ref, g2_ref, ya_ref, yb_ref, mkt_ref, mv_ref,
                wb_ref, wo_ref, ln_ref, o_ref):
    cq = cq_ref[...]
    scores = [jnp.dot(cq[:, h * CA_HEAD_DIM:(h + 1) * CA_HEAD_DIM], mkt_ref[h], preferred_element_type=F32)
              for h in range(CA_HEADS)]
    proj_a = jnp.dot(ya_ref[...], wb_ref[0], preferred_element_type=F32)
    proj_b = jnp.dot(yb_ref[...], wb_ref[1], preferred_element_type=F32)
    heads = []
    for h in range(CA_HEADS):
        s = scores[h] * (CA_HEAD_DIM ** -0.5)
        e = jnp.exp(s - jnp.max(s, axis=-1, keepdims=True))
        pr = e / jnp.sum(e, axis=-1, keepdims=True)
        heads.append(jnp.dot(pr.astype(BF16), mv_ref[h], preferred_element_type=F32))
    yc = jnp.concatenate(heads, axis=1)

    mixed = _sigmoid(g0_ref[...].astype(F32)) * proj_a
    mixed = mixed + _sigmoid(g1_ref[...].astype(F32)) * proj_b
    mixed = mixed + _sigmoid(g2_ref[...].astype(F32)) * _mm(yc, wb_ref[2])
    out = _mm(mixed, wo_ref[...])
    o_ref[...] = x_ref[...] + _rms(out, -1) * ln_ref[...]


def _merge(x2d, cg, ya, yb, mkt, mv, wb, wo, ln_row, batch, seq, tm):
    t, d = x2d.shape
    nt = seq // tm
    tok = lambda col: pl.BlockSpec((tm, d), lambda i: (i, col))
    mem = pl.BlockSpec((None, CA_HEADS, CA_HEAD_DIM, CA_HEAD_DIM), lambda i: (i // nt, 0, 0, 0))
    return pl.pallas_call(
        _merge_body,
        grid=(t // tm,),
        in_specs=[tok(0), tok(0), tok(1), tok(2), tok(3), tok(0), tok(0), mem, mem,
                  pl.BlockSpec((3, d, d), lambda i: (0, 0, 0)),
                  pl.BlockSpec((d, d), lambda i: (0, 0)),
                  pl.BlockSpec((1, d), lambda i: (0, 0))],
        out_specs=tok(0),
        out_shape=jax.ShapeDtypeStruct((t, d), F32),
        compiler_params=_params("parallel"),
        name="merge",
    )(x2d, cg, cg, cg, cg, ya, yb, mkt, mv, wb, wo, ln_row)


def _ffn_body(h_ref, pre_ref, post_ref, w1_ref, w2_ref, o_ref, *, chunk):
    hh = h_ref[...]
    n = (_rms(hh, -1) * pre_ref[...]).astype(BF16)
    d_ff = w1_ref.shape[1]
    f = jnp.zeros(hh.shape, F32)
    for c0 in range(0, d_ff, chunk):
        a = jnp.dot(n, w1_ref[:, c0:c0 + chunk], preferred_element_type=F32)
        a = jnp.square(jnp.maximum(a, 0.0))
        f = f + jnp.dot(a.astype(BF16), w2_ref[c0:c0 + chunk, :], preferred_element_type=F32)
    o_ref[...] = hh + _rms(f, -1) * post_ref[...]


def _ffn(h2d, pre_row, post_row, w1, w2, tm):
    t, d = h2d.shape
    d_ff = w1.shape[1]
    return pl.pallas_call(
        functools.partial(_ffn_body, chunk=1024),
        grid=(t // tm,),
        in_specs=[pl.BlockSpec((tm, d), lambda i: (i, 0)),
                  pl.BlockSpec((1, d), lambda i: (0, 0)),
                  pl.BlockSpec((1, d), lambda i: (0, 0)),
                  pl.BlockSpec((d, d_ff), lambda i: (0, 0)),
                  pl.BlockSpec((d_ff, d), lambda i: (0, 0))],
        out_specs=pl.BlockSpec((tm, d), lambda i: (i, 0)),
        out_shape=jax.ShapeDtypeStruct((t, d), F32),
        compiler_params=_params("parallel"),
        name="ffn",
    )(h2d, pre_row, post_row, w1, w2)


def _pad_lanes(vec, offset):
    return jnp.zeros((1, LANES), F32).at[0, offset:offset + vec.shape[0]].set(vec.astype(F32))


def _layer(h2d, mem2d, rel_bias, w_in, conv_w, dn_a_log, dn_dt_bias, dn_norm_g, da_lambda,
           da_subln_g, mem_norm_g, w_mem_kv, w_branch, w_out, ln_mix_pre, ln_mix_post,
           ln_ff_pre, ln_ff_post, w_ff1, w_ff2, batch, seq, mem_len, layer):
    d = h2d.shape[1]
    lam_init = 0.8 - 0.6 * math.exp(-0.3 * layer)
    tm = min(512, seq)
    row = lambda v: v.reshape(1, -1).astype(F32)

    small0 = 7 * d
    rest0 = small0 + 2 * DN_HEADS
    w_q = w_in[:, :d] * (DA_HEAD_DIM ** -0.5 * LOG2E)
    w_qv_t = jnp.concatenate([w_q, w_in[:, 2 * d:3 * d]], axis=1).T.astype(BF16)
    w_kdn = jnp.concatenate([w_in[:, d:2 * d], w_in[:, 3 * d:small0]], axis=1).astype(BF16)
    w_small = jnp.zeros((d, LANES), F32).at[:, :2 * DN_HEADS].set(w_in[:, small0:rest0]).astype(BF16)
    w_rest = w_in[:, rest0:].astype(BF16)
    g_pre = row(ln_mix_pre)

    qvt = _norm_matmul_t(h2d, g_pre, w_qv_t, BF16, tm)
    kdn = _dn_proj(h2d, g_pre, w_kdn, conv_w.astype(F32), seq, tm)
    cg, ba = _norm_matmul(h2d, g_pre, [w_rest, w_small], [BF16, F32], tm)
    (mkv,) = _norm_matmul(mem2d, row(mem_norm_g), [w_mem_kv.astype(BF16)], [BF16], min(512, batch * mem_len))

    bias = _bias_table(rel_bias.astype(F32), ATTN_TILE)
    y_a = _diff_attention(da_lambda.astype(F32), qvt, kdn, bias, row(da_subln_g), batch, seq, lam_init)

    y_b = _gdn(kdn, ba, _pad_lanes(dn_a_log, DN_HEADS), _pad_lanes(dn_dt_bias, DN_HEADS),
               row(dn_norm_g), batch, seq)

    mkt = mkv[:, :d].reshape(batch, mem_len, CA_HEADS, CA_HEAD_DIM).transpose(0, 2, 3, 1)
    mv = mkv[:, d:].reshape(batch, mem_len, CA_HEADS, CA_HEAD_DIM).transpose(0, 2, 1, 3)
    h1 = _merge(h2d, cg, y_a, y_b, mkt, mv, w_branch.astype(BF16), w_out.astype(BF16),
                row(ln_mix_post), batch, seq, min(512, seq))

    return _ffn(h1, row(ln_ff_pre), row(ln_ff_post), w_ff1.astype(BF16), w_ff2.astype(BF16), min(512, seq))


def kernel(x, mem, rel_bias, w_in, conv_w, dn_a_log, dn_dt_bias, dn_norm_g, da_lambda, da_subln_g,
           mem_norm_g, w_mem_kv, w_branch, w_out, ln_mix_pre, ln_mix_post, ln_ff_pre, ln_ff_post,
           w_ff1, w_ff2):
    batch, seq, d = x.shape
    mem_len = mem.shape[1]
    assert seq % (ATTN_QTILES * ATTN_TILE) == 0 and seq % DN_PAIR == 0
    h2d = x.reshape(batch * seq, d)
    mem2d = mem.reshape(batch * mem_len, d)
    for layer in range(w_in.shape[0]):
        h2d = _layer(h2d, mem2d, rel_bias, w_in[layer], conv_w[layer], dn_a_log[layer], dn_dt_bias[layer],
                     dn_norm_g[layer], da_lambda[layer], da_subln_g[layer], mem_norm_g[layer],
                     w_mem_kv[layer], w_branch[layer], w_out[layer], ln_mix_pre[layer], ln_mix_post[layer],
                     ln_ff_pre[layer], ln_ff_post[layer], w_ff1[layer], w_ff2[layer],
                     batch, seq, mem_len, layer)
    return h2d.reshape(batch, seq, d)
```

```python
import functools
import math

import numpy as np
import jax
import jax.numpy as jnp
from jax import lax
from jax.experimental import pallas as pl
from jax.experimental.pallas import tpu as pltpu

F32 = jnp.float32
BF16 = jnp.bfloat16

EPS = 1e-6
MASK_VALUE = -1e30
LOG2E = math.log2(math.e)
LANES = 128
SUBLANES = 8
VMEM_LIMIT_BYTES = 56 * 1024 * 1024

DA_HEADS = 8
DA_HEAD_DIM = 64
DN_HEADS = 8
DN_DIM = 128
DN_CONV = 4
DN_CHUNK = 64
DN_PAIR = 2 * DN_CHUNK
CA_HEADS = 4
CA_HEAD_DIM = 256
REL_BUCKETS = 32
REL_MAX_EXACT = 16
REL_MAX_DIST = 128

DN_PROJ_SLICE = 256
GDN_STEP_PAIRS = 4
ATTN_TILE = 256
ATTN_QTILES = 4
ATTN_TRIP_TILES = 16
SCORE_LIMIT = 60.0


def _params(*sem):
    return pltpu.CompilerParams(dimension_semantics=sem, vmem_limit_bytes=VMEM_LIMIT_BYTES)


def _mm(a, b):
    return jnp.dot(a.astype(BF16), b.astype(BF16), preferred_element_type=F32)


def _mm_nt(a, b):
    return lax.dot_general(a.astype(BF16), b.astype(BF16), (((1,), (1,)), ((), ())),
                           preferred_element_type=F32)


def _mm_exact_rhs(m01, a):
    hi = a.astype(BF16)
    r1 = a - hi.astype(F32)
    mid = r1.astype(BF16)
    lo = (r1 - mid.astype(F32)).astype(BF16)
    dot = functools.partial(jnp.dot, preferred_element_type=F32)
    return dot(m01, hi) + dot(m01, mid) + dot(m01, lo)


def _rms(x, axis):
    return x * lax.rsqrt(jnp.mean(x * x, axis=axis, keepdims=True) + EPS)


def _sigmoid(x):
    return 1.0 / (1.0 + jnp.exp(-x))


def _silu(x):
    h = 0.5 * x
    return h + h * jnp.tanh(h)


def _normed(x_ref, g_ref):
    return (_rms(x_ref[...], -1) * g_ref[...]).astype(BF16)


def _norm_matmul_body(x_ref, g_ref, *refs, chunk):
    n_out = len(refs) // 2
    n = _normed(x_ref, g_ref)
    for w_ref, o_ref in zip(refs[:n_out], refs[n_out:]):
        cols = w_ref.shape[1]
        step = min(chunk, cols)
        for c0 in range(0, cols, step):
            o_ref[:, c0:c0 + step] = jnp.dot(n, w_ref[:, c0:c0 + step],
                                             preferred_element_type=F32).astype(o_ref.dtype)


def _norm_matmul(x2d, g_row, weights, out_dtypes, tm):
    t, d = x2d.shape
    return pl.pallas_call(
        functools.partial(_norm_matmul_body, chunk=1024),
        grid=(t // tm,),
        in_specs=[pl.BlockSpec((tm, d), lambda i: (i, 0)),
                  pl.BlockSpec((1, d), lambda i: (0, 0))]
                 + [pl.BlockSpec(w.shape, lambda i: (0, 0)) for w in weights],
        out_specs=[pl.BlockSpec((tm, w.shape[1]), lambda i: (i, 0)) for w in weights],
        out_shape=[jax.ShapeDtypeStruct((t, w.shape[1]), dt) for w, dt in zip(weights, out_dtypes)],
        compiler_params=_params("parallel"),
        name="norm_matmul",
    )(x2d, g_row, *weights)


def _norm_matmul_t_body(x_ref, g_ref, wt_ref, o_ref, *, chunk):
    n = _normed(x_ref, g_ref)
    for r0 in range(0, wt_ref.shape[0], chunk):
        o_ref[r0:r0 + chunk, :] = lax.dot_general(
            wt_ref[r0:r0 + chunk, :], n, (((1,), (1,)), ((), ())),
            preferred_element_type=F32).astype(o_ref.dtype)


def _norm_matmul_t(x2d, g_row, wt, out_dtype, tm):
    t, d = x2d.shape
    rows = wt.shape[0]
    return pl.pallas_call(
        functools.partial(_norm_matmul_t_body, chunk=512),
        grid=(t // tm,),
        in_specs=[pl.BlockSpec((tm, d), lambda i: (i, 0)),
                  pl.BlockSpec((1, d), lambda i: (0, 0)),
                  pl.BlockSpec((rows, d), lambda i: (0, 0))],
        out_specs=pl.BlockSpec((rows, tm), lambda i: (0, i)),
        out_shape=jax.ShapeDtypeStruct((rows, t), out_dtype),
        compiler_params=_params("parallel"),
        name="norm_matmul_t",
    )(x2d, g_row, wt)


BIAS_PREV, BIAS_DIAG, BIAS_MASKED, BIAS_NONE = 0, 1, 2, 3


def _bucket_table(tile):
    k = np.arange(tile)[:, None]
    q = np.arange(tile)[None, :]

    def bucket(dist):
        nf = np.maximum(dist, 1).astype(np.float64)
        large = REL_MAX_EXACT + np.trunc(
            np.log(nf / REL_MAX_EXACT) / math.log(REL_MAX_DIST / REL_MAX_EXACT)
            * (REL_BUCKETS - REL_MAX_EXACT)).astype(np.int64)
        large = np.minimum(large, REL_BUCKETS - 1)
        return np.where(dist < REL_MAX_EXACT, dist, large)

    prev = bucket(q - k + tile)
    diag = np.where(q >= k, bucket(np.maximum(q - k, 0)), -1)
    masked = np.full((tile, tile), -1)
    far = np.full((tile, tile), REL_BUCKETS - 1)
    return np.concatenate([prev, diag, masked, far], axis=0).astype(np.int32)


def _bias_table_body(rb_ref, bucket_ref, o_ref):
    h = pl.program_id(0)
    far = rb_ref[REL_BUCKETS - 1, h]
    bk = bucket_ref[...]
    out = jnp.full(bk.shape, MASK_VALUE, F32)
    for b in range(REL_BUCKETS):
        out = jnp.where(bk == b, (rb_ref[b, h] - far) * LOG2E, out)
    o_ref[...] = out


def _bias_table(rel_bias, tile):
    heads = rel_bias.shape[1]
    buckets = jnp.asarray(_bucket_table(tile))
    rows = buckets.shape[0]
    return pl.pallas_call(
        _bias_table_body,
        grid=(heads,),
        in_specs=[pl.BlockSpec(memory_space=pltpu.SMEM),
                  pl.BlockSpec((rows, tile), lambda h: (0, 0))],
        out_specs=pl.BlockSpec((None, rows, tile), lambda h: (h, 0, 0)),
        out_shape=jax.ShapeDtypeStruct((heads, rows, tile), F32),
        compiler_params=_params("arbitrary"),
        name="bias_table",
    )(rel_bias, buckets)


def _attn_body(lam_ref, qt_ref, k_ref, vt_ref, bias_ref, g_ref, o_ref,
               qz_ref, m_ref, x_ref, l_ref, acc_ref, p_ref, *, tile, nsub, lam_init):
    j = pl.program_id(2)
    hd = 2 * DA_HEAD_DIM
    qw = nsub * tile
    t0 = nsub * j

    sub = 2 * tile
    qt = qt_ref[...]
    row = lax.broadcasted_iota(jnp.int32, (hd, tile), 0)
    zero = jnp.zeros((hd, tile), qt.dtype)
    for a in range(nsub):
        qa = qt[:, a * tile:(a + 1) * tile]
        qz_ref[:, a * sub:a * sub + tile] = jnp.where(row < DA_HEAD_DIM, qa, zero)
        qz_ref[:, a * sub + tile:(a + 1) * sub] = jnp.where(row >= DA_HEAD_DIM, qa, zero)

    def tile_off(t):
        return pl.multiple_of(jnp.maximum(t, 0) * tile, tile)

    def bias_block(block):
        b = bias_ref[pl.ds(pl.multiple_of(block * tile, tile), tile), :]
        return jnp.concatenate([b, b], axis=1)

    def bias_cols(t):
        parts = []
        for a in range(nsub):
            rel = t - (t0 + a)
            parts.append(bias_block(jnp.where(
                rel == 0, BIAS_DIAG, jnp.where(rel == -1, BIAS_PREV, jnp.where(rel < -1, BIAS_NONE, BIAS_MASKED)))))
        return jnp.concatenate(parts, axis=1)

    def scores(off, first_sub=0, kinds=None):
        s = jnp.dot(k_ref[pl.ds(off, tile), :], qz_ref[:, first_sub * sub:], preferred_element_type=F32)
        if kinds is None:
            return s
        pieces = [s[:, i * sub:(i + 1) * sub] for i in range(len(kinds))]
        return jnp.concatenate([x if kind is None else x + bias_block(kind) for x, kind in zip(pieces, kinds)], axis=1)

    def values(off, n_keys, p):
        return jnp.dot(vt_ref[:, pl.ds(off, n_keys)], p, preferred_element_type=F32)

    def by_sublane(s):
        return s.reshape(tile // SUBLANES, SUBLANES, s.shape[1])

    def exact_tile(off, bias, first):
        s = scores(off) + bias
        c = jnp.max(s, axis=0, keepdims=True)
        if first:
            m_new = c
        else:
            m_old = m_ref[...]
            m_new = jnp.maximum(m_old, c)
            alpha = jnp.exp2(m_old - m_new)
        p = jnp.exp2(s - m_new)
        rows = jnp.sum(by_sublane(p), axis=0)
        pv = values(off, tile, p.astype(BF16))
        acc_ref[...] = pv if first else alpha * acc_ref[...] + pv
        l_ref[...] = rows if first else alpha * l_ref[...] + rows
        m_ref[...] = m_new

    def diagonal_block(with_previous):
        acc = lsum = xmax = None
        order = list(reversed(range(nsub)))
        s_of = {}
        for c in order:
            kinds = [BIAS_DIAG if a == c else BIAS_PREV if a == c + 1 else None for a in range(c, nsub)]
            s_of[c] = scores(tile_off(t0 + c), c, kinds)
        if with_previous:
            before = [scores(tile_off(t0 - 2)), scores(tile_off(t0 - 1), 0, [BIAS_PREV] + [None] * (nsub - 1))]
        for c in order:
            s = s_of[c]
            p = jnp.exp2(s)
            part = jnp.max(by_sublane(s), axis=0)
            rows = jnp.sum(by_sublane(p), axis=0)
            pv = values(tile_off(t0 + c), tile, p.astype(BF16))
            if acc is None:
                acc, lsum, xmax = pv, rows, part
            else:
                acc = jnp.concatenate([pv[:, :sub], pv[:, sub:] + acc], axis=1)
                lsum = jnp.concatenate([rows[:, :sub], rows[:, sub:] + lsum], axis=1)
                xmax = jnp.concatenate([part[:, :sub], jnp.maximum(part[:, sub:], xmax)], axis=1)
        if with_previous:
            for i, s in enumerate(before):
                p = jnp.exp2(s)
                xmax = jnp.maximum(xmax, jnp.max(by_sublane(s), axis=0))
                lsum = lsum + jnp.sum(by_sublane(p), axis=0)
                p_ref[i * tile:(i + 1) * tile, :] = p.astype(BF16)
            acc = acc + values(tile_off(t0 - 2), 2 * tile, p_ref[:2 * tile, :])
        acc_ref[...], l_ref[...], x_ref[...] = acc, lsum, xmax

    def far_tiles(start, count):
        cpart = None
        lpart = None
        for i in range(count):
            s = scores(tile_off(start + i))
            part = jnp.max(by_sublane(s), axis=0)
            cpart = part if cpart is None else jnp.maximum(cpart, part)
            p = jnp.exp2(s)
            rows = jnp.sum(by_sublane(p), axis=0)
            lpart = rows if lpart is None else lpart + rows
            p_ref[i * tile:(i + 1) * tile, :] = p.astype(BF16)
        n_keys = count * tile
        acc_ref[...] = acc_ref[...] + values(tile_off(start), n_keys, p_ref[:n_keys, :])
        l_ref[...] = l_ref[...] + lpart
        x_ref[...] = jnp.maximum(x_ref[...], cpart)

    def finish():
        lv = lam_ref[...]
        lam = (jnp.exp(jnp.sum(lv[0:1] * lv[1:2], axis=-1, keepdims=True))
               - jnp.exp(jnp.sum(lv[2:3] * lv[3:4], axis=-1, keepdims=True)) + lam_init)
        acc = acc_ref[...] * (1.0 / jnp.sum(l_ref[...], axis=0, keepdims=True))
        ot = jnp.concatenate([acc[:, a * sub:a * sub + tile] - lam * acc[:, a * sub + tile:(a + 1) * sub]
                              for a in range(nsub)], axis=1)
        o = _rms(ot, 0).T * g_ref[...] * (1.0 - lam_init)
        o_ref[...] = o.astype(o_ref.dtype)

    assert nsub % 2 == 0
    n_even = jnp.maximum(t0 - 2, 0)

    @pl.when(j == 0)
    def _():
        diagonal_block(with_previous=False)

    @pl.when(j >= 1)
    def _():
        diagonal_block(with_previous=True)

    def trip(u, carry):
        far_tiles(u * ATTN_TRIP_TILES, ATTN_TRIP_TILES)
        return carry

    n_trips = n_even // ATTN_TRIP_TILES
    lax.fori_loop(0, n_trips, trip, 0)
    done = n_trips * ATTN_TRIP_TILES
    count = ATTN_TRIP_TILES // 2
    while count >= 2:
        take = (n_even - done) >= count

        @pl.when(take)
        def _(done=done, count=count):
            far_tiles(done, count)

        done = done + jnp.where(take, count, 0)
        count //= 2
    finish()

    col_max = jnp.max(x_ref[...], axis=0, keepdims=True)

    @pl.when(jnp.max(jnp.abs(col_max)) > SCORE_LIMIT)
    def _():
        exact_tile(tile_off(t0), bias_cols(t0), first=True)

        def body(u, carry):
            t = u + jnp.where(u >= t0, 1, 0)
            exact_tile(tile_off(t), bias_cols(t), first=False)
            return carry

        lax.fori_loop(0, t0 + nsub - 1, body, 0)
        finish()


def _diff_attention(lam_vecs, qvt, kdn, bias, g_row, batch, seq, lam_init):
    tile = ATTN_TILE
    qw = ATTN_QTILES * tile
    assert seq % qw == 0
    nqt = seq // qw
    hd = 2 * DA_HEAD_DIM
    return pl.pallas_call(
        functools.partial(_attn_body, tile=tile, nsub=ATTN_QTILES, lam_init=lam_init),
        grid=(batch, DA_HEADS, nqt),
        in_specs=[pl.BlockSpec((4, DA_HEAD_DIM), lambda b, h, i: (0, 0)),
                  pl.BlockSpec((hd, qw), lambda b, h, i: (h, b * nqt + i)),
                  pl.BlockSpec((seq, hd), lambda b, h, i: (b, h)),
                  pl.BlockSpec((hd, seq), lambda b, h, i: (DA_HEADS + h, b)),
                  pl.BlockSpec((None,) + bias.shape[1:], lambda b, h, i: (h, 0, 0)),
                  pl.BlockSpec((1, hd), lambda b, h, i: (0, 0))],
        out_specs=pl.BlockSpec((qw, hd), lambda b, h, i: (b * nqt + i, h)),
        out_shape=jax.ShapeDtypeStruct((batch * seq, DA_HEADS * hd), BF16),
        scratch_shapes=[pltpu.VMEM((hd, 2 * qw), BF16),
                        pltpu.VMEM((1, 2 * qw), F32),
                        pltpu.VMEM((SUBLANES, 2 * qw), F32),
                        pltpu.VMEM((SUBLANES, 2 * qw), F32),
                        pltpu.VMEM((hd, 2 * qw), F32),
                        pltpu.VMEM((ATTN_TRIP_TILES * tile, 2 * qw), BF16)],
        compiler_params=_params("parallel", "parallel", "arbitrary"),
        name="diff_attn",
    )(lam_vecs, qvt, kdn, qvt, bias, g_row)


def _dn_proj_body(x_ref, g_ref, w_ref, cw_ref, o_ref, xq_ref, xk_ref, xv_ref, tail_ref, *, tiles_per_seq):
    assert DN_CONV == 4
    width = DN_HEADS * DN_DIM
    rows = x_ref.shape[0]

    @pl.when(pl.program_id(0) % tiles_per_seq == 0)
    def _():
        for x_scr in (xq_ref, xk_ref, xv_ref):
            x_scr[:SUBLANES, :] = jnp.zeros((SUBLANES, width), F32)

    @pl.when(pl.program_id(0) % tiles_per_seq != 0)
    def _():
        for which, x_scr in enumerate((xq_ref, xk_ref, xv_ref)):
            x_scr[:SUBLANES, :] = tail_ref[which]

    n = _normed(x_ref, g_ref)

    def project(c0, ncols):
        return jnp.dot(n, w_ref[:, c0:c0 + ncols], preferred_element_type=F32)

    step = DN_PROJ_SLICE
    plain = [g * width + c0 for g in (0, 4) for c0 in range(0, width, step)]
    slices = [(which, c0) for which in range(3) for c0 in range(0, width, step)]
    for i, (which, c0) in enumerate(slices):
        x_scr = (xq_ref, xk_ref, xv_ref)[which]
        xp = project((1 + which) * width + c0, step)
        x_scr[SUBLANES:, c0:c0 + step] = xp
        tail_ref[which, :, c0:c0 + step] = xp[rows - SUBLANES:]
        j = i - (len(slices) - len(plain))
        if j >= 0:
            o_ref[:, plain[j]:plain[j] + step] = project(plain[j], step).astype(o_ref.dtype)
        for h in range(c0 // DN_DIM, (c0 + step) // DN_DIM):
            cols = slice(h * DN_DIM, (h + 1) * DN_DIM)
            w = 0.5 * cw_ref[:, which * width + h * DN_DIM:which * width + (h + 1) * DN_DIM]
            xe = x_scr[:, cols]
            xe1 = pltpu.roll(xe, 1, 0)
            u = w[1:2] * xe + w[0:1] * xe1
            y = w[3:4] * xe[SUBLANES:] + w[2:3] * xe1[SUBLANES:] + pltpu.roll(u, 2, 0)[SUBLANES:]
            y = y + y * jnp.tanh(y)
            if which < 2:
                inv = lax.rsqrt(jnp.sum(y * y, axis=-1, keepdims=True) + EPS)
                y = y * (inv * (DN_DIM ** -0.5) if which == 0 else inv)
            c0 = (1 + which) * width + h * DN_DIM
            o_ref[:, c0:c0 + DN_DIM] = y.astype(o_ref.dtype)


def _dn_proj(x2d, g_row, w_kdn, conv_w, seq, tm):
    t, d = x2d.shape
    width = DN_HEADS * DN_DIM
    x_scr = pltpu.VMEM((SUBLANES + tm, width), F32)
    return pl.pallas_call(
        functools.partial(_dn_proj_body, tiles_per_seq=seq // tm),
        grid=(t // tm,),
        in_specs=[pl.BlockSpec((tm, d), lambda i: (i, 0)),
                  pl.BlockSpec((1, d), lambda i: (0, 0)),
                  pl.BlockSpec(w_kdn.shape, lambda i: (0, 0)),
                  pl.BlockSpec(conv_w.shape, lambda i: (0, 0))],
        out_specs=pl.BlockSpec((tm, w_kdn.shape[1]), lambda i: (i, 0)),
        out_shape=jax.ShapeDtypeStruct((t, w_kdn.shape[1]), BF16),
        scratch_shapes=[x_scr, x_scr, x_scr, pltpu.VMEM((3, SUBLANES, width), F32)],
        compiler_params=_params("arbitrary"),
        name="dn_proj",
    )(x2d, g_row, w_kdn, conv_w)


def _gdn_body(q_ref, k_ref, v_ref, z_ref, ba_ref, alog_ref, dtb_ref, gn_ref, o_ref, state_ref, *, npairs):
    p = DN_PAIR
    c = DN_CHUNK
    heads = range(DN_HEADS)
    head_cols = [slice(h * DN_DIM, (h + 1) * DN_DIM) for h in heads]

    @pl.when(pl.program_id(1) == 0)
    def _():
        state_ref[...] = jnp.zeros(state_ref.shape, F32)

    ri = lax.broadcasted_iota(jnp.int32, (p, p), 0)
    ci = lax.broadcasted_iota(jnp.int32, (p, p), 1)
    same = (ri // c) == (ci // c)
    causal = same & (ri >= ci)
    strict = same & (ri > ci)
    cum01 = jnp.where(causal, 1.0, 0.0).astype(BF16)

    def lane(x, j):
        return jnp.broadcast_to(x[:, j:j + 1], (x.shape[0], DN_DIM))

    first_chunk_lanes = lax.broadcasted_iota(jnp.int32, (c, p), 1) < c

    def side_by_side(bd):
        return jnp.where(first_chunk_lanes, bd[:c], bd[c:])

    def block_diagonal(ss):
        return jnp.concatenate([jnp.where(first_chunk_lanes, ss, 0.0), jnp.where(first_chunk_lanes, 0.0, ss)], axis=0)

    def prepare(pr):
        tok = slice(pr * p, (pr + 1) * p)
        ba = ba_ref[tok, :]
        beta_all = _sigmoid(ba)
        xs = ba + dtb_ref[...]
        softplus = jnp.maximum(xs, 0.0) + jnp.log(1.0 + jnp.exp(-jnp.abs(xs)))
        g_all = -jnp.exp(alog_ref[...]) * softplus
        gc_all = _mm_exact_rhs(cum01, g_all)
        glast_all = jnp.concatenate(
            [jnp.broadcast_to(gc_all[c - 1:c], (c, LANES)), jnp.broadcast_to(gc_all[p - 1:p], (c, LANES))], axis=0)
        tail_all = jnp.exp(glast_all - gc_all)
        dl_all = jnp.exp(glast_all)

        q = [q_ref[tok, cols].astype(F32) for cols in head_cols]
        k = [k_ref[tok, cols].astype(F32) for cols in head_cols]
        v = [v_ref[tok, cols].astype(F32) for cols in head_cols]
        gc = [lane(gc_all, DN_HEADS + h) for h in heads]
        gc_all_t = gc_all.T
        gr = [jnp.broadcast_to(gc_all_t[DN_HEADS + h:DN_HEADS + h + 1], (p, p)) for h in heads]
        decay = [jnp.exp(jnp.where(causal, gc[h] - gr[h], MASK_VALUE)) for h in heads]
        exp_g = [jnp.exp(x) for x in gc]
        beta = [lane(beta_all, h) for h in heads]
        kb = [k[h] * beta[h] for h in heads]
        vb = [v[h] * beta[h] for h in heads]
        kq = [_mm_nt(jnp.concatenate([kb[h], q[h]], axis=0), k[h]) for h in heads]
        yield None
        low = [jnp.where(strict, kq[h][:p] * decay[h], 0.0) for h in heads]
        intra = [jnp.where(causal, kq[h][p:] * decay[h], 0.0) for h in heads]
        y = [-side_by_side(x) for x in low]
        pw = [_mm(side_by_side(x), x) for x in low]
        yield None
        for it in range(5):
            pw_bd = [block_diagonal(x) for x in pw]
            if it < 4:
                both = [_mm(jnp.concatenate([y[h], pw[h]], axis=0), pw_bd[h]) for h in heads]
                y = [y[h] + pw[h] + both[h][:c] for h in heads]
                pw = [both[h][c:] for h in heads]
            else:
                y = [y[h] + pw[h] + _mm(y[h], pw_bd[h]) for h in heads]
            yield None
        rhs = [jnp.concatenate([vb[h], kb[h] * exp_g[h]], axis=1) for h in heads]
        sol = [rhs[h] + _mm(block_diagonal(y[h]), rhs[h]) for h in heads]
        q_dec = [q[h] * exp_g[h] for h in heads]
        tail_all_t = tail_all.T
        kt_t = [k[h].T * jnp.broadcast_to(tail_all_t[DN_HEADS + h:DN_HEADS + h + 1], (DN_DIM, p))
                for h in heads]
        yield sol, intra, q_dec, kt_t, dl_all

    zeros = jnp.zeros((c, DN_DIM), F32)

    def advance(pr, operands):
        sol, intra, q_dec, kt_t, dl_all = operands
        tok = slice(pr * p, (pr + 1) * p)
        outs = [[], []]
        for ch in range(2):
            rows = slice(ch * c, (ch + 1) * c)
            st = [state_ref[h] for h in heads]
            on_state = [_mm(jnp.concatenate([sol[h][rows, DN_DIM:], q_dec[h][rows]], axis=0), st[h]) for h in heads]
            yield None
            v_new = [sol[h][rows, :DN_DIM] - on_state[h][:c] for h in heads]
            v_pad = [jnp.concatenate([x, zeros] if ch == 0 else [zeros, x], axis=0) for x in v_new]
            on_v = [_mm(jnp.concatenate([intra[h][rows], kt_t[h]], axis=0), v_pad[h]) for h in heads]
            outs[ch] = [on_state[h][c:] + on_v[h][:c] for h in heads]
            for h in heads:
                dl = jnp.broadcast_to(dl_all[ch * c:ch * c + 1, DN_HEADS + h:DN_HEADS + h + 1], (1, DN_DIM))
                state_ref[h] = st[h] * dl + on_v[h][c:]
            yield None
        for h in heads:
            o = jnp.concatenate([outs[0][h], outs[1][h]], axis=0)
            yb = _rms(o, -1) * gn_ref[...] * _silu(z_ref[tok, head_cols[h]].astype(F32))
            o_ref[tok, head_cols[h]] = yb.astype(o_ref.dtype)

    def drain(gen):
        result = None
        for item in gen:
            result = item if item is not None else result
        return result

    operands = drain(prepare(0))
    for pr in range(npairs):
        chain = advance(pr, operands)
        nxt = prepare(pr + 1) if pr + 1 < npairs else iter(())
        operands = None
        chain_done = False
        while not chain_done:
            for _ in range(2):
                item = next(nxt, None)
                operands = item if item is not None else operands
            chain_done = next(chain, "done") == "done"
        rest = drain(nxt)
        operands = rest if rest is not None else operands


def _gdn(kdn, ba, alog_row, dtb_row, gn_row, batch, seq):
    t = batch * seq
    p = GDN_STEP_PAIRS * DN_PAIR
    nt = seq // p
    width = DN_HEADS * DN_DIM
    tok = lambda col: pl.BlockSpec((p, width), lambda b, i: (b * nt + i, col))
    row = pl.BlockSpec((1, LANES), lambda b, i: (0, 0))
    return pl.pallas_call(
        functools.partial(_gdn_body, npairs=GDN_STEP_PAIRS),
        grid=(batch, nt),
        in_specs=[tok(1), tok(2), tok(3), tok(4),
                  pl.BlockSpec((p, LANES), lambda b, i: (b * nt + i, 0)),
                  row, row, row],
        out_specs=pl.BlockSpec((p, width), lambda b, i: (b * nt + i, 0)),
        out_shape=jax.ShapeDtypeStruct((t, width), BF16),
        scratch_shapes=[pltpu.VMEM((DN_HEADS, DN_DIM, DN_DIM), F32)],
        compiler_params=_params("parallel", "arbitrary"),
        name="gdn",
    )(kdn, kdn, kdn, kdn, ba, alog_row, dtb_row, gn_row)


def _merge_body(x_ref, cq_ref, g0_ref, g1_ref, g2_ref, ya_ref, yb_ref, mkt_ref, mv_ref,
                wb_ref, wo_ref, ln_ref, o_ref):
    cq = cq_ref[...]
    scores = [jnp.dot(cq[:, h * CA_HEAD_DIM:(h + 1) * CA_HEAD_DIM], mkt_ref[h], preferred_element_type=F32)
              for h in range(CA_HEADS)]
    proj_a = jnp.dot(ya_ref[...], wb_ref[0], preferred_element_type=F32)
    proj_b = jnp.dot(yb_ref[...], wb_ref[1], preferred_element_type=F32)
    heads = []
    for h in range(CA_HEADS):
        s = scores[h] * (CA_HEAD_DIM ** -0.5)
        e = jnp.exp(s - jnp.max(s, axis=-1, keepdims=True))
        pr = e / jnp.sum(e, axis=-1, keepdims=True)
        heads.append(jnp.dot(pr.astype(BF16), mv_ref[h], preferred_element_type=F32))
    yc = jnp.concatenate(heads, axis=1)

    mixed = _sigmoid(g0_ref[...].astype(F32)) * proj_a
    mixed = mixed + _sigmoid(g1_ref[...].astype(F32)) * proj_b
    mixed = mixed + _sigmoid(g2_ref[...].astype(F32)) * _mm(yc, wb_ref[2])
    out = _mm(mixed, wo_ref[...])
    o_ref[...] = x_ref[...] + _rms(out, -1) * ln_ref[...]


def _merge(x2d, cg, ya, yb, mkt, mv, wb, wo, ln_row, batch, seq, tm):
    t, d = x2d.shape
    nt = seq // tm
    tok = lambda col: pl.BlockSpec((tm, d), lambda i: (i, col))
    mem = pl.BlockSpec((None, CA_HEADS, CA_HEAD_DIM, CA_HEAD_DIM), lambda i: (i // nt, 0, 0, 0))
    return pl.pallas_call(
        _merge_body,
        grid=(t // tm,),
        in_specs=[tok(0), tok(0), tok(1), tok(2), tok(3), tok(0), tok(0), mem, mem,
                  pl.BlockSpec((3, d, d), lambda i: (0, 0, 0)),
                  pl.BlockSpec((d, d), lambda i: (0, 0)),
                  pl.BlockSpec((1, d), lambda i: (0, 0))],
        out_specs=tok(0),
        out_shape=jax.ShapeDtypeStruct((t, d), F32),
        compiler_params=_params("parallel"),
        name="merge",
    )(x2d, cg, cg, cg, cg, ya, yb, mkt, mv, wb, wo, ln_row)


def _ffn_body(h_ref, pre_ref, post_ref, w1_ref, w2_ref, o_ref, *, chunk):
    hh = h_ref[...]
    n = (_rms(hh, -1) * pre_ref[...]).astype(BF16)
    d_ff = w1_ref.shape[1]
    f = jnp.zeros(hh.shape, F32)
    for c0 in range(0, d_ff, chunk):
        a = jnp.dot(n, w1_ref[:, c0:c0 + chunk], preferred_element_type=F32)
        a = jnp.square(jnp.maximum(a, 0.0))
        f = f + jnp.dot(a.astype(BF16), w2_ref[c0:c0 + chunk, :], preferred_element_type=F32)
    o_ref[...] = hh + _rms(f, -1) * post_ref[...]


def _ffn(h2d, pre_row, post_row, w1, w2, tm):
    t, d = h2d.shape
    d_ff = w1.shape[1]
    return pl.pallas_call(
        functools.partial(_ffn_body, chunk=1024),
        grid=(t // tm,),
        in_specs=[pl.BlockSpec((tm, d), lambda i: (i, 0)),
                  pl.BlockSpec((1, d), lambda i: (0, 0)),
                  pl.BlockSpec((1, d), lambda i: (0, 0)),
                  pl.BlockSpec((d, d_ff), lambda i: (0, 0)),
                  pl.BlockSpec((d_ff, d), lambda i: (0, 0))],
        out_specs=pl.BlockSpec((tm, d), lambda i: (i, 0)),
        out_shape=jax.ShapeDtypeStruct((t, d), F32),
        compiler_params=_params("parallel"),
        name="ffn",
    )(h2d, pre_row, post_row, w1, w2)


def _pad_lanes(vec, offset):
    return jnp.zeros((1, LANES), F32).at[0, offset:offset + vec.shape[0]].set(vec.astype(F32))


def _layer(h2d, mem2d, rel_bias, w_in, conv_w, dn_a_log, dn_dt_bias, dn_norm_g, da_lambda,
           da_subln_g, mem_norm_g, w_mem_kv, w_branch, w_out, ln_mix_pre, ln_mix_post,
           ln_ff_pre, ln_ff_post, w_ff1, w_ff2, batch, seq, mem_len, layer):
    d = h2d.shape[1]
    lam_init = 0.8 - 0.6 * math.exp(-0.3 * layer)
    tm = min(512, seq)
    row = lambda v: v.reshape(1, -1).astype(F32)

    small0 = 7 * d
    rest0 = small0 + 2 * DN_HEADS
    w_q = w_in[:, :d] * (DA_HEAD_DIM ** -0.5 * LOG2E)
    w_qv_t = jnp.concatenate([w_q, w_in[:, 2 * d:3 * d]], axis=1).T.astype(BF16)
    w_kdn = jnp.concatenate([w_in[:, d:2 * d], w_in[:, 3 * d:small0]], axis=1).astype(BF16)
    w_small = jnp.zeros((d, LANES), F32).at[:, :2 * DN_HEADS].set(w_in[:, small0:rest0]).astype(BF16)
    w_rest = w_in[:, rest0:].astype(BF16)
    g_pre = row(ln_mix_pre)

    qvt = _norm_matmul_t(h2d, g_pre, w_qv_t, BF16, tm)
    kdn = _dn_proj(h2d, g_pre, w_kdn, conv_w.astype(F32), seq, tm)
    cg, ba = _norm_matmul(h2d, g_pre, [w_rest, w_small], [BF16, F32], tm)
    (mkv,) = _norm_matmul(mem2d, row(mem_norm_g), [w_mem_kv.astype(BF16)], [BF16], min(512, batch * mem_len))

    bias = _bias_table(rel_bias.astype(F32), ATTN_TILE)
    y_a = _diff_attention(da_lambda.astype(F32), qvt, kdn, bias, row(da_subln_g), batch, seq, lam_init)

    y_b = _gdn(kdn, ba, _pad_lanes(dn_a_log, DN_HEADS), _pad_lanes(dn_dt_bias, DN_HEADS),
               row(dn_norm_g), batch, seq)

    mkt = mkv[:, :d].reshape(batch, mem_len, CA_HEADS, CA_HEAD_DIM).transpose(0, 2, 3, 1)
    mv = mkv[:, d:].reshape(batch, mem_len, CA_HEADS, CA_HEAD_DIM).transpose(0, 2, 1, 3)
    h1 = _merge(h2d, cg, y_a, y_b, mkt, mv, w_branch.astype(BF16), w_out.astype(BF16),
                row(ln_mix_post), batch, seq, min(512, seq))

    return _ffn(h1, row(ln_ff_pre), row(ln_ff_post), w_ff1.astype(BF16), w_ff2.astype(BF16), min(512, seq))


def kernel(x, mem, rel_bias, w_in, conv_w, dn_a_log, dn_dt_bias, dn_norm_g, da_lambda, da_subln_g,
           mem_norm_g, w_mem_kv, w_branch, w_out, ln_mix_pre, ln_mix_post, ln_ff_pre, ln_ff_post,
           w_ff1, w_ff2):
    batch, seq, d = x.shape
    mem_len = mem.shape[1]
    assert seq % (ATTN_QTILES * ATTN_TILE) == 0 and seq % DN_PAIR == 0
    h2d = x.reshape(batch * seq, d)
    mem2d = mem.reshape(batch * mem_len, d)
    for layer in range(w_in.shape[0]):
        h2d = _layer(h2d, mem2d, rel_bias, w_in[layer], conv_w[layer], dn_a_log[layer], dn_dt_bias[layer],
                     dn_norm_g[layer], da_lambda[layer], da_subln_g[layer], mem_norm_g[layer],
                     w_mem_kv[layer], w_branch[layer], w_out[layer], ln_mix_pre[layer], ln_mix_post[layer],
                     ln_ff_pre[layer], ln_ff_post[layer], w_ff1[layer], w_ff2[layer],
                     batch, seq, mem_len, layer)
    return h2d.reshape(batch, seq, d)
```
